```python
import jax, jax.numpy as jnp
from jax import lax
import numpy as np

D_MODEL = 2048
BATCH = 8
SEQ = 8192
DEPTH = 1

CHUNK = 64
D_MIX = D_MODEL
CONV_WIDTH = D_MIX // 2
CONV_HEADS = 8
CONV_HEAD_DIM = CONV_WIDTH // CONV_HEADS
CONV_K = 3
POOL_WIDTH = D_MIX - CONV_WIDTH
POOL_WINDOWS = (2, 4, 8, 16)
N_POOL_GROUPS = len(POOL_WINDOWS)
POOL_GROUP_DIM = POOL_WIDTH // N_POOL_GROUPS
IN_PROJ_WIDTH = 3 * CONV_WIDTH + POOL_WIDTH
D_FF = ((8 * D_MODEL // 3 + 255) // 256) * 256
EPS = 1e-6

kernel_name = "hybrid_shortconv_multiscale_pool_block"


def rms_norm(x, g):
    xf = x.astype(jnp.float32)
    y = xf * lax.rsqrt(jnp.mean(xf * xf, axis=-1, keepdims=True) + EPS)
    return (y * g.astype(jnp.float32)).astype(x.dtype)


def rms_norm_plain(x):
    xf = x.astype(jnp.float32)
    y = xf * lax.rsqrt(jnp.mean(xf * xf, axis=-1, keepdims=True) + EPS)
    return y.astype(x.dtype)


def short_conv_causal(u, w):
    c = u.shape[-1]
    rhs = w[:, None, :].astype(u.dtype)
    return lax.conv_general_dilated(
        u, rhs, window_strides=(1,), padding=[(CONV_K - 1, 0)],
        dimension_numbers=("NWC", "WIO", "NWC"), feature_group_count=c)


def multiscale_pool_causal(v):
    bn, s, _ = v.shape
    vg = v.reshape(bn, s, N_POOL_GROUPS, POOL_GROUP_DIM).astype(jnp.float32)
    cs = jnp.cumsum(vg, axis=1)
    pos = jnp.arange(1, s + 1, dtype=jnp.float32)
    outs = []
    for gi, w in enumerate(POOL_WINDOWS):
        c = cs[:, :, gi]
        prev = jnp.pad(c, ((0, 0), (w, 0), (0, 0)))[:, :s]
        cnt = jnp.minimum(pos, float(w))[None, :, None]
        outs.append((c - prev) / cnt - vg[:, :, gi])
    return jnp.stack(outs, axis=2)


def _fwd_setup_inputs(seed: int = 0) -> dict:
    key = jax.random.key(seed)
    ks = jax.random.split(key, 16)
    L = DEPTH

    def nrm(k, shape, fan_in):
        return jax.random.normal(k, shape, jnp.float32) * (fan_in ** -0.5)

    def gain(k, shape):
        return 1.0 + 0.05 * jax.random.normal(k, shape, jnp.float32)

    return {
        "x": jax.random.normal(ks[0], (BATCH, SEQ, D_MODEL), jnp.float32),
        "ln_mix_pre": gain(ks[1], (L, D_MODEL)),
        "w_in": nrm(ks[2], (L, D_MODEL, IN_PROJ_WIDTH), D_MODEL),
        "conv_w": nrm(ks[3], (L, CONV_K, CONV_WIDTH), CONV_K),
        "pool_w": nrm(ks[4], (L, N_POOL_GROUPS, POOL_GROUP_DIM, POOL_GROUP_DIM), POOL_GROUP_DIM),
        "pool_scale": gain(ks[5], (L, POOL_WIDTH)),
        "w_out": nrm(ks[6], (L, D_MIX, D_MODEL), D_MIX),
        "ln_mix_post": gain(ks[7], (L, D_MODEL)),
        "ln_ffn_pre": gain(ks[8], (L, D_MODEL)),
        "w_gate": nrm(ks[9], (L, D_MODEL, D_FF), D_MODEL),
        "w_up": nrm(ks[10], (L, D_MODEL, D_FF), D_MODEL),
        "w_down": nrm(ks[11], (L, D_FF, D_MODEL), D_FF),
        "ln_ffn_post": gain(ks[12], (L, D_MODEL)),
    }


def _fwd_reference(x, ln_mix_pre, w_in, conv_w, pool_w, pool_scale, w_out, ln_mix_post,
              ln_ffn_pre, w_gate, w_up, w_down, ln_ffn_post):
    bn, s, _ = x.shape
    for l in range(DEPTH):
        h = rms_norm(x, ln_mix_pre[l])
        proj = jnp.einsum("bsd,de->bse", h, w_in[l])
        gate_b, gate_c, u, v = jnp.split(
            proj, [CONV_WIDTH, 2 * CONV_WIDTH, 3 * CONV_WIDTH], axis=-1)

        y_conv = gate_b * short_conv_causal(gate_c * u, conv_w[l])
        y_conv = rms_norm_plain(y_conv.reshape(bn, s, CONV_HEADS, CONV_HEAD_DIM))
        y_conv = y_conv.reshape(bn, s, CONV_WIDTH)

        pooled = multiscale_pool_causal(v).astype(v.dtype)
        y_pool = jnp.einsum("bsgc,gcd->bsgd", pooled, pool_w[l])
        y_pool = rms_norm_plain(y_pool).reshape(bn, s, POOL_WIDTH) * pool_scale[l]

        mixed = jnp.concatenate([y_conv, y_pool], axis=-1)
        mix_out = jnp.einsum("bse,ed->bsd", mixed, w_out[l])
        x = x + rms_norm(mix_out, ln_mix_post[l])

        hf = rms_norm(x, ln_ffn_pre[l])
        g = jnp.einsum("bsd,df->bsf", hf, w_gate[l])
        up = jnp.einsum("bsd,df->bsf", hf, w_up[l])
        ff = jnp.einsum("bsf,fd->bsd", jax.nn.silu(g) * up, w_down[l])
        x = x + rms_norm(ff, ln_ffn_post[l])
    return x


import jax as _jax
import jax.numpy as _jnp

TWIN_FORMAT = 'train_step'
FWD_PARAMS = ['x', 'ln_mix_pre', 'w_in', 'conv_w', 'pool_w', 'pool_scale', 'w_out', 'ln_mix_post', 'ln_ffn_pre', 'w_gate', 'w_up', 'w_down', 'ln_ffn_post']
TWIN_WEIGHTS = ['ln_mix_pre', 'w_in', 'conv_w', 'pool_w', 'pool_scale', 'w_out', 'ln_mix_post', 'ln_ffn_pre', 'w_gate', 'w_up', 'w_down', 'ln_ffn_post']
TWIN_DIFF_INPUT = 'x'
TWIN_INPUTS = ['x', 'ln_mix_pre', 'w_in', 'conv_w', 'pool_w', 'pool_scale', 'w_out', 'ln_mix_post', 'ln_ffn_pre', 'w_gate', 'w_up', 'w_down', 'ln_ffn_post', 'loss_target', 'm_ln_mix_pre', 'm_w_in', 'm_conv_w', 'm_pool_w', 'm_pool_scale', 'm_w_out', 'm_ln_mix_post', 'm_ln_ffn_pre', 'm_w_gate', 'm_w_up', 'm_w_down', 'm_ln_ffn_post', 'v_ln_mix_pre', 'v_w_in', 'v_conv_w', 'v_pool_w', 'v_pool_scale', 'v_w_out', 'v_ln_mix_post', 'v_ln_ffn_pre', 'v_w_gate', 'v_w_up', 'v_w_down', 'v_ln_ffn_post']
TWIN_OUTPUTS = ['loss', 'grad_x', 'grad_ln_mix_pre', 'grad_w_in', 'grad_conv_w', 'grad_pool_w', 'grad_pool_scale', 'grad_w_out', 'grad_ln_mix_post', 'grad_ln_ffn_pre', 'grad_w_gate', 'grad_w_up', 'grad_w_down', 'grad_ln_ffn_post', 'delta_ln_mix_pre', 'delta_w_in', 'delta_conv_w', 'delta_pool_w', 'delta_pool_scale', 'delta_w_out', 'delta_ln_mix_post', 'delta_ln_ffn_pre', 'delta_w_gate', 'delta_w_up', 'delta_w_down', 'delta_ln_ffn_post', 'new_m_ln_mix_pre', 'new_m_w_in', 'new_m_conv_w', 'new_m_pool_w', 'new_m_pool_scale', 'new_m_w_out', 'new_m_ln_mix_post', 'new_m_ln_ffn_pre', 'new_m_w_gate', 'new_m_w_up', 'new_m_w_down', 'new_m_ln_ffn_post', 'new_v_ln_mix_pre', 'new_v_w_in', 'new_v_conv_w', 'new_v_pool_w', 'new_v_pool_scale', 'new_v_w_out', 'new_v_ln_mix_post', 'new_v_ln_ffn_pre', 'new_v_w_gate', 'new_v_w_up', 'new_v_w_down', 'new_v_ln_ffn_post']
TWIN_LEAF_KINDS = {'loss': 'loss', 'grad_x': 'grad_x', 'grad_ln_mix_pre': 'grad_w', 'grad_w_in': 'grad_w', 'grad_conv_w': 'grad_w', 'grad_pool_w': 'grad_w', 'grad_pool_scale': 'grad_w', 'grad_w_out': 'grad_w', 'grad_ln_mix_post': 'grad_w', 'grad_ln_ffn_pre': 'grad_w', 'grad_w_gate': 'grad_w', 'grad_w_up': 'grad_w', 'grad_w_down': 'grad_w', 'grad_ln_ffn_post': 'grad_w', 'delta_ln_mix_pre': 'delta_w', 'delta_w_in': 'delta_w', 'delta_conv_w': 'delta_w', 'delta_pool_w': 'delta_w', 'delta_pool_scale': 'delta_w', 'delta_w_out': 'delta_w', 'delta_ln_mix_post': 'delta_w', 'delta_ln_ffn_pre': 'delta_w', 'delta_w_gate': 'delta_w', 'delta_w_up': 'delta_w', 'delta_w_down': 'delta_w', 'delta_ln_ffn_post': 'delta_w', 'new_m_ln_mix_pre': 'new_m', 'new_m_w_in': 'new_m', 'new_m_conv_w': 'new_m', 'new_m_pool_w': 'new_m', 'new_m_pool_scale': 'new_m', 'new_m_w_out': 'new_m', 'new_m_ln_mix_post': 'new_m', 'new_m_ln_ffn_pre': 'new_m', 'new_m_w_gate': 'new_m', 'new_m_w_up': 'new_m', 'new_m_w_down': 'new_m', 'new_m_ln_ffn_post': 'new_m', 'new_v_ln_mix_pre': 'new_v', 'new_v_w_in': 'new_v', 'new_v_conv_w': 'new_v', 'new_v_pool_w': 'new_v', 'new_v_pool_scale': 'new_v', 'new_v_w_out': 'new_v', 'new_v_ln_mix_post': 'new_v', 'new_v_ln_ffn_pre': 'new_v', 'new_v_w_gate': 'new_v', 'new_v_w_up': 'new_v', 'new_v_w_down': 'new_v', 'new_v_ln_ffn_post': 'new_v'}


def _forward(args):
    return _fwd_reference(*[args[k] for k in FWD_PARAMS])


def _output_shape():
    def fwd():
        inp = _fwd_setup_inputs(0)
        return _fwd_reference(*[inp[k] for k in FWD_PARAMS])
    out = _jax.eval_shape(fwd)
    return out.shape, out.dtype

N_MICROBATCH = 1
ADAM_LR = 0.001
ADAM_B1 = 0.9
ADAM_B2 = 0.999
ADAM_EPS = 1e-08
ADAM_WD = 0.01
ADAM_STEP = 10
PER_EXAMPLE_BATCH_AXIS = {'x': 0, 'loss_target': 0}
SHARED_INPUTS = []
_WEIGHT_DTYPES = {'ln_mix_pre': _jnp.float32, 'w_in': _jnp.float32, 'conv_w': _jnp.float32, 'pool_w': _jnp.float32, 'pool_scale': _jnp.float32, 'w_out': _jnp.float32, 'ln_mix_post': _jnp.float32, 'ln_ffn_pre': _jnp.float32, 'w_gate': _jnp.float32, 'w_up': _jnp.float32, 'w_down': _jnp.float32, 'ln_ffn_post': _jnp.float32}
MOMENT_SCALE = {'ln_mix_pre': 4.770642e-01, 'w_in': 3.643053e-01, 'conv_w': 3.242558e-01, 'pool_w': 5.294018e-01, 'pool_scale': 5.355566e-01, 'w_out': 4.545924e-01, 'ln_mix_post': 3.204557e+01, 'ln_ffn_pre': 3.848587e-01, 'w_gate': 1.359639e-01, 'w_up': 2.052851e-01, 'w_down': 3.410801e-01, 'ln_ffn_post': 3.203962e+01}


def _to_microbatches(a, axis):
    t = _jnp.moveaxis(a, axis, 0)
    t = t.reshape((N_MICROBATCH, t.shape[0] // N_MICROBATCH) + t.shape[1:])
    return _jnp.moveaxis(t, 1, axis + 1)


def setup_inputs(seed: int = 0) -> dict:
    inp = _fwd_setup_inputs(seed)
    key = _jax.random.fold_in(_jax.random.key(seed), 7919)
    shape, _ = _output_shape()
    out = dict(inp)
    out["loss_target"] = _jax.random.normal(_jax.random.fold_in(key, 0), shape, _jnp.float32)
    for i, name in enumerate(TWIN_WEIGHTS):
        w = inp[name].astype(_jnp.float32)
        if MOMENT_SCALE is None:
            s = _jnp.sqrt(_jnp.mean(_jnp.square(w)) + 1e-30)
        else:
            s = MOMENT_SCALE[name]
        km, kv = _jax.random.split(_jax.random.fold_in(key, i + 1))
        out[name] = w
        out["m_" + name] = s * _jax.random.normal(km, w.shape, _jnp.float32)
        out["v_" + name] = (s * s) * _jax.random.uniform(kv, w.shape, _jnp.float32, 0.5, 1.5)
    if N_MICROBATCH > 1:
        for name, axis in PER_EXAMPLE_BATCH_AXIS.items():
            out[name] = _to_microbatches(out[name], axis)
    return {'x': out['x'], 'ln_mix_pre': out['ln_mix_pre'], 'w_in': out['w_in'], 'conv_w': out['conv_w'], 'pool_w': out['pool_w'], 'pool_scale': out['pool_scale'], 'w_out': out['w_out'], 'ln_mix_post': out['ln_mix_post'], 'ln_ffn_pre': out['ln_ffn_pre'], 'w_gate': out['w_gate'], 'w_up': out['w_up'], 'w_down': out['w_down'], 'ln_ffn_post': out['ln_ffn_post'], 'loss_target': out['loss_target'], 'm_ln_mix_pre': out['m_ln_mix_pre'], 'm_w_in': out['m_w_in'], 'm_conv_w': out['m_conv_w'], 'm_pool_w': out['m_pool_w'], 'm_pool_scale': out['m_pool_scale'], 'm_w_out': out['m_w_out'], 'm_ln_mix_post': out['m_ln_mix_post'], 'm_ln_ffn_pre': out['m_ln_ffn_pre'], 'm_w_gate': out['m_w_gate'], 'm_w_up': out['m_w_up'], 'm_w_down': out['m_w_down'], 'm_ln_ffn_post': out['m_ln_ffn_post'], 'v_ln_mix_pre': out['v_ln_mix_pre'], 'v_w_in': out['v_w_in'], 'v_conv_w': out['v_conv_w'], 'v_pool_w': out['v_pool_w'], 'v_pool_scale': out['v_pool_scale'], 'v_w_out': out['v_w_out'], 'v_ln_mix_post': out['v_ln_mix_post'], 'v_ln_ffn_pre': out['v_ln_ffn_pre'], 'v_w_gate': out['v_w_gate'], 'v_w_up': out['v_w_up'], 'v_w_down': out['v_w_down'], 'v_ln_ffn_post': out['v_ln_ffn_post']}


def _loss(weights, diff, rest, loss_target):
    with _jax.named_scope("forward"):
        args = {**rest, TWIN_DIFF_INPUT: diff, **{k: w.astype(_WEIGHT_DTYPES[k]) for k, w in weights.items()}}
        y = _forward(args)
    with _jax.named_scope("loss_head"):
        err = _jnp.square(y.astype(_jnp.float32) - loss_target)
        return 0.5 * _jnp.sum(_jnp.mean(err, axis=-1)) if err.ndim else 0.5 * err


def _adamw(w, g, m, v):
    m = ADAM_B1 * m + (1.0 - ADAM_B1) * g
    v = ADAM_B2 * v + (1.0 - ADAM_B2) * _jnp.square(g)
    m_hat = m / (1.0 - ADAM_B1 ** ADAM_STEP)
    v_hat = v / (1.0 - ADAM_B2 ** ADAM_STEP)
    delta = -ADAM_LR * (m_hat / (_jnp.sqrt(v_hat) + ADAM_EPS) + ADAM_WD * w)
    return delta, m, v


def reference(x, ln_mix_pre, w_in, conv_w, pool_w, pool_scale, w_out, ln_mix_post, ln_ffn_pre, w_gate, w_up, w_down, ln_ffn_post, loss_target, m_ln_mix_pre, m_w_in, m_conv_w, m_pool_w, m_pool_scale, m_w_out, m_ln_mix_post, m_ln_ffn_pre, m_w_gate, m_w_up, m_w_down, m_ln_ffn_post, v_ln_mix_pre, v_w_in, v_conv_w, v_pool_w, v_pool_scale, v_w_out, v_ln_mix_post, v_ln_ffn_pre, v_w_gate, v_w_up, v_w_down, v_ln_ffn_post):
    given = dict(x=x, ln_mix_pre=ln_mix_pre, w_in=w_in, conv_w=conv_w, pool_w=pool_w, pool_scale=pool_scale, w_out=w_out, ln_mix_post=ln_mix_post, ln_ffn_pre=ln_ffn_pre, w_gate=w_gate, w_up=w_up, w_down=w_down, ln_ffn_post=ln_ffn_post, loss_target=loss_target, m_ln_mix_pre=m_ln_mix_pre, m_w_in=m_w_in, m_conv_w=m_conv_w, m_pool_w=m_pool_w, m_pool_scale=m_pool_scale, m_w_out=m_w_out, m_ln_mix_post=m_ln_mix_post, m_ln_ffn_pre=m_ln_ffn_pre, m_w_gate=m_w_gate, m_w_up=m_w_up, m_w_down=m_w_down, m_ln_ffn_post=m_ln_ffn_post, v_ln_mix_pre=v_ln_mix_pre, v_w_in=v_w_in, v_conv_w=v_conv_w, v_pool_w=v_pool_w, v_pool_scale=v_pool_scale, v_w_out=v_w_out, v_ln_mix_post=v_ln_mix_post, v_ln_ffn_pre=v_ln_ffn_pre, v_w_gate=v_w_gate, v_w_up=v_w_up, v_w_down=v_w_down, v_ln_ffn_post=v_ln_ffn_post)
    weights = {n: given[n] for n in TWIN_WEIGHTS}
    shared = {n: given[n] for n in SHARED_INPUTS}
    per_example = {n: given[n] for n in ['x']}
    grad_fn = _jax.value_and_grad(_loss, argnums=(0, 1))

    def one_microbatch(ex, loss_target):
        ex = dict(ex)
        diff = ex.pop(TWIN_DIFF_INPUT)
        return grad_fn(weights, diff, {**shared, **ex}, loss_target)

    if N_MICROBATCH == 1:
        loss, (grad_w, grad_x) = one_microbatch(per_example, given["loss_target"])
    else:
        def body(carry, xs):
            loss_sum, grad_sum = carry
            l_k, (gw_k, gx_k) = one_microbatch(xs[0], xs[1])
            with _jax.named_scope("update"):
                return (loss_sum + l_k, _jax.tree.map(_jnp.add, grad_sum, gw_k)), gx_k

        init = (_jnp.zeros((), _jnp.float32), _jax.tree.map(_jnp.zeros_like, weights))
        (loss, grad_w), grad_x = _jax.lax.scan(body, init, (per_example, given["loss_target"]))
    with _jax.named_scope("update"):
        delta_w, new_m, new_v = {}, {}, {}
        for n in TWIN_WEIGHTS:
            delta_w[n], new_m[n], new_v[n] = _adamw(weights[n], grad_w[n], given["m_" + n], given["v_" + n])
    return (loss, grad_x, *[grad_w[n] for n in TWIN_WEIGHTS], *[delta_w[n] for n in TWIN_WEIGHTS],
            *[new_m[n] for n in TWIN_WEIGHTS], *[new_v[n] for n in TWIN_WEIGHTS])
```

```python
import functools

import jax
import jax.numpy as jnp
from jax import lax
from jax.experimental import pallas as pl
from jax.experimental.pallas import tpu as pltpu

EPS = 1e-6
NDEV = 8
NCHIP = 4
CONV_HEADS = 8
HEAD_DIM = 128
CONV_WIDTH = CONV_HEADS * HEAD_DIM
POOL_WINDOWS = (2, 4, 8, 16)
POOL_GROUP_DIM = 256
HALO = 16

ADAM_LR = 0.001
ADAM_B1 = 0.9
ADAM_B2 = 0.999
ADAM_EPS = 1e-08
ADAM_WD = 0.01
ADAM_STEP = 10

F32 = jnp.float32
BF16 = jnp.bfloat16
VMEM_LIMIT = 58 * 1024 * 1024
MESH = pl.DeviceIdType.MESH

NT_DIMS = (((1,), (1,)), ((), ()))
TN_DIMS = (((0,), (0,)), ((), ()))


def _params(*sem):
    return pltpu.CompilerParams(dimension_semantics=sem, vmem_limit_bytes=VMEM_LIMIT)


def _rsq(v):
    return lax.rsqrt(jnp.mean(v * v, axis=-1, keepdims=True) + EPS)


def _norm_bwd(dn, n, r):
    return r * (dn - n * jnp.mean(dn * n, axis=-1, keepdims=True))


def _whole(shape):
    nd = len(shape)
    return pl.BlockSpec(shape, lambda *_: (0,) * nd, pipeline_mode=pl.Buffered(1))


def _inv_count(t0, tm, w):
    t = t0 + lax.broadcasted_iota(jnp.int32, (tm, 1), 0)
    return 1.0 / jnp.minimum(t + 1, w).astype(F32)


def _window_sum(ext, w, back):
    n = ext.shape[0]
    s, shift = ext, 1
    while shift < w:
        s = s + pltpu.roll(s, shift if back else n - shift, 0)
        shift *= 2
    return s


def _inproj(x, g1, win, tm=512):
    T, D = x.shape
    nb, _, bn = win.shape

    def body(x_ref, g_ref, w_ref, proj_ref, h_ref):
        @pl.when(pl.program_id(1) == 0)
        def _():
            xv = x_ref[...]
            h_ref[...] = (xv * _rsq(xv) * g_ref[...]).astype(BF16)

        proj_ref[...] = jnp.dot(h_ref[...], w_ref[0], preferred_element_type=F32)

    return pl.pallas_call(
        body, name="inproj", grid=(T // tm, nb),
        in_specs=[pl.BlockSpec((tm, D), lambda i, j: (i, 0)),
                  pl.BlockSpec((1, D), lambda i, j: (0, 0)),
                  pl.BlockSpec((1, D, bn), lambda i, j: (j, 0, 0))],
        out_specs=[pl.BlockSpec((tm, bn), lambda i, j: (i, j)),
                   pl.BlockSpec((tm, D), lambda i, j: (i, 0))],
        out_shape=[jax.ShapeDtypeStruct((T, nb * bn), F32), jax.ShapeDtypeStruct((T, D), BF16)],
        compiler_params=_params("arbitrary", "arbitrary"),
    )(x, g1, win)


def _mixer_fwd(proj, x, cw, pw, ps, wout, g2, g3, tm=256):
    T, D = x.shape
    P = proj.shape[1]

    def body(proj_ref, x_ref, cw_ref, pw_ref, ps_ref, wout_ref, g2_ref, g3_ref,
             x1_ref, hf_ref, mixed_ref, mo_ref, cu_carry, v_carry):
        i = pl.program_id(0)

        @pl.when(i == 0)
        def _():
            cu_carry[...] = jnp.zeros_like(cu_carry)
            v_carry[...] = jnp.zeros_like(v_carry)

        for h in range(CONV_HEADS):
            lo = h * HEAD_DIM
            gate_b = proj_ref[:, lo:lo + HEAD_DIM]
            cu = proj_ref[:, CONV_WIDTH + lo:CONV_WIDTH + lo + HEAD_DIM] * \
                proj_ref[:, 2 * CONV_WIDTH + lo:2 * CONV_WIDTH + lo + HEAD_DIM]
            ext = jnp.concatenate([cu_carry[:, lo:lo + HEAD_DIM], cu], axis=0)
            c1 = pltpu.roll(ext, 1, 0)[HALO:]
            c2 = pltpu.roll(ext, 2, 0)[HALO:]
            ya = gate_b * (cw_ref[h, 2:3, :] * cu + cw_ref[h, 1:2, :] * c1 + cw_ref[h, 0:1, :] * c2)
            mixed_ref[:, lo:lo + HEAD_DIM] = (ya * _rsq(ya)).astype(BF16)
            cu_carry[:, lo:lo + HEAD_DIM] = cu[tm - HALO:]

        for gi, w in enumerate(POOL_WINDOWS):
            lo = gi * POOL_GROUP_DIM
            v = proj_ref[:, 3 * CONV_WIDTH + lo:3 * CONV_WIDTH + lo + POOL_GROUP_DIM]
            ext = jnp.concatenate([v_carry[:, lo:lo + POOL_GROUP_DIM], v], axis=0)
            pooled = _window_sum(ext, w, True)[HALO:] * _inv_count(i * tm, tm, w) - v
            y = jnp.dot(pooled.astype(BF16), pw_ref[gi], preferred_element_type=F32)
            yb = y * _rsq(y) * ps_ref[:, lo:lo + POOL_GROUP_DIM]
            mixed_ref[:, CONV_WIDTH + lo:CONV_WIDTH + lo + POOL_GROUP_DIM] = yb.astype(BF16)
            v_carry[:, lo:lo + POOL_GROUP_DIM] = v[tm - HALO:]

        mo = jnp.dot(mixed_ref[...], wout_ref[...], preferred_element_type=F32)
        mo_ref[...] = mo
        x1 = x_ref[...] + mo * _rsq(mo) * g2_ref[...]
        x1_ref[...] = x1
        hf_ref[...] = (x1 * _rsq(x1) * g3_ref[...]).astype(BF16)

    row = lambda n: pl.BlockSpec((tm, n), lambda i: (i, 0))
    return pl.pallas_call(
        body, name="mixer_fwd", grid=(T // tm,),
        in_specs=[row(P), row(D), _whole(cw.shape), _whole(pw.shape), _whole(ps.shape),
                  _whole(wout.shape), _whole(g2.shape), _whole(g3.shape)],
        out_specs=[row(D), row(D), row(D), row(D)],
        out_shape=[jax.ShapeDtypeStruct((T, D), F32), jax.ShapeDtypeStruct((T, D), BF16),
                   jax.ShapeDtypeStruct((T, D), BF16), jax.ShapeDtypeStruct((T, D), F32)],
        scratch_shapes=[pltpu.VMEM((HALO, CONV_WIDTH), F32), pltpu.VMEM((HALO, CONV_WIDTH), F32)],
        compiler_params=_params("arbitrary"),
    )(proj, x, cw, pw, ps, wout, g2, g3)


def _ffn_up(hf, wg, wu, tm=1024):
    T, D = hf.shape
    nb, _, bf = wg.shape

    def body(hf_ref, wg_ref, wu_ref, g_ref, u_ref, a_ref):
        hv = hf_ref[...]
        g = jnp.dot(hv, wg_ref[0], preferred_element_type=F32)
        u = jnp.dot(hv, wu_ref[0], preferred_element_type=F32)
        g_ref[0] = g.astype(BF16)
        u_ref[0] = u.astype(BF16)
        a_ref[0] = (g * jax.nn.sigmoid(g) * u).astype(BF16)

    wspec = pl.BlockSpec((1, D, bf), lambda i, j: (j, 0, 0))
    ospec = pl.BlockSpec((1, tm, bf), lambda i, j: (j, i, 0))
    oshape = jax.ShapeDtypeStruct((nb, T, bf), BF16)
    return pl.pallas_call(
        body, name="ffn_up", grid=(T // tm, nb),
        in_specs=[pl.BlockSpec((tm, D), lambda i, j: (i, 0)), wspec, wspec],
        out_specs=[ospec, ospec, ospec], out_shape=[oshape, oshape, oshape],
        compiler_params=_params("arbitrary", "arbitrary"),
    )(hf, wg, wu)


def _ffn_down_loss(a, wd, x1, tgt, g4, tm=512):
    nb, T, bf = a.shape
    D = x1.shape[1]
    nt = T // tm

    def body(a_ref, wd_ref, x1_ref, tgt_ref, g4_ref, dy_ref, dff_ref, loss_ref, dg4_ref, acc_ref, lacc_ref):
        i, j = pl.program_id(0), pl.program_id(1)

        @pl.when((i == 0) & (j == 0))
        def _():
            lacc_ref[...] = jnp.zeros_like(lacc_ref)
            dg4_ref[...] = jnp.zeros_like(dg4_ref)

        @pl.when(j == 0)
        def _():
            acc_ref[...] = jnp.zeros_like(acc_ref)

        acc_ref[...] += jnp.dot(a_ref[0], wd_ref[0], preferred_element_type=F32)

        @pl.when(j == nb - 1)
        def _():
            ff = acc_ref[...]
            r = _rsq(ff)
            n = ff * r
            g4v = g4_ref[...]
            e = x1_ref[...] + n * g4v - tgt_ref[...]
            lacc_ref[...] += jnp.sum(e * e, axis=0, keepdims=True)
            dy = e * (1.0 / D)
            dy_ref[...] = dy
            dg4_ref[...] += jnp.sum(dy * n, axis=0, keepdims=True)
            dff_ref[...] = _norm_bwd(dy * g4v, n, r).astype(BF16)

        @pl.when((i == nt - 1) & (j == nb - 1))
        def _():
            loss_ref[...] = jnp.full(loss_ref.shape, (0.5 / D) * jnp.sum(lacc_ref[...]), F32)

    row = pl.BlockSpec((tm, D), lambda i, j: (i, 0))
    vec = pl.BlockSpec((1, D), lambda i, j: (0, 0))
    return pl.pallas_call(
        body, name="ffn_down_loss", grid=(nt, nb),
        in_specs=[pl.BlockSpec((1, tm, bf), lambda i, j: (j, i, 0)),
                  pl.BlockSpec((1, bf, D), lambda i, j: (j, 0, 0)), row, row, vec],
        out_specs=[row, row, pl.BlockSpec((1, 128), lambda i, j: (0, 0)), vec],
        out_shape=[jax.ShapeDtypeStruct((T, D), F32), jax.ShapeDtypeStruct((T, D), BF16),
                   jax.ShapeDtypeStruct((1, 128), F32), jax.ShapeDtypeStruct((1, D), F32)],
        scratch_shapes=[pltpu.VMEM((tm, D), F32), pltpu.VMEM((1, D), F32)],
        compiler_params=_params("arbitrary", "arbitrary"),
    )(a, wd, x1, tgt, g4)


def _ffn_bwd_act(dff, wd, g, u, tm=1024):
    T, D = dff.shape
    nb, bf, _ = wd.shape

    def body(dff_ref, wd_ref, g_ref, u_ref, dg_ref, du_ref):
        da = lax.dot_general(dff_ref[...], wd_ref[0], NT_DIMS, preferred_element_type=F32)
        gv = g_ref[0].astype(F32)
        s = jax.nn.sigmoid(gv)
        du_ref[0] = (da * (gv * s)).astype(BF16)
        dg_ref[0] = (da * u_ref[0].astype(F32) * (s * (1.0 + gv * (1.0 - s)))).astype(BF16)

    blk = pl.BlockSpec((1, tm, bf), lambda i, j: (j, i, 0))
    oshape = jax.ShapeDtypeStruct((nb, T, bf), BF16)
    return pl.pallas_call(
        body, name="ffn_bwd_act", grid=(T // tm, nb),
        in_specs=[pl.BlockSpec((tm, D), lambda i, j: (i, 0)),
                  pl.BlockSpec((1, bf, D), lambda i, j: (j, 0, 0)), blk, blk],
        out_specs=[blk, blk], out_shape=[oshape, oshape],
        compiler_params=_params("arbitrary", "arbitrary"),
    )(dff, wd, g, u)


def _ffn_bwd_in(dg, du, wg, wu, x1, dy, g3, tm=512):
    nb, T, bf = dg.shape
    D = x1.shape[1]

    def body(dg_ref, du_ref, wg_ref, wu_ref, x1_ref, dy_ref, g3_ref, dx1_ref, dg3_ref, acc_ref):
        i, j = pl.program_id(0), pl.program_id(1)

        @pl.when((i == 0) & (j == 0))
        def _():
            dg3_ref[...] = jnp.zeros_like(dg3_ref)

        @pl.when(j == 0)
        def _():
            acc_ref[...] = jnp.zeros_like(acc_ref)

        acc_ref[...] += (lax.dot_general(dg_ref[0], wg_ref[0], NT_DIMS, preferred_element_type=F32)
                         + lax.dot_general(du_ref[0], wu_ref[0], NT_DIMS, preferred_element_type=F32))

        @pl.when(j == nb - 1)
        def _():
            dhf = acc_ref[...]
            x1v = x1_ref[...]
            r = _rsq(x1v)
            n = x1v * r
            dg3_ref[...] += jnp.sum(dhf * n, axis=0, keepdims=True)
            dx1_ref[...] = dy_ref[...] + _norm_bwd(dhf * g3_ref[...], n, r)

    row = pl.BlockSpec((tm, D), lambda i, j: (i, 0))
    vec = pl.BlockSpec((1, D), lambda i, j: (0, 0))
    ablk = pl.BlockSpec((1, tm, bf), lambda i, j: (j, i, 0))
    wblk = pl.BlockSpec((1, D, bf), lambda i, j: (j, 0, 0))
    return pl.pallas_call(
        body, name="ffn_bwd_in", grid=(T // tm, nb),
        in_specs=[ablk, ablk, wblk, wblk, row, row, vec],
        out_specs=[row, vec],
        out_shape=[jax.ShapeDtypeStruct((T, D), F32), jax.ShapeDtypeStruct((1, D), F32)],
        scratch_shapes=[pltpu.VMEM((tm, D), F32)],
        compiler_params=_params("arbitrary", "arbitrary"),
    )(dg, du, wg, wu, x1, dy, g3)


def _wgrad_cols(lhs, rhs_blocks, name, tk=512):
    T, M = lhs.shape
    nb, _, N = rhs_blocks.shape

    def body(a_ref, b_ref, o_ref):
        @pl.when(pl.program_id(1) == 0)
        def _():
            o_ref[...] = jnp.zeros_like(o_ref)

        o_ref[0] += lax.dot_general(a_ref[...], b_ref[0], TN_DIMS, preferred_element_type=F32)

    return pl.pallas_call(
        body, name=name, grid=(nb, T // tk),
        in_specs=[pl.BlockSpec((tk, M), lambda j, k: (k, 0)),
                  pl.BlockSpec((1, tk, N), lambda j, k: (j, k, 0))],
        out_specs=pl.BlockSpec((1, M, N), lambda j, k: (j, 0, 0)),
        out_shape=jax.ShapeDtypeStruct((nb, M, N), F32),
        compiler_params=_params("arbitrary", "arbitrary"),
    )(lhs, rhs_blocks)


def _wgrad_pair(lhs, rhs_a, rhs_b, name, tk=512):
    T, M = lhs.shape
    nb, _, N = rhs_a.shape

    def body(l_ref, a_ref, b_ref, oa_ref, ob_ref):
        @pl.when(pl.program_id(1) == 0)
        def _():
            oa_ref[...] = jnp.zeros_like(oa_ref)
            ob_ref[...] = jnp.zeros_like(ob_ref)

        lv = l_ref[...]
        oa_ref[0] += lax.dot_general(lv, a_ref[0], TN_DIMS, preferred_element_type=F32)
        ob_ref[0] += lax.dot_general(lv, b_ref[0], TN_DIMS, preferred_element_type=F32)

    rblk = pl.BlockSpec((1, tk, N), lambda j, k: (j, k, 0))
    oblk = pl.BlockSpec((1, M, N), lambda j, k: (j, 0, 0))
    oshape = jax.ShapeDtypeStruct((nb, M, N), F32)
    return pl.pallas_call(
        body, name=name, grid=(nb, T // tk),
        in_specs=[pl.BlockSpec((tk, M), lambda j, k: (k, 0)), rblk, rblk],
        out_specs=[oblk, oblk], out_shape=[oshape, oshape],
        compiler_params=_params("arbitrary", "arbitrary"),
    )(lhs, rhs_a, rhs_b)


def _wgrad_rows(lhs_blocks, rhs, name, tk=1024):
    nb, T, M = lhs_blocks.shape
    N = rhs.shape[1]

    def body(a_ref, b_ref, o_ref):
        @pl.when(pl.program_id(1) == 0)
        def _():
            o_ref[...] = jnp.zeros_like(o_ref)

        o_ref[0] += lax.dot_general(a_ref[0], b_ref[...], TN_DIMS, preferred_element_type=F32)

    return pl.pallas_call(
        body, name=name, grid=(nb, T // tk),
        in_specs=[pl.BlockSpec((1, tk, M), lambda j, k: (j, k, 0)),
                  pl.BlockSpec((tk, N), lambda j, k: (k, 0))],
        out_specs=pl.BlockSpec((1, M, N), lambda j, k: (j, 0, 0)),
        out_shape=jax.ShapeDtypeStruct((nb, M, N), F32),
        compiler_params=_params("arbitrary", "arbitrary"),
    )(lhs_blocks, rhs)


def _wgrad_out(mixed, dmo, nb=NDEV, tk=1024):
    T, D = mixed.shape
    bm = D // nb

    def body(a_ref, b_ref, o_ref):
        @pl.when(pl.program_id(1) == 0)
        def _():
            o_ref[...] = jnp.zeros_like(o_ref)

        o_ref[0] += lax.dot_general(a_ref[...], b_ref[...], TN_DIMS, preferred_element_type=F32)

    return pl.pallas_call(
        body, name="wgrad_out", grid=(nb, T // tk),
        in_specs=[pl.BlockSpec((tk, bm), lambda j, k: (k, j)),
                  pl.BlockSpec((tk, D), lambda j, k: (k, 0))],
        out_specs=pl.BlockSpec((1, bm, D), lambda j, k: (j, 0, 0)),
        out_shape=jax.ShapeDtypeStruct((nb, bm, D), F32),
        compiler_params=_params("arbitrary", "arbitrary"),
    )(mixed, dmo)


def _wgrad_in(h, dproj, nb=NDEV, tk=1024):
    T, D = h.shape
    bn = dproj.shape[1] // nb

    def body(a_ref, b_ref, o_ref):
        @pl.when(pl.program_id(1) == 0)
        def _():
            o_ref[...] = jnp.zeros_like(o_ref)

        o_ref[0] += lax.dot_general(a_ref[...], b_ref[...], TN_DIMS, preferred_element_type=F32)

    return pl.pallas_call(
        body, name="wgrad_in", grid=(nb, T // tk),
        in_specs=[pl.BlockSpec((tk, D), lambda j, k: (k, 0)),
                  pl.BlockSpec((tk, bn), lambda j, k: (k, j))],
        out_specs=pl.BlockSpec((1, D, bn), lambda j, k: (j, 0, 0)),
        out_shape=jax.ShapeDtypeStruct((nb, D, bn), F32),
        compiler_params=_params("arbitrary", "arbitrary"),
    )(h, dproj)


def _mixer_bwd(dx1, mo, proj, cw, pw, ps, wout, g2, tm=256):
    T, D = dx1.shape
    P = proj.shape[1]
    nt = T // tm
    n_ext = tm + HALO
    hb = tm // HALO

    def body(dx1_ref, mo_ref, proj_ref, hc_ref, hu_ref, hv_ref, cw_ref, pw_ref, ps_ref, wout_ref, g2_ref,
             dmo_ref, dproj_ref, dg2_ref, dcw_ref, dps_ref, dpw_ref, dmix_ref, dconv_carry, q_carry):
        i = pl.program_id(0)
        tile = nt - 1 - i

        @pl.when(i == 0)
        def _():
            dconv_carry[...] = jnp.zeros_like(dconv_carry)
            q_carry[...] = jnp.zeros_like(q_carry)
            dg2_ref[...] = jnp.zeros_like(dg2_ref)
            dcw_ref[...] = jnp.zeros_like(dcw_ref)
            dps_ref[...] = jnp.zeros_like(dps_ref)
            dpw_ref[...] = jnp.zeros_like(dpw_ref)

        mov = mo_ref[...]
        r2 = _rsq(mov)
        n2 = mov * r2
        dx1v = dx1_ref[...]
        dg2_ref[...] += jnp.sum(dx1v * n2, axis=0, keepdims=True)
        dmo = _norm_bwd(dx1v * g2_ref[...], n2, r2).astype(BF16)
        dmo_ref[...] = dmo
        dmix_ref[...] = lax.dot_general(dmo, wout_ref[...], NT_DIMS, preferred_element_type=F32)

        has_prev = (tile > 0).astype(F32)

        for h in range(CONV_HEADS):
            lo = h * HEAD_DIM
            sl = slice(lo, lo + HEAD_DIM)
            gate_b = proj_ref[:, lo:lo + HEAD_DIM]
            gate_c = proj_ref[:, CONV_WIDTH + lo:CONV_WIDTH + lo + HEAD_DIM]
            uu = proj_ref[:, 2 * CONV_WIDTH + lo:2 * CONV_WIDTH + lo + HEAD_DIM]
            cu = gate_c * uu
            ext = jnp.concatenate([hc_ref[:, sl] * hu_ref[:, sl] * has_prev, cu], axis=0)
            c1 = pltpu.roll(ext, 1, 0)[HALO:]
            c2 = pltpu.roll(ext, 2, 0)[HALO:]
            w0, w1, w2 = cw_ref[h, 0:1, :], cw_ref[h, 1:2, :], cw_ref[h, 2:3, :]
            conv = w2 * cu + w1 * c1 + w0 * c2
            ya = gate_b * conv
            ra = _rsq(ya)
            dya = _norm_bwd(dmix_ref[:, sl], ya * ra, ra)
            dconv = dya * gate_b
            dcw_ref[h, 0:1, :] += jnp.sum(dconv * c2, axis=0, keepdims=True)
            dcw_ref[h, 1:2, :] += jnp.sum(dconv * c1, axis=0, keepdims=True)
            dcw_ref[h, 2:3, :] += jnp.sum(dconv * cu, axis=0, keepdims=True)
            extd = jnp.concatenate([dconv, dconv_carry[:, sl]], axis=0)
            d1 = pltpu.roll(extd, n_ext - 1, 0)[:tm]
            d2 = pltpu.roll(extd, n_ext - 2, 0)[:tm]
            dcu = w2 * dconv + w1 * d1 + w0 * d2
            dconv_carry[:, sl] = dconv[:HALO]
            dproj_ref[:, lo:lo + HEAD_DIM] = (dya * conv).astype(BF16)
            dproj_ref[:, CONV_WIDTH + lo:CONV_WIDTH + lo + HEAD_DIM] = (dcu * uu).astype(BF16)
            dproj_ref[:, 2 * CONV_WIDTH + lo:2 * CONV_WIDTH + lo + HEAD_DIM] = (dcu * gate_c).astype(BF16)

        for gi, w in enumerate(POOL_WINDOWS):
            lo = gi * POOL_GROUP_DIM
            sl = slice(lo, lo + POOL_GROUP_DIM)
            v = proj_ref[:, 3 * CONV_WIDTH + lo:3 * CONV_WIDTH + lo + POOL_GROUP_DIM]
            inv = _inv_count(tile * tm, tm, w)
            ext = jnp.concatenate([hv_ref[:, sl] * has_prev, v], axis=0)
            pooled = (_window_sum(ext, w, True)[HALO:] * inv - v).astype(BF16)
            y = jnp.dot(pooled, pw_ref[gi], preferred_element_type=F32)
            rp = _rsq(y)
            nb_ = y * rp
            dyb = dmix_ref[:, CONV_WIDTH + lo:CONV_WIDTH + lo + POOL_GROUP_DIM]
            dps_ref[:, sl] += jnp.sum(dyb * nb_, axis=0, keepdims=True)
            dy = _norm_bwd(dyb * ps_ref[:, sl], nb_, rp).astype(BF16)
            dpw_ref[gi] += lax.dot_general(pooled, dy, TN_DIMS, preferred_element_type=F32)
            dpooled = lax.dot_general(dy, pw_ref[gi], NT_DIMS, preferred_element_type=F32)
            q = dpooled * inv
            extq = jnp.concatenate([q, q_carry[:, sl]], axis=0)
            dv = _window_sum(extq, w, False)[:tm] - dpooled
            q_carry[:, sl] = q[:HALO]
            dproj_ref[:, 3 * CONV_WIDTH + lo:3 * CONV_WIDTH + lo + POOL_GROUP_DIM] = dv.astype(BF16)

    rev = lambda n: pl.BlockSpec((tm, n), lambda i: (nt - 1 - i, 0))

    def halo(col):
        return pl.BlockSpec((HALO, CONV_WIDTH), lambda i: (jnp.maximum((nt - 1 - i) * hb - 1, 0), col))

    return pl.pallas_call(
        body, name="mixer_bwd", grid=(nt,),
        in_specs=[rev(D), rev(D), rev(P), halo(1), halo(2), halo(3), _whole(cw.shape), _whole(pw.shape),
                  _whole(ps.shape), _whole(wout.shape), _whole(g2.shape)],
        out_specs=[rev(D), rev(P), pl.BlockSpec((1, D), lambda i: (0, 0)),
                   pl.BlockSpec(cw.shape, lambda i: (0, 0, 0)), pl.BlockSpec(ps.shape, lambda i: (0, 0)),
                   pl.BlockSpec(pw.shape, lambda i: (0, 0, 0))],
        out_shape=[jax.ShapeDtypeStruct((T, D), BF16), jax.ShapeDtypeStruct((T, P), BF16),
                   jax.ShapeDtypeStruct((1, D), F32), jax.ShapeDtypeStruct(cw.shape, F32),
                   jax.ShapeDtypeStruct(ps.shape, F32), jax.ShapeDtypeStruct(pw.shape, F32)],
        scratch_shapes=[pltpu.VMEM((tm, D), F32), pltpu.VMEM((HALO, CONV_WIDTH), F32),
                        pltpu.VMEM((HALO, CONV_WIDTH), F32)],
        compiler_params=_params("arbitrary"),
    )(dx1, mo, proj, proj, proj, proj, cw, pw, ps, wout, g2)


def _inproj_bwd(dproj, win, x, dx1, g1, tm=512):
    T, D = x.shape
    nb, _, bn = win.shape

    def body(dp_ref, w_ref, x_ref, dx1_ref, g1_ref, gx_ref, dg1_ref, acc_ref):
        i, j = pl.program_id(0), pl.program_id(1)

        @pl.when((i == 0) & (j == 0))
        def _():
            dg1_ref[...] = jnp.zeros_like(dg1_ref)

        @pl.when(j == 0)
        def _():
            acc_ref[...] = jnp.zeros_like(acc_ref)

        acc_ref[...] += lax.dot_general(dp_ref[...], w_ref[0], NT_DIMS, preferred_element_type=F32)

        @pl.when(j == nb - 1)
        def _():
            dh = acc_ref[...]
            xv = x_ref[...]
            r = _rsq(xv)
            n = xv * r
            dg1_ref[...] += jnp.sum(dh * n, axis=0, keepdims=True)
            gx_ref[...] = dx1_ref[...] + _norm_bwd(dh * g1_ref[...], n, r)

    row = pl.BlockSpec((tm, D), lambda i, j: (i, 0))
    vec = pl.BlockSpec((1, D), lambda i, j: (0, 0))
    return pl.pallas_call(
        body, name="inproj_bwd", grid=(T // tm, nb),
        in_specs=[pl.BlockSpec((tm, bn), lambda i, j: (i, j)),
                  pl.BlockSpec((1, D, bn), lambda i, j: (j, 0, 0)), row, row, vec],
        out_specs=[row, vec],
        out_shape=[jax.ShapeDtypeStruct((T, D), F32), jax.ShapeDtypeStruct((1, D), F32)],
        scratch_shapes=[pltpu.VMEM((tm, D), F32)],
        compiler_params=_params("arbitrary", "arbitrary"),
    )(dproj, win, x, dx1, g1)


HBM_SPEC = pl.BlockSpec(memory_space=pl.ANY)
VMEM_SPEC = pl.BlockSpec(memory_space=pltpu.VMEM)


def _coords():
    return lax.axis_index("x"), lax.axis_index("y"), lax.axis_index("c")


def _other_chips(x, y):
    return [(1 - x, y), (x, 1 - y), (1 - x, 1 - y)]


POOL_ITEM = 5


def _allgather_weights(shards):
    n = len(shards)
    dtypes = [BF16] * 6 + [F32]
    out_shapes = [(NDEV,) + s.shape for s in shards]
    g, rows, cols = shards[POOL_ITEM].shape
    out_shapes[POOL_ITEM] = (g, rows * NDEV, cols)

    def body(*refs):
        ins, outs, stage = refs[:n], refs[n:2 * n], refs[2 * n:3 * n]
        send_sems, recv_sems, local_sems = refs[3 * n:]
        x, y, c = _coords()
        me, sibling = (x, y, c), (x, y, 1 - c)
        chips = _other_chips(x, y)

        def view(a, dev):
            i = 4 * dev[0] + 2 * dev[1] + dev[2]
            if a == POOL_ITEM:
                return outs[a].at[:, pl.ds(i * rows, rows), :]
            return outs[a].at[i]

        def copy(a, k, block, to, src=None):
            return pltpu.make_async_remote_copy(
                src_ref=view(a, block) if src is None else src, dst_ref=view(a, block),
                send_sem=send_sems.at[a, k], recv_sem=recv_sems.at[a, k], device_id=to, device_id_type=MESH)

        for a in range(n):
            stage[a][...] = ins[a][...].astype(dtypes[a])
        mine = [pltpu.make_async_copy(stage[a], view(a, me), local_sems.at[a]) for a in range(n)]
        for cp in mine:
            cp.start()
        first = []
        for a in range(n):
            first.append(copy(a, 0, me, sibling, src=stage[a]))
            first += [copy(a, 1 + j, me, (*chip, c), src=stage[a]) for j, chip in enumerate(chips)]
        for cp in first:
            cp.start()
        passed = []
        for j, chip in enumerate(chips):
            for a in range(n):
                copy(a, 1 + j, (*chip, c), me).wait_recv()
                passed.append(copy(a, 4 + j, (*chip, c), sibling))
                passed[-1].start()
        for a in range(n):
            copy(a, 0, sibling, me).wait_recv()
            for j, chip in enumerate(chips):
                copy(a, 4 + j, (*chip, 1 - c), me).wait_recv()
        for cp in first + passed:
            cp.wait_send()
        for cp in mine:
            cp.wait()

    return pl.pallas_call(
        body, name="allgather_weights",
        in_specs=[VMEM_SPEC] * n, out_specs=[HBM_SPEC] * n,
        out_shape=[jax.ShapeDtypeStruct(s, d) for s, d in zip(out_shapes, dtypes)],
        scratch_shapes=[pltpu.VMEM(s.shape, d) for s, d in zip(shards, dtypes)]
        + [pltpu.SemaphoreType.DMA((n, 7)), pltpu.SemaphoreType.DMA((n, 7)), pltpu.SemaphoreType.DMA((n,))],
        compiler_params=pltpu.CompilerParams(vmem_limit_bytes=VMEM_LIMIT),
    )(*shards)


def _exchange_sibling(grads):
    n = len(grads)

    def body(*refs):
        ins, outs = refs[:n], refs[n:2 * n]
        send_sems, recv_sems = refs[2 * n:]
        x, y, c = _coords()
        copies = []
        for a in range(n):
            for q in range(NCHIP):
                copies.append(pltpu.make_async_remote_copy(
                    src_ref=ins[a].at[2 * q + 1 - c], dst_ref=outs[a].at[q],
                    send_sem=send_sems.at[a, q], recv_sem=recv_sems.at[a, q],
                    device_id=(x, y, 1 - c), device_id_type=MESH))
                copies[-1].start()
        for cp in copies:
            cp.wait()

    return pl.pallas_call(
        body, name="exchange_sibling",
        in_specs=[HBM_SPEC] * n, out_specs=[HBM_SPEC] * n,
        out_shape=[jax.ShapeDtypeStruct((NCHIP,) + gr.shape[1:], F32) for gr in grads],
        scratch_shapes=[pltpu.SemaphoreType.DMA((n, NCHIP)), pltpu.SemaphoreType.DMA((n, NCHIP))],
    )(*grads)


def _add_sibling(grad, recv, core, name, tr):
    _, r, cd = grad.shape
    grad4 = grad.reshape(NCHIP, 2, r, cd)

    def body(core_ref, g_ref, r_ref, o_ref):
        o_ref[0] = (g_ref[0, 0] + r_ref[0]).astype(BF16)

    return pl.pallas_call(
        body, name=name,
        grid_spec=pltpu.PrefetchScalarGridSpec(
            num_scalar_prefetch=1, grid=(NCHIP, r // tr),
            in_specs=[pl.BlockSpec((1, 1, tr, cd), lambda q, i, core_ref: (q, core_ref[0], i, 0)),
                      pl.BlockSpec((1, tr, cd), lambda q, i, core_ref: (q, i, 0))],
            out_specs=pl.BlockSpec((1, tr, cd), lambda q, i, core_ref: (q, i, 0))),
        out_shape=jax.ShapeDtypeStruct((NCHIP, r, cd), BF16),
        compiler_params=_params("arbitrary", "arbitrary"),
    )(core, grad4, recv)


def _exchange_chips(parts):
    n = len(parts)

    def body(*refs):
        ins, outs = refs[:n], refs[n:2 * n]
        send_sems, recv_sems, local_sems = refs[2 * n:]
        x, y, c = _coords()
        my_q = 2 * x + y
        chips = _other_chips(x, y)
        mine = [pltpu.make_async_copy(ins[a].at[my_q], outs[a].at[my_q], local_sems.at[a]) for a in range(n)]
        for cp in mine:
            cp.start()
        copies = []
        for a in range(n):
            for j, chip in enumerate(chips):
                q = 2 * chip[0] + chip[1]
                copies.append((
                    pltpu.make_async_remote_copy(
                        src_ref=ins[a].at[q], dst_ref=outs[a].at[my_q], send_sem=send_sems.at[a, j],
                        recv_sem=recv_sems.at[a, j], device_id=(*chip, c), device_id_type=MESH),
                    pltpu.make_async_remote_copy(
                        src_ref=ins[a].at[q], dst_ref=outs[a].at[q], send_sem=send_sems.at[a, j],
                        recv_sem=recv_sems.at[a, j], device_id=(*chip, c), device_id_type=MESH)))
                copies[-1][0].start()
        for send, recv in copies:
            send.wait_send()
            recv.wait_recv()
        for cp in mine:
            cp.wait()

    return pl.pallas_call(
        body, name="exchange_chips",
        in_specs=[HBM_SPEC] * n, out_specs=[HBM_SPEC] * n,
        out_shape=[jax.ShapeDtypeStruct(p.shape, p.dtype) for p in parts],
        scratch_shapes=[pltpu.SemaphoreType.DMA((n, 3)), pltpu.SemaphoreType.DMA((n, 3)),
                        pltpu.SemaphoreType.DMA((n,))],
    )(*parts)


def _adamw(w, g, m, v):
    m = ADAM_B1 * m + (1.0 - ADAM_B1) * g
    v = ADAM_B2 * v + (1.0 - ADAM_B2) * jnp.square(g)
    m_hat = m / (1.0 - ADAM_B1 ** ADAM_STEP)
    v_hat = v / (1.0 - ADAM_B2 ** ADAM_STEP)
    delta = -ADAM_LR * (m_hat / (jnp.sqrt(v_hat) + ADAM_EPS) + ADAM_WD * w)
    return delta, m, v


def _sum_adamw(parts, w, m, v, name, tr):
    r, cd = w.shape

    def body(p_ref, w_ref, m_ref, v_ref, g_ref, d_ref, mo_ref, vo_ref):
        g = ((p_ref[0].astype(F32) + p_ref[1].astype(F32)) + p_ref[2].astype(F32)) + p_ref[3].astype(F32)
        g_ref[...] = g
        d_ref[...], mo_ref[...], vo_ref[...] = _adamw(w_ref[...], g, m_ref[...], v_ref[...])

    blk = pl.BlockSpec((tr, cd), lambda i: (i, 0))
    shp = jax.ShapeDtypeStruct((r, cd), F32)
    return pl.pallas_call(
        body, name=name, grid=(r // tr,),
        in_specs=[pl.BlockSpec((NCHIP, tr, cd), lambda i: (0, i, 0)), blk, blk, blk],
        out_specs=[blk] * 4, out_shape=[shp] * 4,
        compiler_params=_params("arbitrary"),
    )(parts, w, m, v)


def _small_reduce_adamw(vec_grads, dps, dcw, dpw, vec_state, ps_state, cw_state, pw_state):
    D = vec_grads[0].shape[1]
    pw_rows = pw_state[0].shape[1]
    states = list(vec_state) + [ps_state, cw_state, pw_state]
    n_in = 4 + 3 + 3 * len(states)
    n_out = 4 * len(states)

    def body(*refs):
        dg = refs[0:4]
        dps_ref, dcw_ref, dpw_ref = refs[4:7]
        st = refs[7:n_in]
        outs = refs[n_in:n_in + n_out]
        pack, gat, cbuf, pbuf, send_sems, recv_sems, local_sems = refs[n_in + n_out:]
        x, y, c = _coords()
        me = 4 * x + 2 * y + c

        pack[...] = jnp.zeros_like(pack)
        for k in range(4):
            pack[k:k + 1, :] = dg[k][...]
        pack[4:5, 0:dps_ref.shape[1]] = dps_ref[...]

        def pw_slice(i):
            return dpw_ref.at[:, pl.ds(i * pw_rows, pw_rows), :]

        mine = [pltpu.make_async_copy(pack, gat.at[me], local_sems.at[0]),
                pltpu.make_async_copy(dcw_ref.at[me], cbuf.at[me], local_sems.at[1]),
                pltpu.make_async_copy(pw_slice(me), pbuf.at[me], local_sems.at[2])]
        for cp in mine:
            cp.start()
        sends, recvs = [], []
        for mask in range(1, NDEV):
            px = 1 - x if mask & 4 else x
            py = 1 - y if mask & 2 else y
            pc = 1 - c if mask & 1 else c
            peer = 4 * px + 2 * py + pc
            for k, (src, buf) in enumerate(((pack, gat), (dcw_ref.at[peer], cbuf), (pw_slice(peer), pbuf))):
                sends.append(pltpu.make_async_remote_copy(
                    src_ref=src, dst_ref=buf.at[me], send_sem=send_sems.at[mask, k], recv_sem=recv_sems.at[mask, k],
                    device_id=(px, py, pc), device_id_type=MESH))
                recvs.append(pltpu.make_async_remote_copy(
                    src_ref=src, dst_ref=buf.at[peer], send_sem=send_sems.at[mask, k], recv_sem=recv_sems.at[mask, k],
                    device_id=(px, py, pc), device_id_type=MESH))
                sends[-1].start()
        for cp in recvs:
            cp.wait_recv()
        for cp in sends:
            cp.wait_send()
        for cp in mine:
            cp.wait()

        def slot_sum(buf):
            s = buf[0]
            for k in range(1, NDEV):
                s = s + buf[k]
            return s

        vec = slot_sum(gat)
        grads = [vec[k:k + 1, :] for k in range(4)] + [vec[4:5, 0:dps_ref.shape[1]], slot_sum(cbuf), slot_sum(pbuf)]
        for k, g in enumerate(grads):
            w_ref, m_ref, v_ref = st[3 * k:3 * k + 3]
            outs[4 * k][...] = g
            outs[4 * k + 1][...], outs[4 * k + 2][...], outs[4 * k + 3][...] = _adamw(
                w_ref[...], g, m_ref[...], v_ref[...])

    flat_state = [a for s in states for a in s]
    out_shape = [jax.ShapeDtypeStruct(s[0].shape, F32) for s in states for _ in range(4)]
    return pl.pallas_call(
        body, name="small_reduce_adamw",
        in_specs=[VMEM_SPEC] * n_in, out_specs=[VMEM_SPEC] * n_out, out_shape=out_shape,
        scratch_shapes=[pltpu.VMEM((NDEV, D), F32), pltpu.VMEM((NDEV, NDEV, D), F32),
                        pltpu.VMEM((NDEV,) + cw_state[0].shape, F32), pltpu.VMEM((NDEV,) + pw_state[0].shape, F32),
                        pltpu.SemaphoreType.DMA((NDEV, 3)), pltpu.SemaphoreType.DMA((NDEV, 3)),
                        pltpu.SemaphoreType.DMA((3,))],
        compiler_params=pltpu.CompilerParams(vmem_limit_bytes=VMEM_LIMIT),
    )(*vec_grads, dps, dcw, dpw, *flat_state)


def _local_step(x, tgt, g1, g2, g3, g4, ps, win, cw, pw, wout, wg, wu, wd):
    proj, h = _inproj(x, g1, win)
    x1, hf, mixed, mo = _mixer_fwd(proj, x, cw, pw, ps, wout, g2, g3)
    g, u, a = _ffn_up(hf, wg, wu)
    dy, dff, loss, dg4 = _ffn_down_loss(a, wd, x1, tgt, g4)
    dwd = _wgrad_rows(a, dff, "wgrad_down")
    dg, du = _ffn_bwd_act(dff, wd, g, u)
    dwg, dwu = _wgrad_pair(hf, dg, du, "wgrad_gate_up")
    dx1, dg3 = _ffn_bwd_in(dg, du, wg, wu, x1, dy, g3)
    dmo, dproj, dg2, dcw, dps, dpw = _mixer_bwd(dx1, mo, proj, cw, pw, ps, wout, g2)
    dwout = _wgrad_out(mixed, dmo)
    dwin = _wgrad_in(h, dproj)
    gx, dg1 = _inproj_bwd(dproj, win, x, dx1, g1)
    return dict(loss=loss[0, 0], gx=gx, dg1=dg1, dg2=dg2, dg3=dg3, dg4=dg4, dps=dps, dcw=dcw, dpw=dpw,
                dwin=dwin, dwout=dwout, dwg=dwg, dwu=dwu, dwd=dwd)


ROW_TILE = dict(w_in=512, w_gate=256, w_up=256, w_down=176, w_out=128)


def kernel(x, ln_mix_pre, w_in, conv_w, pool_w, pool_scale, w_out, ln_mix_post, ln_ffn_pre, w_gate, w_up, w_down, ln_ffn_post, loss_target, m_ln_mix_pre, m_w_in, m_conv_w, m_pool_w, m_pool_scale, m_w_out, m_ln_mix_post, m_ln_ffn_pre, m_w_gate, m_w_up, m_w_down, m_ln_ffn_post, v_ln_mix_pre, v_w_in, v_conv_w, v_pool_w, v_pool_scale, v_w_out, v_ln_mix_post, v_ln_ffn_pre, v_w_gate, v_w_up, v_w_down, v_ln_ffn_post):
    D = x.shape[2]
    win, wg, wu, wd, wout, pw, cw = _allgather_weights(
        [w_in[0], w_gate[0], w_up[0], w_down[0], w_out[0], pool_w[0], conv_w[0]])
    r = _local_step(x[0], loss_target[0], ln_mix_pre, ln_mix_post, ln_ffn_pre, ln_ffn_post, pool_scale,
                    win, cw, pw, wout.reshape(D, D), wg, wu, wd)
    loss = lax.psum(r["loss"], ("x", "y", "c"))

    big = dict(w_in=(r["dwin"], w_in, m_w_in, v_w_in), w_gate=(r["dwg"], w_gate, m_w_gate, v_w_gate),
               w_up=(r["dwu"], w_up, m_w_up, v_w_up), w_down=(r["dwd"], w_down, m_w_down, v_w_down),
               w_out=(r["dwout"], w_out, m_w_out, v_w_out))
    names = list(big)
    from_sibling = _exchange_sibling([big[k][0] for k in names])
    core = lax.axis_index("c").astype(jnp.int32).reshape(1)
    parts = [_add_sibling(big[k][0], rc, core, "add_sibling_" + k, ROW_TILE[k]) for k, rc in zip(names, from_sibling)]
    sums = _exchange_chips(parts)
    res = {}
    for k, s in zip(names, sums):
        _, w, m, v = big[k]
        outs = _sum_adamw(s, w[0], m[0], v[0], "sum_adamw_" + k, ROW_TILE[k])
        res[k] = [o.reshape(w.shape) for o in outs]

    small = _small_reduce_adamw(
        [r["dg1"], r["dg2"], r["dg3"], r["dg4"]], r["dps"], r["dcw"], r["dpw"],
        [(ln_mix_pre, m_ln_mix_pre, v_ln_mix_pre), (ln_mix_post, m_ln_mix_post, v_ln_mix_post),
         (ln_ffn_pre, m_ln_ffn_pre, v_ln_ffn_pre), (ln_ffn_post, m_ln_ffn_post, v_ln_ffn_post)],
        (pool_scale, m_pool_scale, v_pool_scale), (conv_w[0], m_conv_w[0], v_conv_w[0]),
        (pool_w[0], m_pool_w[0], v_pool_w[0]))
    small_names = ["ln_mix_pre", "ln_mix_post", "ln_ffn_pre", "ln_ffn_post", "pool_scale", "conv_w", "pool_w"]
    shapes = dict(conv_w=conv_w.shape, pool_w=pool_w.shape)
    for i, k in enumerate(small_names):
        res[k] = [o.reshape(shapes[k]) if k in shapes else o for o in small[4 * i:4 * i + 4]]

    order = ["ln_mix_pre", "w_in", "conv_w", "pool_w", "pool_scale", "w_out", "ln_mix_post", "ln_ffn_pre",
             "w_gate", "w_up", "w_down", "ln_ffn_post"]
    return (loss, r["gx"][None], *[res[k][0] for k in order], *[res[k][1] for k in order],
            *[res[k][2] for k in order], *[res[k][3] for k in order])
```

```python
import functools
from typing import Any, NamedTuple

import jax
import jax.numpy as jnp
from jax import lax
from jax.experimental import pallas as pl
from jax.experimental.pallas import tpu as pltpu

EPS = 1e-6
NDEV = 8
CONV_HEADS = 8
HEAD_DIM = 128
CONV_WIDTH = CONV_HEADS * HEAD_DIM
POOL_WINDOWS = (2, 4, 8, 16)
POOL_GROUP_DIM = 256
HALO = 16

ADAM_LR = 0.001
ADAM_B1 = 0.9
ADAM_B2 = 0.999
ADAM_EPS = 1e-08
ADAM_WD = 0.01
ADAM_STEP = 10

F32 = jnp.float32
BF16 = jnp.bfloat16
VMEM_LIMIT = 58 * 1024 * 1024
MESH = pl.DeviceIdType.MESH
HBM_SPEC = pl.BlockSpec(memory_space=pl.ANY)
VMEM_SPEC = pl.BlockSpec(memory_space=pltpu.VMEM)

NT_DIMS = (((1,), (1,)), ((), ()))
TN_DIMS = (((0,), (0,)), ((), ()))


def _params(*sem):
    return pltpu.CompilerParams(dimension_semantics=sem, vmem_limit_bytes=VMEM_LIMIT)


def _rsq(v):
    return lax.rsqrt(jnp.mean(v * v, axis=-1, keepdims=True) + EPS)


def _norm_bwd(dn, n, r):
    return r * (dn - n * jnp.mean(dn * n, axis=-1, keepdims=True))


def _whole(shape):
    nd = len(shape)
    return pl.BlockSpec(shape, lambda *_: (0,) * nd, pipeline_mode=pl.Buffered(1))


def _inv_count(t0, tm, w):
    t = t0 + lax.broadcasted_iota(jnp.int32, (tm, 1), 0)
    return 1.0 / jnp.minimum(t + 1, w).astype(F32)


def _window_sum(ext, w, back):
    n = ext.shape[0]
    s, shift = ext, 1
    while shift < w:
        s = s + pltpu.roll(s, shift if back else n - shift, 0)
        shift *= 2
    return s


class _Comm(NamedTuple):
    arrays: Any
    out_shape: Any
    aliases: Any
    scratch: Any
    hooks: Any


def _coords():
    return lax.axis_index("x"), lax.axis_index("y"), lax.axis_index("c")


def _other_chips(x, y):
    return [(1 - x, y), (x, 1 - y), (1 - x, 1 - y)]


def _device_index(dev):
    return 4 * dev[0] + 2 * dev[1] + dev[2]


def _host_call(body, *, name, grid, in_specs, out_specs, out_shape, args, scratch_shapes=(), comm=None):
    sem = ("arbitrary",) * len(grid)
    in_specs, out_specs, out_shape, scratch_shapes = list(in_specs), list(out_specs), list(out_shape), list(scratch_shapes)
    if comm is None:
        res = pl.pallas_call(body, name=name, grid=grid, in_specs=in_specs, out_specs=out_specs, out_shape=out_shape,
                             scratch_shapes=scratch_shapes, compiler_params=_params(*sem))(*args)
        return res, []
    n_in, n_out, n_scr = len(in_specs), len(out_specs), len(scratch_shapes)
    n_cin, n_cout = len(comm.arrays), len(comm.out_shape)
    total = functools.reduce(lambda a, b: a * b, grid)

    def wrapped(*refs):
        ins, cin = refs[:n_in], refs[n_in:n_in + n_cin]
        o0 = n_in + n_cin
        outs, cout = refs[o0:o0 + n_out], refs[o0 + n_out:o0 + n_out + n_cout]
        s0 = o0 + n_out + n_cout
        scr, sems = refs[s0:s0 + n_scr], refs[s0 + n_scr:]
        step = pl.program_id(0)
        for d in range(1, len(grid)):
            step = step * grid[d] + pl.program_id(d)
        for when, before, fn in comm.hooks:
            if before:
                pl.when(step == when % total)(functools.partial(fn, cin, cout, sems))
        body(*ins, *outs, *scr)
        for when, before, fn in comm.hooks:
            if not before:
                pl.when(step == when % total)(functools.partial(fn, cin, cout, sems))

    res = pl.pallas_call(
        wrapped, name=name, grid=grid,
        in_specs=in_specs + [HBM_SPEC] * n_cin, out_specs=out_specs + [HBM_SPEC] * n_cout,
        out_shape=out_shape + list(comm.out_shape), scratch_shapes=scratch_shapes + list(comm.scratch),
        input_output_aliases={n_in + i: n_out + o for i, o in comm.aliases.items()},
        compiler_params=_params(*sem),
    )(*args, *comm.arrays)
    return res[:n_out], res[n_out:]


def _gather_steps(n, view, own_src, send_sems, recv_sems):
    x, y, c = _coords()
    me, sibling = (x, y, c), (x, y, 1 - c)
    chips = _other_chips(x, y)

    def copy(a, k, block, to, src=None):
        return pltpu.make_async_remote_copy(
            src_ref=view(a, block) if src is None else src, dst_ref=view(a, block),
            send_sem=send_sems.at[a, k], recv_sem=recv_sems.at[a, k], device_id=to, device_id_type=MESH)

    def first_copies():
        cps = []
        for a in range(n):
            cps.append(copy(a, 0, me, sibling, src=own_src(a)))
            cps += [copy(a, 1 + j, me, (*chip, c), src=own_src(a)) for j, chip in enumerate(chips)]
        return cps

    def passed_copies():
        return [copy(a, 4 + j, (*chip, c), sibling) for j, chip in enumerate(chips) for a in range(n)]

    def first():
        for cp in first_copies():
            cp.start()

    def forward():
        for j, chip in enumerate(chips):
            for a in range(n):
                copy(a, 1 + j, (*chip, c), me).wait_recv()
                copy(a, 4 + j, (*chip, c), sibling).start()

    def finish():
        for a in range(n):
            copy(a, 0, sibling, me).wait_recv()
            for j, chip in enumerate(chips):
                copy(a, 4 + j, (*chip, 1 - c), me).wait_recv()
        for cp in first_copies() + passed_copies():
            cp.wait_send()

    return first, forward, finish


def _gather_comm(arrays, forward_step):
    n = len(arrays)

    def steps(cout, sems):
        view = lambda a, dev: cout[a].at[_device_index(dev)]
        return _gather_steps(n, view, lambda a: view(a, _coords()), sems[0], sems[1])

    hooks = [(0, True, lambda cin, cout, sems: steps(cout, sems)[0]()),
             (forward_step, True, lambda cin, cout, sems: steps(cout, sems)[1]()),
             (-1, False, lambda cin, cout, sems: steps(cout, sems)[2]())]
    return _Comm(list(arrays), [jax.ShapeDtypeStruct(a.shape, a.dtype) for a in arrays], {i: i for i in range(n)},
                 [pltpu.SemaphoreType.DMA((n, 7)), pltpu.SemaphoreType.DMA((n, 7))], hooks)


def _scatter_comm(grads):
    n = len(grads)

    def copies(cin, cout, sems):
        send_sems, recv_sems, local_sems = sems
        x, y, c = _coords()
        me = _device_index((x, y, c))
        mine = [pltpu.make_async_copy(cin[a].at[me], cout[a].at[me], local_sems.at[a]) for a in range(n)]
        sends, recvs = [], []
        for a in range(n):
            for mask in range(1, NDEV):
                peer = (1 - x if mask & 4 else x, 1 - y if mask & 2 else y, 1 - c if mask & 1 else c)
                p = _device_index(peer)
                kw = dict(send_sem=send_sems.at[a, mask - 1], recv_sem=recv_sems.at[a, mask - 1],
                          device_id=peer, device_id_type=MESH)
                sends.append(pltpu.make_async_remote_copy(src_ref=cin[a].at[p], dst_ref=cout[a].at[me], **kw))
                recvs.append(pltpu.make_async_remote_copy(src_ref=cin[a].at[p], dst_ref=cout[a].at[p], **kw))
        return mine, sends, recvs

    def start(cin, cout, sems):
        mine, sends, _ = copies(cin, cout, sems)
        for cp in mine + sends:
            cp.start()

    def finish(cin, cout, sems):
        mine, sends, recvs = copies(cin, cout, sems)
        for cp in recvs:
            cp.wait_recv()
        for cp in sends:
            cp.wait_send()
        for cp in mine:
            cp.wait()

    return _Comm(list(grads), [jax.ShapeDtypeStruct(g.shape, g.dtype) for g in grads], {},
                 [pltpu.SemaphoreType.DMA((n, NDEV - 1)), pltpu.SemaphoreType.DMA((n, NDEV - 1)),
                  pltpu.SemaphoreType.DMA((n,))],
                 [(0, True, start), (-1, False, finish)])


NOW_ITEMS = (0, 5, 6)
POOL_ITEM = 5


def _cast_gather_first(shards):
    n = len(shards)
    dtypes = [BF16] * 6 + [F32]
    out_shapes = [(NDEV,) + s.shape for s in shards]
    g, rows, cols = shards[POOL_ITEM].shape
    out_shapes[POOL_ITEM] = (g, rows * NDEV, cols)

    def body(*refs):
        ins, outs, stage = refs[:n], refs[n:2 * n], refs[2 * n:3 * n]
        send_sems, recv_sems, local_sems = refs[3 * n:]

        def view(a, dev):
            i = _device_index(dev)
            if a == POOL_ITEM:
                return outs[a].at[:, pl.ds(i * rows, rows), :]
            return outs[a].at[i]

        for a in range(n):
            stage[a][...] = ins[a][...].astype(dtypes[a])
        mine = [pltpu.make_async_copy(stage[a], view(a, _coords()), local_sems.at[a]) for a in range(n)]
        for cp in mine:
            cp.start()
        first, forward, finish = _gather_steps(
            len(NOW_ITEMS), lambda k, dev: view(NOW_ITEMS[k], dev), lambda k: stage[NOW_ITEMS[k]], send_sems, recv_sems)
        first()
        forward()
        finish()
        for cp in mine:
            cp.wait()

    return pl.pallas_call(
        body, name="cast_gather_first",
        in_specs=[VMEM_SPEC] * n, out_specs=[HBM_SPEC] * n,
        out_shape=[jax.ShapeDtypeStruct(s, d) for s, d in zip(out_shapes, dtypes)],
        scratch_shapes=[pltpu.VMEM(s.shape, d) for s, d in zip(shards, dtypes)]
        + [pltpu.SemaphoreType.DMA((len(NOW_ITEMS), 7)), pltpu.SemaphoreType.DMA((len(NOW_ITEMS), 7)),
           pltpu.SemaphoreType.DMA((n,))],
        compiler_params=pltpu.CompilerParams(vmem_limit_bytes=VMEM_LIMIT),
    )(*shards)


def _inproj(x, g1, win, comm, tm=512):
    T, D = x.shape
    nb, _, bn = win.shape

    def body(x_ref, g_ref, w_ref, proj_ref, h_ref):
        @pl.when(pl.program_id(1) == 0)
        def _():
            xv = x_ref[...]
            h_ref[...] = (xv * _rsq(xv) * g_ref[...]).astype(BF16)

        proj_ref[...] = jnp.dot(h_ref[...], w_ref[0], preferred_element_type=F32)

    return _host_call(
        body, name="inproj", grid=(T // tm, nb), comm=comm, args=(x, g1, win),
        in_specs=[pl.BlockSpec((tm, D), lambda i, j: (i, 0)),
                  pl.BlockSpec((1, D), lambda i, j: (0, 0)),
                  pl.BlockSpec((1, D, bn), lambda i, j: (j, 0, 0))],
        out_specs=[pl.BlockSpec((tm, bn), lambda i, j: (i, j)),
                   pl.BlockSpec((tm, D), lambda i, j: (i, 0))],
        out_shape=[jax.ShapeDtypeStruct((T, nb * bn), F32), jax.ShapeDtypeStruct((T, D), BF16)])


def _mixer_fwd(proj, x, cw, pw, ps, wout, g2, g3, comm, tm=256):
    T, D = x.shape
    P = proj.shape[1]

    def body(proj_ref, x_ref, cw_ref, pw_ref, ps_ref, wout_ref, g2_ref, g3_ref,
             x1_ref, hf_ref, mixed_ref, mo_ref, cu_carry, v_carry):
        i = pl.program_id(0)

        @pl.when(i == 0)
        def _():
            cu_carry[...] = jnp.zeros_like(cu_carry)
            v_carry[...] = jnp.zeros_like(v_carry)

        for h in range(CONV_HEADS):
            lo = h * HEAD_DIM
            gate_b = proj_ref[:, lo:lo + HEAD_DIM]
            cu = proj_ref[:, CONV_WIDTH + lo:CONV_WIDTH + lo + HEAD_DIM] * \
                proj_ref[:, 2 * CONV_WIDTH + lo:2 * CONV_WIDTH + lo + HEAD_DIM]
            ext = jnp.concatenate([cu_carry[:, lo:lo + HEAD_DIM], cu], axis=0)
            c1 = pltpu.roll(ext, 1, 0)[HALO:]
            c2 = pltpu.roll(ext, 2, 0)[HALO:]
            ya = gate_b * (cw_ref[h, 2:3, :] * cu + cw_ref[h, 1:2, :] * c1 + cw_ref[h, 0:1, :] * c2)
            mixed_ref[:, lo:lo + HEAD_DIM] = (ya * _rsq(ya)).astype(BF16)
            cu_carry[:, lo:lo + HEAD_DIM] = cu[tm - HALO:]

        for gi, w in enumerate(POOL_WINDOWS):
            lo = gi * POOL_GROUP_DIM
            v = proj_ref[:, 3 * CONV_WIDTH + lo:3 * CONV_WIDTH + lo + POOL_GROUP_DIM]
            ext = jnp.concatenate([v_carry[:, lo:lo + POOL_GROUP_DIM], v], axis=0)
            pooled = _window_sum(ext, w, True)[HALO:] * _inv_count(i * tm, tm, w) - v
            y = jnp.dot(pooled.astype(BF16), pw_ref[gi], preferred_element_type=F32)
            yb = y * _rsq(y) * ps_ref[:, lo:lo + POOL_GROUP_DIM]
            mixed_ref[:, CONV_WIDTH + lo:CONV_WIDTH + lo + POOL_GROUP_DIM] = yb.astype(BF16)
            v_carry[:, lo:lo + POOL_GROUP_DIM] = v[tm - HALO:]

        mo = jnp.dot(mixed_ref[...], wout_ref[...], preferred_element_type=F32)
        mo_ref[...] = mo
        x1 = x_ref[...] + mo * _rsq(mo) * g2_ref[...]
        x1_ref[...] = x1
        hf_ref[...] = (x1 * _rsq(x1) * g3_ref[...]).astype(BF16)

    row = lambda n: pl.BlockSpec((tm, n), lambda i: (i, 0))
    return _host_call(
        body, name="mixer_fwd", grid=(T // tm,), comm=comm, args=(proj, x, cw, pw, ps, wout, g2, g3),
        in_specs=[row(P), row(D), _whole(cw.shape), _whole(pw.shape), _whole(ps.shape),
                  _whole(wout.shape), _whole(g2.shape), _whole(g3.shape)],
        out_specs=[row(D), row(D), row(D), row(D)],
        out_shape=[jax.ShapeDtypeStruct((T, D), F32), jax.ShapeDtypeStruct((T, D), BF16),
                   jax.ShapeDtypeStruct((T, D), BF16), jax.ShapeDtypeStruct((T, D), F32)],
        scratch_shapes=[pltpu.VMEM((HALO, CONV_WIDTH), F32), pltpu.VMEM((HALO, CONV_WIDTH), F32)])


def _ffn_up(hf, wg, wu, tm=1024):
    T, D = hf.shape
    nb, _, bf = wg.shape

    def body(hf_ref, wg_ref, wu_ref, g_ref, u_ref, a_ref):
        hv = hf_ref[...]
        g = jnp.dot(hv, wg_ref[0], preferred_element_type=F32)
        u = jnp.dot(hv, wu_ref[0], preferred_element_type=F32)
        g_ref[0] = g.astype(BF16)
        u_ref[0] = u.astype(BF16)
        a_ref[0] = (g * jax.nn.sigmoid(g) * u).astype(BF16)

    wspec = pl.BlockSpec((1, D, bf), lambda i, j: (j, 0, 0))
    ospec = pl.BlockSpec((1, tm, bf), lambda i, j: (j, i, 0))
    oshape = jax.ShapeDtypeStruct((nb, T, bf), BF16)
    return _host_call(
        body, name="ffn_up", grid=(T // tm, nb), args=(hf, wg, wu),
        in_specs=[pl.BlockSpec((tm, D), lambda i, j: (i, 0)), wspec, wspec],
        out_specs=[ospec, ospec, ospec], out_shape=[oshape, oshape, oshape])[0]


def _ffn_down_loss(a, wd, x1, tgt, g4, tm=512):
    nb, T, bf = a.shape
    D = x1.shape[1]
    nt = T // tm

    def body(a_ref, wd_ref, x1_ref, tgt_ref, g4_ref, dy_ref, dff_ref, loss_ref, dg4_ref, acc_ref, lacc_ref):
        i, j = pl.program_id(0), pl.program_id(1)

        @pl.when((i == 0) & (j == 0))
        def _():
            lacc_ref[...] = jnp.zeros_like(lacc_ref)
            dg4_ref[...] = jnp.zeros_like(dg4_ref)

        @pl.when(j == 0)
        def _():
            acc_ref[...] = jnp.zeros_like(acc_ref)

        acc_ref[...] += jnp.dot(a_ref[0], wd_ref[0], preferred_element_type=F32)

        @pl.when(j == nb - 1)
        def _():
            ff = acc_ref[...]
            r = _rsq(ff)
            n = ff * r
            g4v = g4_ref[...]
            e = x1_ref[...] + n * g4v - tgt_ref[...]
            lacc_ref[...] += jnp.sum(e * e, axis=0, keepdims=True)
            dy = e * (1.0 / D)
            dy_ref[...] = dy
            dg4_ref[...] += jnp.sum(dy * n, axis=0, keepdims=True)
            dff_ref[...] = _norm_bwd(dy * g4v, n, r).astype(BF16)

        @pl.when((i == nt - 1) & (j == nb - 1))
        def _():
            loss_ref[...] = jnp.full(loss_ref.shape, (0.5 / D) * jnp.sum(lacc_ref[...]), F32)

    row = pl.BlockSpec((tm, D), lambda i, j: (i, 0))
    vec = pl.BlockSpec((1, D), lambda i, j: (0, 0))
    return _host_call(
        body, name="ffn_down_loss", grid=(nt, nb), args=(a, wd, x1, tgt, g4),
        in_specs=[pl.BlockSpec((1, tm, bf), lambda i, j: (j, i, 0)),
                  pl.BlockSpec((1, bf, D), lambda i, j: (j, 0, 0)), row, row, vec],
        out_specs=[row, row, pl.BlockSpec((1, 128), lambda i, j: (0, 0)), vec],
        out_shape=[jax.ShapeDtypeStruct((T, D), F32), jax.ShapeDtypeStruct((T, D), BF16),
                   jax.ShapeDtypeStruct((1, 128), F32), jax.ShapeDtypeStruct((1, D), F32)],
        scratch_shapes=[pltpu.VMEM((tm, D), F32), pltpu.VMEM((1, D), F32)])[0]


def _ffn_bwd_act(dff, wd, g, u, comm, tm=1024):
    T, D = dff.shape
    nb, bf, _ = wd.shape

    def body(dff_ref, wd_ref, g_ref, u_ref, dg_ref, du_ref):
        da = lax.dot_general(dff_ref[...], wd_ref[0], NT_DIMS, preferred_element_type=F32)
        gv = g_ref[0].astype(F32)
        s = jax.nn.sigmoid(gv)
        du_ref[0] = (da * (gv * s)).astype(BF16)
        dg_ref[0] = (da * u_ref[0].astype(F32) * (s * (1.0 + gv * (1.0 - s)))).astype(BF16)

    blk = pl.BlockSpec((1, tm, bf), lambda i, j: (j, i, 0))
    oshape = jax.ShapeDtypeStruct((nb, T, bf), BF16)
    return _host_call(
        body, name="ffn_bwd_act", grid=(T // tm, nb), comm=comm, args=(dff, wd, g, u),
        in_specs=[pl.BlockSpec((tm, D), lambda i, j: (i, 0)),
                  pl.BlockSpec((1, bf, D), lambda i, j: (j, 0, 0)), blk, blk],
        out_specs=[blk, blk], out_shape=[oshape, oshape])


def _ffn_bwd_in(dg, du, wg, wu, x1, dy, g3, comm, tm=512):
    nb, T, bf = dg.shape
    D = x1.shape[1]

    def body(dg_ref, du_ref, wg_ref, wu_ref, x1_ref, dy_ref, g3_ref, dx1_ref, dg3_ref, acc_ref):
        i, j = pl.program_id(0), pl.program_id(1)

        @pl.when((i == 0) & (j == 0))
        def _():
            dg3_ref[...] = jnp.zeros_like(dg3_ref)

        @pl.when(j == 0)
        def _():
            acc_ref[...] = jnp.zeros_like(acc_ref)

        acc_ref[...] += (lax.dot_general(dg_ref[0], wg_ref[0], NT_DIMS, preferred_element_type=F32)
                         + lax.dot_general(du_ref[0], wu_ref[0], NT_DIMS, preferred_element_type=F32))

        @pl.when(j == nb - 1)
        def _():
            dhf = acc_ref[...]
            x1v = x1_ref[...]
            r = _rsq(x1v)
            n = x1v * r
            dg3_ref[...] += jnp.sum(dhf * n, axis=0, keepdims=True)
            dx1_ref[...] = dy_ref[...] + _norm_bwd(dhf * g3_ref[...], n, r)

    row = pl.BlockSpec((tm, D), lambda i, j: (i, 0))
    vec = pl.BlockSpec((1, D), lambda i, j: (0, 0))
    ablk = pl.BlockSpec((1, tm, bf), lambda i, j: (j, i, 0))
    wblk = pl.BlockSpec((1, D, bf), lambda i, j: (j, 0, 0))
    return _host_call(
        body, name="ffn_bwd_in", grid=(T // tm, nb), comm=comm, args=(dg, du, wg, wu, x1, dy, g3),
        in_specs=[ablk, ablk, wblk, wblk, row, row, vec],
        out_specs=[row, vec],
        out_shape=[jax.ShapeDtypeStruct((T, D), F32), jax.ShapeDtypeStruct((1, D), F32)],
        scratch_shapes=[pltpu.VMEM((tm, D), F32)])


def _wgrad(name, lhs, rhs, lhs_spec, rhs_spec, n_rhs, M, N, nb, nk, comm=None):
    def body(*refs):
        l_ref, r_refs = refs[0], refs[1:1 + n_rhs]
        o_refs, acc_refs = refs[1 + n_rhs:1 + 2 * n_rhs], refs[1 + 2 * n_rhs:]
        k = pl.program_id(1)
        tile = lambda ref: ref[0] if len(ref.shape) == 3 else ref[...]
        lv = tile(l_ref)
        for r_ref, o_ref, acc_ref in zip(r_refs, o_refs, acc_refs):
            p = lax.dot_general(lv, tile(r_ref), TN_DIMS, preferred_element_type=F32)

            @pl.when(k == 0)
            def _():
                acc_ref[...] = p

            @pl.when(k > 0)
            def _():
                acc_ref[...] += p

            @pl.when(k == nk - 1)
            def _():
                o_ref[0] = acc_ref[...].astype(BF16)

    oblk = pl.BlockSpec((1, M, N), lambda j, k: (j, 0, 0))
    oshape = jax.ShapeDtypeStruct((nb, M, N), BF16)
    return _host_call(
        body, name=name, grid=(nb, nk), comm=comm, args=(lhs, *rhs),
        in_specs=[lhs_spec] + [rhs_spec] * n_rhs, out_specs=[oblk] * n_rhs, out_shape=[oshape] * n_rhs,
        scratch_shapes=[pltpu.VMEM((M, N), F32)] * n_rhs)


def _wgrad_down(a, dff, tk=1024):
    nb, T, M = a.shape
    N = dff.shape[1]
    return _wgrad("wgrad_down", a, [dff], pl.BlockSpec((1, tk, M), lambda j, k: (j, k, 0)),
                  pl.BlockSpec((tk, N), lambda j, k: (k, 0)), 1, M, N, nb, T // tk)[0][0]


def _wgrad_gate_up(hf, dg, du, tk=512):
    T, M = hf.shape
    nb, _, N = dg.shape
    return _wgrad("wgrad_gate_up", hf, [dg, du], pl.BlockSpec((tk, M), lambda j, k: (k, 0)),
                  pl.BlockSpec((1, tk, N), lambda j, k: (j, k, 0)), 2, M, N, nb, T // tk)[0]


def _wgrad_out(mixed, dmo, nb=NDEV, tk=1024):
    T, D = mixed.shape
    return _wgrad("wgrad_out", mixed, [dmo], pl.BlockSpec((tk, D // nb), lambda j, k: (k, j)),
                  pl.BlockSpec((tk, D), lambda j, k: (k, 0)), 1, D // nb, D, nb, T // tk)[0][0]


def _wgrad_in(h, dproj, comm, nb=NDEV, tk=1024):
    T, D = h.shape
    bn = dproj.shape[1] // nb
    res, cres = _wgrad("wgrad_in", h, [dproj], pl.BlockSpec((tk, D), lambda j, k: (k, 0)),
                       pl.BlockSpec((tk, bn), lambda j, k: (k, j)), 1, D, bn, nb, T // tk, comm=comm)
    return res[0], cres


def _mixer_bwd(dx1, mo, proj, cw, pw, ps, wout, g2, tm=256):
    T, D = dx1.shape
    P = proj.shape[1]
    nt = T // tm
    n_ext = tm + HALO
    hb = tm // HALO

    def body(dx1_ref, mo_ref, proj_ref, hc_ref, hu_ref, hv_ref, cw_ref, pw_ref, ps_ref, wout_ref, g2_ref,
             dmo_ref, dproj_ref, dg2_ref, dcw_ref, dps_ref, dpw_ref, dmix_ref, dconv_carry, q_carry):
        i = pl.program_id(0)
        tile = nt - 1 - i

        @pl.when(i == 0)
        def _():
            dconv_carry[...] = jnp.zeros_like(dconv_carry)
            q_carry[...] = jnp.zeros_like(q_carry)
            dg2_ref[...] = jnp.zeros_like(dg2_ref)
            dcw_ref[...] = jnp.zeros_like(dcw_ref)
            dps_ref[...] = jnp.zeros_like(dps_ref)
            dpw_ref[...] = jnp.zeros_like(dpw_ref)

        mov = mo_ref[...]
        r2 = _rsq(mov)
        n2 = mov * r2
        dx1v = dx1_ref[...]
        dg2_ref[...] += jnp.sum(dx1v * n2, axis=0, keepdims=True)
        dmo = _norm_bwd(dx1v * g2_ref[...], n2, r2).astype(BF16)
        dmo_ref[...] = dmo
        dmix_ref[...] = lax.dot_general(dmo, wout_ref[...], NT_DIMS, preferred_element_type=F32)

        has_prev = (tile > 0).astype(F32)

        for h in range(CONV_HEADS):
            lo = h * HEAD_DIM
            sl = slice(lo, lo + HEAD_DIM)
            gate_b = proj_ref[:, lo:lo + HEAD_DIM]
            gate_c = proj_ref[:, CONV_WIDTH + lo:CONV_WIDTH + lo + HEAD_DIM]
            uu = proj_ref[:, 2 * CONV_WIDTH + lo:2 * CONV_WIDTH + lo + HEAD_DIM]
            cu = gate_c * uu
            ext = jnp.concatenate([hc_ref[:, sl] * hu_ref[:, sl] * has_prev, cu], axis=0)
            c1 = pltpu.roll(ext, 1, 0)[HALO:]
            c2 = pltpu.roll(ext, 2, 0)[HALO:]
            w0, w1, w2 = cw_ref[h, 0:1, :], cw_ref[h, 1:2, :], cw_ref[h, 2:3, :]
            conv = w2 * cu + w1 * c1 + w0 * c2
            ya = gate_b * conv
            ra = _rsq(ya)
            dya = _norm_bwd(dmix_ref[:, sl], ya * ra, ra)
            dconv = dya * gate_b
            dcw_ref[h, 0:1, :] += jnp.sum(dconv * c2, axis=0, keepdims=True)
            dcw_ref[h, 1:2, :] += jnp.sum(dconv * c1, axis=0, keepdims=True)
            dcw_ref[h, 2:3, :] += jnp.sum(dconv * cu, axis=0, keepdims=True)
            extd = jnp.concatenate([dconv, dconv_carry[:, sl]], axis=0)
            d1 = pltpu.roll(extd, n_ext - 1, 0)[:tm]
            d2 = pltpu.roll(extd, n_ext - 2, 0)[:tm]
            dcu = w2 * dconv + w1 * d1 + w0 * d2
            dconv_carry[:, sl] = dconv[:HALO]
            dproj_ref[:, lo:lo + HEAD_DIM] = (dya * conv).astype(BF16)
            dproj_ref[:, CONV_WIDTH + lo:CONV_WIDTH + lo + HEAD_DIM] = (dcu * uu).astype(BF16)
            dproj_ref[:, 2 * CONV_WIDTH + lo:2 * CONV_WIDTH + lo + HEAD_DIM] = (dcu * gate_c).astype(BF16)

        for gi, w in enumerate(POOL_WINDOWS):
            lo = gi * POOL_GROUP_DIM
            sl = slice(lo, lo + POOL_GROUP_DIM)
            v = proj_ref[:, 3 * CONV_WIDTH + lo:3 * CONV_WIDTH + lo + POOL_GROUP_DIM]
            inv = _inv_count(tile * tm, tm, w)
            ext = jnp.concatenate([hv_ref[:, sl] * has_prev, v], axis=0)
            pooled = (_window_sum(ext, w, True)[HALO:] * inv - v).astype(BF16)
            y = jnp.dot(pooled, pw_ref[gi], preferred_element_type=F32)
            rp = _rsq(y)
            nb_ = y * rp
            dyb = dmix_ref[:, CONV_WIDTH + lo:CONV_WIDTH + lo + POOL_GROUP_DIM]
            dps_ref[:, sl] += jnp.sum(dyb * nb_, axis=0, keepdims=True)
            dy = _norm_bwd(dyb * ps_ref[:, sl], nb_, rp).astype(BF16)
            dpw_ref[gi] += lax.dot_general(pooled, dy, TN_DIMS, preferred_element_type=F32)
            dpooled = lax.dot_general(dy, pw_ref[gi], NT_DIMS, preferred_element_type=F32)
            q = dpooled * inv
            extq = jnp.concatenate([q, q_carry[:, sl]], axis=0)
            dv = _window_sum(extq, w, False)[:tm] - dpooled
            q_carry[:, sl] = q[:HALO]
            dproj_ref[:, 3 * CONV_WIDTH + lo:3 * CONV_WIDTH + lo + POOL_GROUP_DIM] = dv.astype(BF16)

    rev = lambda n: pl.BlockSpec((tm, n), lambda i: (nt - 1 - i, 0))

    def halo(col):
        return pl.BlockSpec((HALO, CONV_WIDTH), lambda i: (jnp.maximum((nt - 1 - i) * hb - 1, 0), col))

    return _host_call(
        body, name="mixer_bwd", grid=(nt,), args=(dx1, mo, proj, proj, proj, proj, cw, pw, ps, wout, g2),
        in_specs=[rev(D), rev(D), rev(P), halo(1), halo(2), halo(3), _whole(cw.shape), _whole(pw.shape),
                  _whole(ps.shape), _whole(wout.shape), _whole(g2.shape)],
        out_specs=[rev(D), rev(P), pl.BlockSpec((1, D), lambda i: (0, 0)),
                   pl.BlockSpec(cw.shape, lambda i: (0, 0, 0)), pl.BlockSpec(ps.shape, lambda i: (0, 0)),
                   pl.BlockSpec(pw.shape, lambda i: (0, 0, 0))],
        out_shape=[jax.ShapeDtypeStruct((T, D), BF16), jax.ShapeDtypeStruct((T, P), BF16),
                   jax.ShapeDtypeStruct((1, D), F32), jax.ShapeDtypeStruct(cw.shape, F32),
                   jax.ShapeDtypeStruct(ps.shape, F32), jax.ShapeDtypeStruct(pw.shape, F32)],
        scratch_shapes=[pltpu.VMEM((tm, D), F32), pltpu.VMEM((HALO, CONV_WIDTH), F32),
                        pltpu.VMEM((HALO, CONV_WIDTH), F32)])[0]


def _inproj_bwd(dproj, win, x, dx1, g1, comm, tm=512):
    T, D = x.shape
    nb, _, bn = win.shape

    def body(dp_ref, w_ref, x_ref, dx1_ref, g1_ref, gx_ref, dg1_ref, acc_ref):
        i, j = pl.program_id(0), pl.program_id(1)

        @pl.when((i == 0) & (j == 0))
        def _():
            dg1_ref[...] = jnp.zeros_like(dg1_ref)

        @pl.when(j == 0)
        def _():
            acc_ref[...] = jnp.zeros_like(acc_ref)

        acc_ref[...] += lax.dot_general(dp_ref[...], w_ref[0], NT_DIMS, preferred_element_type=F32)

        @pl.when(j == nb - 1)
        def _():
            dh = acc_ref[...]
            xv = x_ref[...]
            r = _rsq(xv)
            n = xv * r
            dg1_ref[...] += jnp.sum(dh * n, axis=0, keepdims=True)
            gx_ref[...] = dx1_ref[...] + _norm_bwd(dh * g1_ref[...], n, r)

    row = pl.BlockSpec((tm, D), lambda i, j: (i, 0))
    vec = pl.BlockSpec((1, D), lambda i, j: (0, 0))
    return _host_call(
        body, name="inproj_bwd", grid=(T // tm, nb), comm=comm, args=(dproj, win, x, dx1, g1),
        in_specs=[pl.BlockSpec((tm, bn), lambda i, j: (i, j)),
                  pl.BlockSpec((1, D, bn), lambda i, j: (j, 0, 0)), row, row, vec],
        out_specs=[row, vec],
        out_shape=[jax.ShapeDtypeStruct((T, D), F32), jax.ShapeDtypeStruct((1, D), F32)],
        scratch_shapes=[pltpu.VMEM((tm, D), F32)])


def _adamw(w, g, m, v):
    m = ADAM_B1 * m + (1.0 - ADAM_B1) * g
    v = ADAM_B2 * v + (1.0 - ADAM_B2) * jnp.square(g)
    m_hat = m / (1.0 - ADAM_B1 ** ADAM_STEP)
    v_hat = v / (1.0 - ADAM_B2 ** ADAM_STEP)
    delta = -ADAM_LR * (m_hat / (jnp.sqrt(v_hat) + ADAM_EPS) + ADAM_WD * w)
    return delta, m, v


def _sum_adamw(parts, w, m, v, name, tr):
    r, cd = w.shape

    def body(p_ref, w_ref, m_ref, v_ref, g_ref, d_ref, mo_ref, vo_ref):
        g = p_ref[0].astype(F32)
        for k in range(1, NDEV):
            g = g + p_ref[k].astype(F32)
        g_ref[...] = g
        d_ref[...], mo_ref[...], vo_ref[...] = _adamw(w_ref[...], g, m_ref[...], v_ref[...])

    blk = pl.BlockSpec((tr, cd), lambda i: (i, 0))
    shp = jax.ShapeDtypeStruct((r, cd), F32)
    return pl.pallas_call(
        body, name=name, grid=(r // tr,),
        in_specs=[pl.BlockSpec((NDEV, tr, cd), lambda i: (0, i, 0)), blk, blk, blk],
        out_specs=[blk] * 4, out_shape=[shp] * 4,
        compiler_params=_params("arbitrary"),
    )(parts, w, m, v)


def _small_reduce_adamw(vec_grads, dps, dcw, dpw, vec_state, ps_state, cw_state, pw_state):
    D = vec_grads[0].shape[1]
    pw_rows = pw_state[0].shape[1]
    states = list(vec_state) + [ps_state, cw_state, pw_state]
    n_in = 4 + 3 + 3 * len(states)
    n_out = 4 * len(states)

    def body(*refs):
        dg = refs[0:4]
        dps_ref, dcw_ref, dpw_ref = refs[4:7]
        st = refs[7:n_in]
        outs = refs[n_in:n_in + n_out]
        pack, gat, cbuf, pbuf, send_sems, recv_sems, local_sems = refs[n_in + n_out:]
        x, y, c = _coords()
        me = _device_index((x, y, c))

        pack[...] = jnp.zeros_like(pack)
        for k in range(4):
            pack[k:k + 1, :] = dg[k][...]
        pack[4:5, 0:dps_ref.shape[1]] = dps_ref[...]

        def pw_slice(i):
            return dpw_ref.at[:, pl.ds(i * pw_rows, pw_rows), :]

        mine = [pltpu.make_async_copy(pack, gat.at[me], local_sems.at[0]),
                pltpu.make_async_copy(dcw_ref.at[me], cbuf.at[me], local_sems.at[1]),
                pltpu.make_async_copy(pw_slice(me), pbuf.at[me], local_sems.at[2])]
        for cp in mine:
            cp.start()
        sends, recvs = [], []
        for mask in range(1, NDEV):
            peer = (1 - x if mask & 4 else x, 1 - y if mask & 2 else y, 1 - c if mask & 1 else c)
            p = _device_index(peer)
            for k, (src, buf) in enumerate(((pack, gat), (dcw_ref.at[p], cbuf), (pw_slice(p), pbuf))):
                kw = dict(send_sem=send_sems.at[mask, k], recv_sem=recv_sems.at[mask, k],
                          device_id=peer, device_id_type=MESH)
                sends.append(pltpu.make_async_remote_copy(src_ref=src, dst_ref=buf.at[me], **kw))
                recvs.append(pltpu.make_async_remote_copy(src_ref=src, dst_ref=buf.at[p], **kw))
                sends[-1].start()
        for cp in recvs:
            cp.wait_recv()
        for cp in sends:
            cp.wait_send()
        for cp in mine:
            cp.wait()

        def slot_sum(buf):
            s = buf[0]
            for k in range(1, NDEV):
                s = s + buf[k]
            return s

        vec = slot_sum(gat)
        grads = [vec[k:k + 1, :] for k in range(4)] + [vec[4:5, 0:dps_ref.shape[1]], slot_sum(cbuf), slot_sum(pbuf)]
        for k, g in enumerate(grads):
            w_ref, m_ref, v_ref = st[3 * k:3 * k + 3]
            outs[4 * k][...] = g
            outs[4 * k + 1][...], outs[4 * k + 2][...], outs[4 * k + 3][...] = _adamw(
                w_ref[...], g, m_ref[...], v_ref[...])

    flat_state = [a for s in states for a in s]
    out_shape = [jax.ShapeDtypeStruct(s[0].shape, F32) for s in states for _ in range(4)]
    return pl.pallas_call(
        body, name="small_reduce_adamw",
        in_specs=[VMEM_SPEC] * n_in, out_specs=[VMEM_SPEC] * n_out, out_shape=out_shape,
        scratch_shapes=[pltpu.VMEM((NDEV, D), F32), pltpu.VMEM((NDEV, NDEV, D), F32),
                        pltpu.VMEM((NDEV,) + cw_state[0].shape, F32), pltpu.VMEM((NDEV,) + pw_state[0].shape, F32),
                        pltpu.SemaphoreType.DMA((NDEV, 3)), pltpu.SemaphoreType.DMA((NDEV, 3)),
                        pltpu.SemaphoreType.DMA((3,))],
        compiler_params=pltpu.CompilerParams(vmem_limit_bytes=VMEM_LIMIT),
    )(*vec_grads, dps, dcw, dpw, *flat_state)


ROW_TILE = dict(w_in=512, w_gate=256, w_up=256, w_down=176, w_out=128)
FORWARD_STEP = dict(inproj=116, mixer_fwd=27)


def kernel(x, ln_mix_pre, w_in, conv_w, pool_w, pool_scale, w_out, ln_mix_post, ln_ffn_pre, w_gate, w_up, w_down, ln_ffn_post, loss_target, m_ln_mix_pre, m_w_in, m_conv_w, m_pool_w, m_pool_scale, m_w_out, m_ln_mix_post, m_ln_ffn_pre, m_w_gate, m_w_up, m_w_down, m_ln_ffn_post, v_ln_mix_pre, v_w_in, v_conv_w, v_pool_w, v_pool_scale, v_w_out, v_ln_mix_post, v_ln_ffn_pre, v_w_gate, v_w_up, v_w_down, v_ln_ffn_post):
    D = x.shape[2]
    xs, tgt = x[0], loss_target[0]
    win, wg, wu, wd, wout, pw, cw = _cast_gather_first(
        [w_in[0], w_gate[0], w_up[0], w_down[0], w_out[0], pool_w[0], conv_w[0]])

    (proj, h), (wout, wg, wu) = _inproj(xs, ln_mix_pre, win, _gather_comm([wout, wg, wu], FORWARD_STEP["inproj"]))
    wout2 = wout.reshape(D, D)
    (x1, hf, mixed, mo), (wd,) = _mixer_fwd(proj, xs, cw, pw, pool_scale, wout2, ln_mix_post, ln_ffn_pre,
                                            _gather_comm([wd], FORWARD_STEP["mixer_fwd"]))
    g, u, a = _ffn_up(hf, wg, wu)
    dy, dff, loss, dg4 = _ffn_down_loss(a, wd, x1, tgt, ln_ffn_post)

    dwd = _wgrad_down(a, dff)
    (dg, du), (dwd_parts,) = _ffn_bwd_act(dff, wd, g, u, _scatter_comm([dwd]))
    dwg, dwu = _wgrad_gate_up(hf, dg, du)
    (dx1, dg3), (dwg_parts, dwu_parts) = _ffn_bwd_in(dg, du, wg, wu, x1, dy, ln_ffn_pre, _scatter_comm([dwg, dwu]))
    dmo, dproj, dg2, dcw, dps, dpw = _mixer_bwd(dx1, mo, proj, cw, pw, pool_scale, wout2, ln_mix_post)
    dwout = _wgrad_out(mixed, dmo)
    dwin, (dwout_parts,) = _wgrad_in(h, dproj, _scatter_comm([dwout]))
    (gx, dg1), (dwin_parts,) = _inproj_bwd(dproj, win, xs, dx1, ln_mix_pre, _scatter_comm([dwin]))

    loss = lax.psum(loss[0, 0], ("x", "y", "c"))
    res = {}
    for k, parts, w, m, v in (("w_down", dwd_parts, w_down, m_w_down, v_w_down),
                              ("w_gate", dwg_parts, w_gate, m_w_gate, v_w_gate),
                              ("w_up", dwu_parts, w_up, m_w_up, v_w_up),
                              ("w_out", dwout_parts, w_out, m_w_out, v_w_out),
                              ("w_in", dwin_parts, w_in, m_w_in, v_w_in)):
        res[k] = [o.reshape(w.shape) for o in _sum_adamw(parts, w[0], m[0], v[0], "sum_adamw_" + k, ROW_TILE[k])]

    small = _small_reduce_adamw(
        [dg1, dg2, dg3, dg4], dps, dcw, dpw,
        [(ln_mix_pre, m_ln_mix_pre, v_ln_mix_pre), (ln_mix_post, m_ln_mix_post, v_ln_mix_post),
         (ln_ffn_pre, m_ln_ffn_pre, v_ln_ffn_pre), (ln_ffn_post, m_ln_ffn_post, v_ln_ffn_post)],
        (pool_scale, m_pool_scale, v_pool_scale), (conv_w[0], m_conv_w[0], v_conv_w[0]),
        (pool_w[0], m_pool_w[0], v_pool_w[0]))
    small_names = ["ln_mix_pre", "ln_mix_post", "ln_ffn_pre", "ln_ffn_post", "pool_scale", "conv_w", "pool_w"]
    shapes = dict(conv_w=conv_w.shape, pool_w=pool_w.shape)
    for i, k in enumerate(small_names):
        res[k] = [o.reshape(shapes[k]) if k in shapes else o for o in small[4 * i:4 * i + 4]]

    order = ["ln_mix_pre", "w_in", "conv_w", "pool_w", "pool_scale", "w_out", "ln_mix_post", "ln_ffn_pre",
             "w_gate", "w_up", "w_down", "ln_ffn_post"]
    return (loss, gx[None], *[res[k][0] for k in order], *[res[k][1] for k in order],
            *[res[k][2] for k in order], *[res[k][3] for k in order])
```

```python
import functools
from typing import Any, NamedTuple

import jax
import jax.numpy as jnp
from jax import lax
from jax.experimental import pallas as pl
from jax.experimental.pallas import tpu as pltpu

EPS = 1e-6
NDEV = 8
CONV_HEADS = 8
HEAD_DIM = 128
CONV_WIDTH = CONV_HEADS * HEAD_DIM
POOL_WINDOWS = (2, 4, 8, 16)
POOL_GROUP_DIM = 256
HALO = 16

ADAM_LR = 0.001
ADAM_B1 = 0.9
ADAM_B2 = 0.999
ADAM_EPS = 1e-08
ADAM_WD = 0.01
ADAM_STEP = 10

F32 = jnp.float32
BF16 = jnp.bfloat16
VMEM_LIMIT = 58 * 1024 * 1024
MESH = pl.DeviceIdType.MESH
HBM_SPEC = pl.BlockSpec(memory_space=pl.ANY)
VMEM_SPEC = pl.BlockSpec(memory_space=pltpu.VMEM)

NT_DIMS = (((1,), (1,)), ((), ()))
TN_DIMS = (((0,), (0,)), ((), ()))


def _params(*sem):
    return pltpu.CompilerParams(dimension_semantics=sem, vmem_limit_bytes=VMEM_LIMIT)


def _rsq(v):
    return lax.rsqrt(jnp.mean(v * v, axis=-1, keepdims=True) + EPS)


def _norm_bwd(dn, n, r):
    return r * (dn - n * jnp.mean(dn * n, axis=-1, keepdims=True))


def _whole(shape):
    nd = len(shape)
    return pl.BlockSpec(shape, lambda *_: (0,) * nd, pipeline_mode=pl.Buffered(1))


def _inv_count(t0, tm, w):
    t = t0 + lax.broadcasted_iota(jnp.int32, (tm, 1), 0)
    return 1.0 / jnp.minimum(t + 1, w).astype(F32)


def _late_rows(hbm_ref, buf, sem, i, tm):
    return pltpu.make_async_copy(hbm_ref.at[pl.ds(pl.multiple_of(i * tm, tm), tm), :], buf, sem)


def _window_sum(ext, w, back):
    n = ext.shape[0]
    s, shift = ext, 1
    while shift < w:
        s = s + pltpu.roll(s, shift if back else n - shift, 0)
        shift *= 2
    return s


class _Comm(NamedTuple):
    arrays: Any
    out_shape: Any
    aliases: Any
    scratch: Any
    hooks: Any


def _coords():
    return lax.axis_index("x"), lax.axis_index("y"), lax.axis_index("c")


def _other_chips(x, y):
    return [(1 - x, y), (x, 1 - y), (1 - x, 1 - y)]


def _device_index(dev):
    return 4 * dev[0] + 2 * dev[1] + dev[2]


def _host_call(body, *, name, grid, in_specs, out_specs, out_shape, args, scratch_shapes=(), comm=None):
    sem = ("arbitrary",) * len(grid)
    in_specs, out_specs, out_shape, scratch_shapes = list(in_specs), list(out_specs), list(out_shape), list(scratch_shapes)
    if comm is None:
        res = pl.pallas_call(body, name=name, grid=grid, in_specs=in_specs, out_specs=out_specs, out_shape=out_shape,
                             scratch_shapes=scratch_shapes, compiler_params=_params(*sem))(*args)
        return res, []
    n_in, n_out, n_scr = len(in_specs), len(out_specs), len(scratch_shapes)
    n_cin, n_cout = len(comm.arrays), len(comm.out_shape)
    total = functools.reduce(lambda a, b: a * b, grid)

    def wrapped(*refs):
        ins, cin = refs[:n_in], refs[n_in:n_in + n_cin]
        o0 = n_in + n_cin
        outs, cout = refs[o0:o0 + n_out], refs[o0 + n_out:o0 + n_out + n_cout]
        s0 = o0 + n_out + n_cout
        scr, sems = refs[s0:s0 + n_scr], refs[s0 + n_scr:]
        step = pl.program_id(0)
        for d in range(1, len(grid)):
            step = step * grid[d] + pl.program_id(d)
        for when, before, fn in comm.hooks:
            if before:
                pl.when(step == when % total)(functools.partial(fn, cin, cout, sems))
        body(*ins, *outs, *scr)
        for when, before, fn in comm.hooks:
            if not before:
                pl.when(step == when % total)(functools.partial(fn, cin, cout, sems))

    res = pl.pallas_call(
        wrapped, name=name, grid=grid,
        in_specs=in_specs + [HBM_SPEC] * n_cin, out_specs=out_specs + [HBM_SPEC] * n_cout,
        out_shape=out_shape + list(comm.out_shape), scratch_shapes=scratch_shapes + list(comm.scratch),
        input_output_aliases={n_in + i: n_out + o for i, o in comm.aliases.items()},
        compiler_params=_params(*sem),
    )(*args, *comm.arrays)
    return res[:n_out], res[n_out:]


def _gather_steps(n, view, own_src, send_sems, recv_sems):
    x, y, c = _coords()
    me, sibling = (x, y, c), (x, y, 1 - c)
    chips = _other_chips(x, y)

    def copy(a, k, block, to, src=None):
        return pltpu.make_async_remote_copy(
            src_ref=view(a, block) if src is None else src, dst_ref=view(a, block),
            send_sem=send_sems.at[a, k], recv_sem=recv_sems.at[a, k], device_id=to, device_id_type=MESH)

    def first_copies():
        cps = []
        for a in range(n):
            cps.append(copy(a, 0, me, sibling, src=own_src(a)))
            cps += [copy(a, 1 + j, me, (*chip, c), src=own_src(a)) for j, chip in enumerate(chips)]
        return cps

    def passed_copies():
        return [copy(a, 4 + j, (*chip, c), sibling) for j, chip in enumerate(chips) for a in range(n)]

    def first():
        for cp in first_copies():
            cp.start()

    def forward():
        for j, chip in enumerate(chips):
            for a in range(n):
                copy(a, 1 + j, (*chip, c), me).wait_recv()
                copy(a, 4 + j, (*chip, c), sibling).start()

    def finish():
        for a in range(n):
            copy(a, 0, sibling, me).wait_recv()
            for j, chip in enumerate(chips):
                copy(a, 4 + j, (*chip, 1 - c), me).wait_recv()
        for cp in first_copies() + passed_copies():
            cp.wait_send()

    return first, forward, finish


def _gather_comm(arrays, forward_step):
    n = len(arrays)

    def steps(cout, sems):
        view = lambda a, dev: cout[a].at[_device_index(dev)]
        return _gather_steps(n, view, lambda a: view(a, _coords()), sems[0], sems[1])

    hooks = [(0, True, lambda cin, cout, sems: steps(cout, sems)[0]()),
             (forward_step, True, lambda cin, cout, sems: steps(cout, sems)[1]()),
             (-1, False, lambda cin, cout, sems: steps(cout, sems)[2]())]
    return _Comm(list(arrays), [jax.ShapeDtypeStruct(a.shape, a.dtype) for a in arrays], {i: i for i in range(n)},
                 [pltpu.SemaphoreType.DMA((n, 7)), pltpu.SemaphoreType.DMA((n, 7))], hooks)


def _scatter_comm(grads):
    n = len(grads)

    def copies(cin, cout, sems):
        send_sems, recv_sems, local_sems = sems
        x, y, c = _coords()
        me = _device_index((x, y, c))
        mine = [pltpu.make_async_copy(cin[a].at[me], cout[a].at[me], local_sems.at[a]) for a in range(n)]
        sends, recvs = [], []
        for a in range(n):
            for mask in range(1, NDEV):
                peer = (1 - x if mask & 4 else x, 1 - y if mask & 2 else y, 1 - c if mask & 1 else c)
                p = _device_index(peer)
                kw = dict(send_sem=send_sems.at[a, mask - 1], recv_sem=recv_sems.at[a, mask - 1],
                          device_id=peer, device_id_type=MESH)
                sends.append(pltpu.make_async_remote_copy(src_ref=cin[a].at[p], dst_ref=cout[a].at[me], **kw))
                recvs.append(pltpu.make_async_remote_copy(src_ref=cin[a].at[p], dst_ref=cout[a].at[p], **kw))
        return mine, sends, recvs

    def start(cin, cout, sems):
        mine, sends, _ = copies(cin, cout, sems)
        for cp in mine + sends:
            cp.start()

    def finish(cin, cout, sems):
        mine, sends, recvs = copies(cin, cout, sems)
        for cp in recvs:
            cp.wait_recv()
        for cp in sends:
            cp.wait_send()
        for cp in mine:
            cp.wait()

    return _Comm(list(grads), [jax.ShapeDtypeStruct(g.shape, g.dtype) for g in grads], {},
                 [pltpu.SemaphoreType.DMA((n, NDEV - 1)), pltpu.SemaphoreType.DMA((n, NDEV - 1)),
                  pltpu.SemaphoreType.DMA((n,))],
                 [(0, True, start), (-1, False, finish)])


NOW_ITEMS = (0, 5, 6)
POOL_ITEM = 5


def _cast_gather_first(shards):
    n = len(shards)
    dtypes = [BF16] * 6 + [F32]
    out_shapes = [(NDEV,) + s.shape for s in shards]
    g, rows, cols = shards[POOL_ITEM].shape
    out_shapes[POOL_ITEM] = (g, rows * NDEV, cols)

    def body(*refs):
        ins, outs, stage = refs[:n], refs[n:2 * n], refs[2 * n:3 * n]
        send_sems, recv_sems, local_sems = refs[3 * n:]

        def view(a, dev):
            i = _device_index(dev)
            if a == POOL_ITEM:
                return outs[a].at[:, pl.ds(i * rows, rows), :]
            return outs[a].at[i]

        for a in range(n):
            stage[a][...] = ins[a][...].astype(dtypes[a])
        mine = [pltpu.make_async_copy(stage[a], view(a, _coords()), local_sems.at[a]) for a in range(n)]
        for cp in mine:
            cp.start()
        first, forward, finish = _gather_steps(
            len(NOW_ITEMS), lambda k, dev: view(NOW_ITEMS[k], dev), lambda k: stage[NOW_ITEMS[k]], send_sems, recv_sems)
        first()
        forward()
        finish()
        for cp in mine:
            cp.wait()

    return pl.pallas_call(
        body, name="cast_gather_first",
        in_specs=[VMEM_SPEC] * n, out_specs=[HBM_SPEC] * n,
        out_shape=[jax.ShapeDtypeStruct(s, d) for s, d in zip(out_shapes, dtypes)],
        scratch_shapes=[pltpu.VMEM(s.shape, d) for s, d in zip(shards, dtypes)]
        + [pltpu.SemaphoreType.DMA((len(NOW_ITEMS), 7)), pltpu.SemaphoreType.DMA((len(NOW_ITEMS), 7)),
           pltpu.SemaphoreType.DMA((n,))],
        compiler_params=pltpu.CompilerParams(vmem_limit_bytes=VMEM_LIMIT),
    )(*shards)


def _inproj(x, g1, win, comm, tm=1024):
    T, D = x.shape
    nb, _, bn = win.shape

    def body(x_ref, g_ref, w_ref, proj_ref, h_ref):
        @pl.when(pl.program_id(1) == 0)
        def _():
            xv = x_ref[...]
            h_ref[...] = (xv * _rsq(xv) * g_ref[...]).astype(BF16)

        proj_ref[...] = jnp.dot(h_ref[...], w_ref[0], preferred_element_type=F32).astype(BF16)

    return _host_call(
        body, name="inproj", grid=(T // tm, nb), comm=comm, args=(x, g1, win),
        in_specs=[pl.BlockSpec((tm, D), lambda i, j: (i, 0)),
                  pl.BlockSpec((1, D), lambda i, j: (0, 0)),
                  pl.BlockSpec((1, D, bn), lambda i, j: (j, 0, 0))],
        out_specs=[pl.BlockSpec((tm, bn), lambda i, j: (i, j)),
                   pl.BlockSpec((tm, D), lambda i, j: (i, 0))],
        out_shape=[jax.ShapeDtypeStruct((T, nb * bn), BF16), jax.ShapeDtypeStruct((T, D), BF16)])


def _mixer_fwd(proj, x, cw, pw, ps, wout, g2, g3, comm, tm=256):
    T, D = x.shape
    P = proj.shape[1]

    def body(proj_ref, x_ref, cw_ref, pw_ref, ps_ref, wout_ref, g2_ref, g3_ref,
             x1_ref, hf_ref, mixed_ref, mo_ref, cu_carry, v_carry):
        i = pl.program_id(0)

        @pl.when(i == 0)
        def _():
            cu_carry[...] = jnp.zeros_like(cu_carry)
            v_carry[...] = jnp.zeros_like(v_carry)

        for h in range(CONV_HEADS):
            lo = h * HEAD_DIM
            gate_b = proj_ref[:, lo:lo + HEAD_DIM].astype(F32)
            cu = proj_ref[:, CONV_WIDTH + lo:CONV_WIDTH + lo + HEAD_DIM].astype(F32) * \
                proj_ref[:, 2 * CONV_WIDTH + lo:2 * CONV_WIDTH + lo + HEAD_DIM].astype(F32)
            ext = jnp.concatenate([cu_carry[:, lo:lo + HEAD_DIM], cu], axis=0)
            c1 = pltpu.roll(ext, 1, 0)[HALO:]
            c2 = pltpu.roll(ext, 2, 0)[HALO:]
            ya = gate_b * (cw_ref[h, 2:3, :] * cu + cw_ref[h, 1:2, :] * c1 + cw_ref[h, 0:1, :] * c2)
            mixed_ref[:, lo:lo + HEAD_DIM] = (ya * _rsq(ya)).astype(BF16)
            cu_carry[:, lo:lo + HEAD_DIM] = cu[tm - HALO:]

        for gi, w in enumerate(POOL_WINDOWS):
            lo = gi * POOL_GROUP_DIM
            v = proj_ref[:, 3 * CONV_WIDTH + lo:3 * CONV_WIDTH + lo + POOL_GROUP_DIM].astype(F32)
            ext = jnp.concatenate([v_carry[:, lo:lo + POOL_GROUP_DIM], v], axis=0)
            pooled = _window_sum(ext, w, True)[HALO:] * _inv_count(i * tm, tm, w) - v
            y = jnp.dot(pooled.astype(BF16), pw_ref[gi], preferred_element_type=F32)
            yb = y * _rsq(y) * ps_ref[:, lo:lo + POOL_GROUP_DIM]
            mixed_ref[:, CONV_WIDTH + lo:CONV_WIDTH + lo + POOL_GROUP_DIM] = yb.astype(BF16)
            v_carry[:, lo:lo + POOL_GROUP_DIM] = v[tm - HALO:]

        mo = jnp.dot(mixed_ref[...], wout_ref[...], preferred_element_type=F32)
        mo_ref[...] = mo
        x1 = x_ref[...] + mo * _rsq(mo) * g2_ref[...]
        x1_ref[...] = x1
        hf_ref[...] = (x1 * _rsq(x1) * g3_ref[...]).astype(BF16)

    row = lambda n: pl.BlockSpec((tm, n), lambda i: (i, 0))
    return _host_call(
        body, name="mixer_fwd", grid=(T // tm,), comm=comm, args=(proj, x, cw, pw, ps, wout, g2, g3),
        in_specs=[row(P), row(D), _whole(cw.shape), _whole(pw.shape), _whole(ps.shape),
                  _whole(wout.shape), _whole(g2.shape), _whole(g3.shape)],
        out_specs=[row(D), row(D), row(D), row(D)],
        out_shape=[jax.ShapeDtypeStruct((T, D), F32), jax.ShapeDtypeStruct((T, D), BF16),
                   jax.ShapeDtypeStruct((T, D), BF16), jax.ShapeDtypeStruct((T, D), F32)],
        scratch_shapes=[pltpu.VMEM((HALO, CONV_WIDTH), F32), pltpu.VMEM((HALO, CONV_WIDTH), F32)])


def _ffn_up(hf, wg, wu, comm, tm=1024):
    T, D = hf.shape
    nb, _, bf = wg.shape

    def body(hf_ref, wg_ref, wu_ref, g_ref, u_ref, a_ref):
        hv = hf_ref[...]
        g = jnp.dot(hv, wg_ref[0], preferred_element_type=F32)
        u = jnp.dot(hv, wu_ref[0], preferred_element_type=F32)
        g_ref[0] = g.astype(BF16)
        u_ref[0] = u.astype(BF16)
        a_ref[0] = (g * jax.nn.sigmoid(g) * u).astype(BF16)

    wspec = pl.BlockSpec((1, D, bf), lambda i, j: (j, 0, 0))
    ospec = pl.BlockSpec((1, tm, bf), lambda i, j: (j, i, 0))
    oshape = jax.ShapeDtypeStruct((nb, T, bf), BF16)
    return _host_call(
        body, name="ffn_up", grid=(T // tm, nb), comm=comm, args=(hf, wg, wu),
        in_specs=[pl.BlockSpec((tm, D), lambda i, j: (i, 0)), wspec, wspec],
        out_specs=[ospec, ospec, ospec], out_shape=[oshape, oshape, oshape])


def _ffn_down_loss(a, wd, x1, tgt, g4, tm=512, kb=2):
    nblk, T, bf = a.shape
    nb = nblk // kb
    D = x1.shape[1]
    nt = T // tm

    def body(a_ref, wd_ref, x1_hbm, tgt_hbm, g4_ref, dy_ref, dff_ref, loss_ref, dg4_ref,
             acc_ref, lacc_ref, x1_ref, tgt_ref, late_sems):
        i, j = pl.program_id(0), pl.program_id(1)
        late = [_late_rows(x1_hbm, x1_ref, late_sems.at[0], i, tm), _late_rows(tgt_hbm, tgt_ref, late_sems.at[1], i, tm)]

        @pl.when((i == 0) & (j == 0))
        def _():
            lacc_ref[...] = jnp.zeros_like(lacc_ref)
            dg4_ref[...] = jnp.zeros_like(dg4_ref)

        @pl.when(j == 0)
        def _():
            for cp in late:
                cp.start()
            acc_ref[...] = jnp.zeros_like(acc_ref)

        part = jnp.dot(a_ref[0], wd_ref[0], preferred_element_type=F32)
        for k in range(1, kb):
            part = part + jnp.dot(a_ref[k], wd_ref[k], preferred_element_type=F32)
        acc_ref[...] += part

        @pl.when(j == nb - 1)
        def _():
            for cp in late:
                cp.wait()
            ff = acc_ref[...]
            r = _rsq(ff)
            n = ff * r
            g4v = g4_ref[...]
            e = x1_ref[...] + n * g4v - tgt_ref[...]
            lacc_ref[...] += jnp.sum(e * e, axis=0, keepdims=True)
            dy = e * (1.0 / D)
            dy_ref[...] = dy
            dg4_ref[...] += jnp.sum(dy * n, axis=0, keepdims=True)
            dff_ref[...] = _norm_bwd(dy * g4v, n, r).astype(BF16)

        @pl.when((i == nt - 1) & (j == nb - 1))
        def _():
            loss_ref[...] = jnp.full(loss_ref.shape, (0.5 / D) * jnp.sum(lacc_ref[...]), F32)

    row = pl.BlockSpec((tm, D), lambda i, j: (i, 0))
    vec = pl.BlockSpec((1, D), lambda i, j: (0, 0))
    return _host_call(
        body, name="ffn_down_loss", grid=(nt, nb), args=(a, wd, x1, tgt, g4),
        in_specs=[pl.BlockSpec((kb, tm, bf), lambda i, j: (j, i, 0)),
                  pl.BlockSpec((kb, bf, D), lambda i, j: (j, 0, 0)), HBM_SPEC, HBM_SPEC, vec],
        out_specs=[row, row, pl.BlockSpec((1, 128), lambda i, j: (0, 0)), vec],
        out_shape=[jax.ShapeDtypeStruct((T, D), F32), jax.ShapeDtypeStruct((T, D), BF16),
                   jax.ShapeDtypeStruct((1, 128), F32), jax.ShapeDtypeStruct((1, D), F32)],
        scratch_shapes=[pltpu.VMEM((tm, D), F32), pltpu.VMEM((1, D), F32), pltpu.VMEM((tm, D), F32),
                        pltpu.VMEM((tm, D), F32), pltpu.SemaphoreType.DMA((2,))])[0]


def _ffn_bwd_act(dff, wd, g, u, comm, tm=1024):
    T, D = dff.shape
    nb, bf, _ = wd.shape

    def body(dff_ref, wd_ref, g_ref, u_ref, dg_ref, du_ref):
        da = lax.dot_general(dff_ref[...], wd_ref[0], NT_DIMS, preferred_element_type=F32)
        gv = g_ref[0].astype(F32)
        s = jax.nn.sigmoid(gv)
        du_ref[0] = (da * (gv * s)).astype(BF16)
        dg_ref[0] = (da * u_ref[0].astype(F32) * (s * (1.0 + gv * (1.0 - s)))).astype(BF16)

    blk = pl.BlockSpec((1, tm, bf), lambda i, j: (j, i, 0))
    oshape = jax.ShapeDtypeStruct((nb, T, bf), BF16)
    return _host_call(
        body, name="ffn_bwd_act", grid=(T // tm, nb), comm=comm, args=(dff, wd, g, u),
        in_specs=[pl.BlockSpec((tm, D), lambda i, j: (i, 0)),
                  pl.BlockSpec((1, bf, D), lambda i, j: (j, 0, 0)), blk, blk],
        out_specs=[blk, blk], out_shape=[oshape, oshape])


def _ffn_bwd_in(dg, du, wg, wu, x1, dy, g3, comm, tm=512):
    nb, T, bf = dg.shape
    D = x1.shape[1]

    def body(dg_ref, du_ref, wg_ref, wu_ref, x1_ref, dy_ref, g3_ref, dx1_ref, dg3_ref, acc_ref):
        i, j = pl.program_id(0), pl.program_id(1)

        @pl.when((i == 0) & (j == 0))
        def _():
            dg3_ref[...] = jnp.zeros_like(dg3_ref)

        @pl.when(j == 0)
        def _():
            acc_ref[...] = jnp.zeros_like(acc_ref)

        acc_ref[...] += (lax.dot_general(dg_ref[0], wg_ref[0], NT_DIMS, preferred_element_type=F32)
                         + lax.dot_general(du_ref[0], wu_ref[0], NT_DIMS, preferred_element_type=F32))

        @pl.when(j == nb - 1)
        def _():
            dhf = acc_ref[...]
            x1v = x1_ref[...]
            r = _rsq(x1v)
            n = x1v * r
            dg3_ref[...] += jnp.sum(dhf * n, axis=0, keepdims=True)
            dx1_ref[...] = dy_ref[...] + _norm_bwd(dhf * g3_ref[...], n, r)

    row = pl.BlockSpec((tm, D), lambda i, j: (i, 0))
    vec = pl.BlockSpec((1, D), lambda i, j: (0, 0))
    ablk = pl.BlockSpec((1, tm, bf), lambda i, j: (j, i, 0))
    wblk = pl.BlockSpec((1, D, bf), lambda i, j: (j, 0, 0))
    return _host_call(
        body, name="ffn_bwd_in", grid=(T // tm, nb), comm=comm, args=(dg, du, wg, wu, x1, dy, g3),
        in_specs=[ablk, ablk, wblk, wblk, row, row, vec],
        out_specs=[row, vec],
        out_shape=[jax.ShapeDtypeStruct((T, D), F32), jax.ShapeDtypeStruct((1, D), F32)],
        scratch_shapes=[pltpu.VMEM((tm, D), F32)])


def _wgrad(name, lhs, rhs, lhs_spec, rhs_spec, n_rhs, M, N, nb, nk, comm=None):
    def body(*refs):
        l_ref, r_refs = refs[0], refs[1:1 + n_rhs]
        o_refs, acc_refs = refs[1 + n_rhs:1 + 2 * n_rhs], refs[1 + 2 * n_rhs:]
        k = pl.program_id(1)
        tile = lambda ref: ref[0] if len(ref.shape) == 3 else ref[...]

        @pl.when(k == 0)
        def _():
            for acc_ref in acc_refs:
                acc_ref[...] = jnp.zeros_like(acc_ref)

        for r_ref, acc_ref in zip(r_refs, acc_refs):
            acc_ref[...] += lax.dot_general(tile(l_ref), tile(r_ref), TN_DIMS, preferred_element_type=F32)

        @pl.when(k == nk - 1)
        def _():
            for o_ref, acc_ref in zip(o_refs, acc_refs):
                o_ref[0] = acc_ref[...].astype(BF16)

    oblk = pl.BlockSpec((1, M, N), lambda j, k: (j, 0, 0))
    oshape = jax.ShapeDtypeStruct((nb, M, N), BF16)
    return _host_call(
        body, name=name, grid=(nb, nk), comm=comm, args=(lhs, *rhs),
        in_specs=[lhs_spec] + [rhs_spec] * n_rhs, out_specs=[oblk] * n_rhs, out_shape=[oshape] * n_rhs,
        scratch_shapes=[pltpu.VMEM((M, N), F32)] * n_rhs)


def _wgrad_down(a, dff, tk=1024):
    nb, T, M = a.shape
    N = dff.shape[1]
    return _wgrad("wgrad_down", a, [dff], pl.BlockSpec((1, tk, M), lambda j, k: (j, k, 0)),
                  pl.BlockSpec((tk, N), lambda j, k: (k, 0)), 1, M, N, nb, T // tk)[0][0]


def _wgrad_gate_up(hf, dg, du, tk=1024):
    T, M = hf.shape
    nb, _, N = dg.shape
    return _wgrad("wgrad_gate_up", hf, [dg, du], pl.BlockSpec((tk, M), lambda j, k: (k, 0)),
                  pl.BlockSpec((1, tk, N), lambda j, k: (j, k, 0)), 2, M, N, nb, T // tk)[0]


def _wgrad_out(mixed, dmo, nb=NDEV, tk=1024):
    T, D = mixed.shape
    return _wgrad("wgrad_out", mixed, [dmo], pl.BlockSpec((tk, D // nb), lambda j, k: (k, j)),
                  pl.BlockSpec((tk, D), lambda j, k: (k, 0)), 1, D // nb, D, nb, T // tk)[0][0]


def _wgrad_in(h, dproj, comm, nb=NDEV, tk=1024):
    T, D = h.shape
    bn = dproj.shape[1] // nb
    res, cres = _wgrad("wgrad_in", h, [dproj], pl.BlockSpec((tk, D), lambda j, k: (k, 0)),
                       pl.BlockSpec((tk, bn), lambda j, k: (k, j)), 1, D, bn, nb, T // tk, comm=comm)
    return res[0], cres


def _mixer_bwd(dx1, mo, proj, cw, pw, ps, wout, g2, tm=256):
    T, D = dx1.shape
    P = proj.shape[1]
    nt = T // tm
    n_ext = tm + HALO
    hb = tm // HALO

    def body(dx1_ref, mo_ref, proj_ref, hc_ref, hu_ref, hv_ref, cw_ref, pw_ref, ps_ref, wout_ref, g2_ref,
             dmo_ref, dproj_ref, dg2_ref, dcw_ref, dps_ref, dpw_ref, dmix_ref, dconv_carry, q_carry):
        i = pl.program_id(0)
        tile = nt - 1 - i

        @pl.when(i == 0)
        def _():
            dconv_carry[...] = jnp.zeros_like(dconv_carry)
            q_carry[...] = jnp.zeros_like(q_carry)
            dg2_ref[...] = jnp.zeros_like(dg2_ref)
            dcw_ref[...] = jnp.zeros_like(dcw_ref)
            dps_ref[...] = jnp.zeros_like(dps_ref)
            dpw_ref[...] = jnp.zeros_like(dpw_ref)

        mov = mo_ref[...]
        r2 = _rsq(mov)
        n2 = mov * r2
        dx1v = dx1_ref[...]
        dg2_ref[...] += jnp.sum(dx1v * n2, axis=0, keepdims=True)
        dmo = _norm_bwd(dx1v * g2_ref[...], n2, r2).astype(BF16)
        dmo_ref[...] = dmo
        dmix_ref[...] = lax.dot_general(dmo, wout_ref[...], NT_DIMS, preferred_element_type=F32)

        has_prev = (tile > 0).astype(F32)

        for h in range(CONV_HEADS):
            lo = h * HEAD_DIM
            sl = slice(lo, lo + HEAD_DIM)
            gate_b = proj_ref[:, lo:lo + HEAD_DIM].astype(F32)
            gate_c = proj_ref[:, CONV_WIDTH + lo:CONV_WIDTH + lo + HEAD_DIM].astype(F32)
            uu = proj_ref[:, 2 * CONV_WIDTH + lo:2 * CONV_WIDTH + lo + HEAD_DIM].astype(F32)
            cu = gate_c * uu
            ext = jnp.concatenate([hc_ref[:, sl].astype(F32) * hu_ref[:, sl].astype(F32) * has_prev, cu], axis=0)
            c1 = pltpu.roll(ext, 1, 0)[HALO:]
            c2 = pltpu.roll(ext, 2, 0)[HALO:]
            w0, w1, w2 = cw_ref[h, 0:1, :], cw_ref[h, 1:2, :], cw_ref[h, 2:3, :]
            conv = w2 * cu + w1 * c1 + w0 * c2
            ya = gate_b * conv
            ra = _rsq(ya)
            dya = _norm_bwd(dmix_ref[:, sl], ya * ra, ra)
            dconv = dya * gate_b
            dcw_ref[h, 0:1, :] += jnp.sum(dconv * c2, axis=0, keepdims=True)
            dcw_ref[h, 1:2, :] += jnp.sum(dconv * c1, axis=0, keepdims=True)
            dcw_ref[h, 2:3, :] += jnp.sum(dconv * cu, axis=0, keepdims=True)
            extd = jnp.concatenate([dconv, dconv_carry[:, sl]], axis=0)
            d1 = pltpu.roll(extd, n_ext - 1, 0)[:tm]
            d2 = pltpu.roll(extd, n_ext - 2, 0)[:tm]
            dcu = w2 * dconv + w1 * d1 + w0 * d2
            dconv_carry[:, sl] = dconv[:HALO]
            dproj_ref[:, lo:lo + HEAD_DIM] = (dya * conv).astype(BF16)
            dproj_ref[:, CONV_WIDTH + lo:CONV_WIDTH + lo + HEAD_DIM] = (dcu * uu).astype(BF16)
            dproj_ref[:, 2 * CONV_WIDTH + lo:2 * CONV_WIDTH + lo + HEAD_DIM] = (dcu * gate_c).astype(BF16)

        for gi, w in enumerate(POOL_WINDOWS):
            lo = gi * POOL_GROUP_DIM
            sl = slice(lo, lo + POOL_GROUP_DIM)
            v = proj_ref[:, 3 * CONV_WIDTH + lo:3 * CONV_WIDTH + lo + POOL_GROUP_DIM].astype(F32)
            inv = _inv_count(tile * tm, tm, w)
            ext = jnp.concatenate([hv_ref[:, sl].astype(F32) * has_prev, v], axis=0)
            pooled = (_window_sum(ext, w, True)[HALO:] * inv - v).astype(BF16)
            y = jnp.dot(pooled, pw_ref[gi], preferred_element_type=F32)
            rp = _rsq(y)
            nb_ = y * rp
            dyb = dmix_ref[:, CONV_WIDTH + lo:CONV_WIDTH + lo + POOL_GROUP_DIM]
            dps_ref[:, sl] += jnp.sum(dyb * nb_, axis=0, keepdims=True)
            dy = _norm_bwd(dyb * ps_ref[:, sl], nb_, rp).astype(BF16)
            dpw_ref[gi] += lax.dot_general(pooled, dy, TN_DIMS, preferred_element_type=F32)
            dpooled = lax.dot_general(dy, pw_ref[gi], NT_DIMS, preferred_element_type=F32)
            q = dpooled * inv
            extq = jnp.concatenate([q, q_carry[:, sl]], axis=0)
            dv = _window_sum(extq, w, False)[:tm] - dpooled
            q_carry[:, sl] = q[:HALO]
            dproj_ref[:, 3 * CONV_WIDTH + lo:3 * CONV_WIDTH + lo + POOL_GROUP_DIM] = dv.astype(BF16)

    rev = lambda n: pl.BlockSpec((tm, n), lambda i: (nt - 1 - i, 0))

    def halo(col):
        return pl.BlockSpec((HALO, CONV_WIDTH), lambda i: (jnp.maximum((nt - 1 - i) * hb - 1, 0), col))

    return _host_call(
        body, name="mixer_bwd", grid=(nt,), args=(dx1, mo, proj, proj, proj, proj, cw, pw, ps, wout, g2),
        in_specs=[rev(D), rev(D), rev(P), halo(1), halo(2), halo(3), _whole(cw.shape), _whole(pw.shape),
                  _whole(ps.shape), _whole(wout.shape), _whole(g2.shape)],
        out_specs=[rev(D), rev(P), pl.BlockSpec((1, D), lambda i: (0, 0)),
                   pl.BlockSpec(cw.shape, lambda i: (0, 0, 0)), pl.BlockSpec(ps.shape, lambda i: (0, 0)),
                   pl.BlockSpec(pw.shape, lambda i: (0, 0, 0))],
        out_shape=[jax.ShapeDtypeStruct((T, D), BF16), jax.ShapeDtypeStruct((T, P), BF16),
                   jax.ShapeDtypeStruct((1, D), F32), jax.ShapeDtypeStruct(cw.shape, F32),
                   jax.ShapeDtypeStruct(ps.shape, F32), jax.ShapeDtypeStruct(pw.shape, F32)],
        scratch_shapes=[pltpu.VMEM((tm, D), F32), pltpu.VMEM((HALO, CONV_WIDTH), F32),
                        pltpu.VMEM((HALO, CONV_WIDTH), F32)])[0]


def _inproj_bwd(dproj, win, x, dx1, g1, comm, tm=512, kb=2):
    T, D = x.shape
    nblk, _, bn = win.shape
    nb = nblk // kb

    def body(dp_ref, w_ref, x_ref, dx1_ref, g1_ref, gx_ref, dg1_ref, acc_ref):
        i, j = pl.program_id(0), pl.program_id(1)

        @pl.when((i == 0) & (j == 0))
        def _():
            dg1_ref[...] = jnp.zeros_like(dg1_ref)

        @pl.when(j == 0)
        def _():
            acc_ref[...] = jnp.zeros_like(acc_ref)

        part = lax.dot_general(dp_ref[:, 0:bn], w_ref[0], NT_DIMS, preferred_element_type=F32)
        for k in range(1, kb):
            part = part + lax.dot_general(dp_ref[:, k * bn:(k + 1) * bn], w_ref[k], NT_DIMS,
                                          preferred_element_type=F32)
        acc_ref[...] += part

        @pl.when(j == nb - 1)
        def _():
            dh = acc_ref[...]
            xv = x_ref[...]
            r = _rsq(xv)
            n = xv * r
            dg1_ref[...] += jnp.sum(dh * n, axis=0, keepdims=True)
            gx_ref[...] = dx1_ref[...] + _norm_bwd(dh * g1_ref[...], n, r)

    row = pl.BlockSpec((tm, D), lambda i, j: (i, 0))
    vec = pl.BlockSpec((1, D), lambda i, j: (0, 0))
    return _host_call(
        body, name="inproj_bwd", grid=(T // tm, nb), comm=comm, args=(dproj, win, x, dx1, g1),
        in_specs=[pl.BlockSpec((tm, kb * bn), lambda i, j: (i, j)),
                  pl.BlockSpec((kb, D, bn), lambda i, j: (j, 0, 0)), row, row, vec],
        out_specs=[row, vec],
        out_shape=[jax.ShapeDtypeStruct((T, D), F32), jax.ShapeDtypeStruct((1, D), F32)],
        scratch_shapes=[pltpu.VMEM((tm, D), F32)])


def _adamw(w, g, m, v):
    m = ADAM_B1 * m + (1.0 - ADAM_B1) * g
    v = ADAM_B2 * v + (1.0 - ADAM_B2) * jnp.square(g)
    m_hat = m / (1.0 - ADAM_B1 ** ADAM_STEP)
    v_hat = v / (1.0 - ADAM_B2 ** ADAM_STEP)
    delta = -ADAM_LR * (m_hat / (jnp.sqrt(v_hat) + ADAM_EPS) + ADAM_WD * w)
    return delta, m, v


def _sum_adamw(parts, w, m, v, name, tr):
    r, cd = w.shape

    def body(p_ref, w_ref, m_ref, v_ref, g_ref, d_ref, mo_ref, vo_ref):
        g = p_ref[0].astype(F32)
        for k in range(1, NDEV):
            g = g + p_ref[k].astype(F32)
        g_ref[...] = g
        d_ref[...], mo_ref[...], vo_ref[...] = _adamw(w_ref[...], g, m_ref[...], v_ref[...])

    blk = pl.BlockSpec((tr, cd), lambda i: (i, 0))
    shp = jax.ShapeDtypeStruct((r, cd), F32)
    return pl.pallas_call(
        body, name=name, grid=(r // tr,),
        in_specs=[pl.BlockSpec((NDEV, tr, cd), lambda i: (0, i, 0)), blk, blk, blk],
        out_specs=[blk] * 4, out_shape=[shp] * 4,
        compiler_params=_params("arbitrary"),
    )(parts, w, m, v)


def _small_reduce_adamw(vec_grads, dps, dcw, dpw, vec_state, ps_state, cw_state, pw_state):
    D = vec_grads[0].shape[1]
    pw_rows = pw_state[0].shape[1]
    states = list(vec_state) + [ps_state, cw_state, pw_state]
    n_in = 4 + 3 + 3 * len(states)
    n_out = 4 * len(states)

    def body(*refs):
        dg = refs[0:4]
        dps_ref, dcw_ref, dpw_ref = refs[4:7]
        st = refs[7:n_in]
        outs = refs[n_in:n_in + n_out]
        pack, gat, cbuf, pbuf, send_sems, recv_sems, local_sems = refs[n_in + n_out:]
        x, y, c = _coords()
        me = _device_index((x, y, c))

        pack[...] = jnp.zeros_like(pack)
        for k in range(4):
            pack[k:k + 1, :] = dg[k][...]
        pack[4:5, 0:dps_ref.shape[1]] = dps_ref[...]

        def pw_slice(i):
            return dpw_ref.at[:, pl.ds(i * pw_rows, pw_rows), :]

        mine = [pltpu.make_async_copy(pack, gat.at[me], local_sems.at[0]),
                pltpu.make_async_copy(dcw_ref.at[me], cbuf.at[me], local_sems.at[1]),
                pltpu.make_async_copy(pw_slice(me), pbuf.at[me], local_sems.at[2])]
        for cp in mine:
            cp.start()
        sends, recvs = [], []
        for mask in range(1, NDEV):
            peer = (1 - x if mask & 4 else x, 1 - y if mask & 2 else y, 1 - c if mask & 1 else c)
            p = _device_index(peer)
            for k, (src, buf) in enumerate(((pack, gat), (dcw_ref.at[p], cbuf), (pw_slice(p), pbuf))):
                kw = dict(send_sem=send_sems.at[mask, k], recv_sem=recv_sems.at[mask, k],
                          device_id=peer, device_id_type=MESH)
                sends.append(pltpu.make_async_remote_copy(src_ref=src, dst_ref=buf.at[me], **kw))
                recvs.append(pltpu.make_async_remote_copy(src_ref=src, dst_ref=buf.at[p], **kw))
                sends[-1].start()
        for cp in recvs:
            cp.wait_recv()
        for cp in sends:
            cp.wait_send()
        for cp in mine:
            cp.wait()

        def slot_sum(buf):
            s = buf[0]
            for k in range(1, NDEV):
                s = s + buf[k]
            return s

        vec = slot_sum(gat)
        grads = [vec[k:k + 1, :] for k in range(4)] + [vec[4:5, 0:dps_ref.shape[1]], slot_sum(cbuf), slot_sum(pbuf)]
        for k, g in enumerate(grads):
            w_ref, m_ref, v_ref = st[3 * k:3 * k + 3]
            outs[4 * k][...] = g
            outs[4 * k + 1][...], outs[4 * k + 2][...], outs[4 * k + 3][...] = _adamw(
                w_ref[...], g, m_ref[...], v_ref[...])

    flat_state = [a for s in states for a in s]
    out_shape = [jax.ShapeDtypeStruct(s[0].shape, F32) for s in states for _ in range(4)]
    return pl.pallas_call(
        body, name="small_reduce_adamw",
        in_specs=[VMEM_SPEC] * n_in, out_specs=[VMEM_SPEC] * n_out, out_shape=out_shape,
        scratch_shapes=[pltpu.VMEM((NDEV, D), F32), pltpu.VMEM((NDEV, NDEV, D), F32),
                        pltpu.VMEM((NDEV,) + cw_state[0].shape, F32), pltpu.VMEM((NDEV,) + pw_state[0].shape, F32),
                        pltpu.SemaphoreType.DMA((NDEV, 3)), pltpu.SemaphoreType.DMA((NDEV, 3)),
                        pltpu.SemaphoreType.DMA((3,))],
        compiler_params=pltpu.CompilerParams(vmem_limit_bytes=VMEM_LIMIT),
    )(*vec_grads, dps, dcw, dpw, *flat_state)


ROW_TILE = dict(w_in=512, w_gate=256, w_up=256, w_down=176, w_out=128)
FORWARD_STEP = dict(inproj=56, mixer_fwd=26, ffn_up=32)


def kernel(x, ln_mix_pre, w_in, conv_w, pool_w, pool_scale, w_out, ln_mix_post, ln_ffn_pre, w_gate, w_up, w_down, ln_ffn_post, loss_target, m_ln_mix_pre, m_w_in, m_conv_w, m_pool_w, m_pool_scale, m_w_out, m_ln_mix_post, m_ln_ffn_pre, m_w_gate, m_w_up, m_w_down, m_ln_ffn_post, v_ln_mix_pre, v_w_in, v_conv_w, v_pool_w, v_pool_scale, v_w_out, v_ln_mix_post, v_ln_ffn_pre, v_w_gate, v_w_up, v_w_down, v_ln_ffn_post):
    D = x.shape[2]
    xs, tgt = x[0], loss_target[0]
    win, wg, wu, wd, wout, pw, cw = _cast_gather_first(
        [w_in[0], w_gate[0], w_up[0], w_down[0], w_out[0], pool_w[0], conv_w[0]])

    (proj, h), (wout, wg) = _inproj(xs, ln_mix_pre, win, _gather_comm([wout, wg], FORWARD_STEP["inproj"]))
    wout2 = wout.reshape(D, D)
    (x1, hf, mixed, mo), (wu,) = _mixer_fwd(proj, xs, cw, pw, pool_scale, wout2, ln_mix_post, ln_ffn_pre,
                                            _gather_comm([wu], FORWARD_STEP["mixer_fwd"]))
    (g, u, a), (wd,) = _ffn_up(hf, wg, wu, _gather_comm([wd], FORWARD_STEP["ffn_up"]))
    dy, dff, loss, dg4 = _ffn_down_loss(a, wd, x1, tgt, ln_ffn_post)

    dwd = _wgrad_down(a, dff)
    (dg, du), (dwd_parts,) = _ffn_bwd_act(dff, wd, g, u, _scatter_comm([dwd]))
    dwg, dwu = _wgrad_gate_up(hf, dg, du)
    (dx1, dg3), (dwg_parts, dwu_parts) = _ffn_bwd_in(dg, du, wg, wu, x1, dy, ln_ffn_pre, _scatter_comm([dwg, dwu]))
    dmo, dproj, dg2, dcw, dps, dpw = _mixer_bwd(dx1, mo, proj, cw, pw, pool_scale, wout2, ln_mix_post)
    dwout = _wgrad_out(mixed, dmo)
    dwin, (dwout_parts,) = _wgrad_in(h, dproj, _scatter_comm([dwout]))
    (gx, dg1), (dwin_parts,) = _inproj_bwd(dproj, win, xs, dx1, ln_mix_pre, _scatter_comm([dwin]))

    loss = lax.psum(loss[0, 0], ("x", "y", "c"))
    res = {}
    for k, parts, w, m, v in (("w_down", dwd_parts, w_down, m_w_down, v_w_down),
                              ("w_gate", dwg_parts, w_gate, m_w_gate, v_w_gate),
                              ("w_up", dwu_parts, w_up, m_w_up, v_w_up),
                              ("w_out", dwout_parts, w_out, m_w_out, v_w_out),
                              ("w_in", dwin_parts, w_in, m_w_in, v_w_in)):
        res[k] = [o.reshape(w.shape) for o in _sum_adamw(parts, w[0], m[0], v[0], "sum_adamw_" + k, ROW_TILE[k])]

    small = _small_reduce_adamw(
        [dg1, dg2, dg3, dg4], dps, dcw, dpw,
        [(ln_mix_pre, m_ln_mix_pre, v_ln_mix_pre), (ln_mix_post, m_ln_mix_post, v_ln_mix_post),
         (ln_ffn_pre, m_ln_ffn_pre, v_ln_ffn_pre), (ln_ffn_post, m_ln_ffn_post, v_ln_ffn_post)],
        (pool_scale, m_pool_scale, v_pool_scale), (conv_w[0], m_conv_w[0], v_conv_w[0]),
        (pool_w[0], m_pool_w[0], v_pool_w[0]))
    small_names = ["ln_mix_pre", "ln_mix_post", "ln_ffn_pre", "ln_ffn_post", "pool_scale", "conv_w", "pool_w"]
    shapes = dict(conv_w=conv_w.shape, pool_w=pool_w.shape)
    for i, k in enumerate(small_names):
        res[k] = [o.reshape(shapes[k]) if k in shapes else o for o in small[4 * i:4 * i + 4]]

    order = ["ln_mix_pre", "w_in", "conv_w", "pool_w", "pool_scale", "w_out", "ln_mix_post", "ln_ffn_pre",
             "w_gate", "w_up", "w_down", "ln_ffn_post"]
    return (loss, gx[None], *[res[k][0] for k in order], *[res[k][1] for k in order],
            *[res[k][2] for k in order], *[res[k][3] for k in order])
```

```python
import functools
from typing import Any, NamedTuple

import jax
import jax.numpy as jnp
from jax import lax
from jax.experimental import pallas as pl
from jax.experimental.pallas import tpu as pltpu

EPS = 1e-6
NDEV = 8
CONV_HEADS = 8
HEAD_DIM = 128
CONV_WIDTH = CONV_HEADS * HEAD_DIM
POOL_WINDOWS = (2, 4, 8, 16)
POOL_GROUP_DIM = 256
HALO = 16

ADAM_LR = 0.001
ADAM_B1 = 0.9
ADAM_B2 = 0.999
ADAM_EPS = 1e-08
ADAM_WD = 0.01
ADAM_STEP = 10

F32 = jnp.float32
BF16 = jnp.bfloat16
VMEM_LIMIT = 58 * 1024 * 1024
MESH = pl.DeviceIdType.MESH
HBM_SPEC = pl.BlockSpec(memory_space=pl.ANY)
VMEM_SPEC = pl.BlockSpec(memory_space=pltpu.VMEM)

NT_DIMS = (((1,), (1,)), ((), ()))
TN_DIMS = (((0,), (0,)), ((), ()))


def _params(*sem):
    return pltpu.CompilerParams(dimension_semantics=sem, vmem_limit_bytes=VMEM_LIMIT)


def _rsq(v):
    return lax.rsqrt(jnp.mean(v * v, axis=-1, keepdims=True) + EPS)


def _norm_bwd(dn, n, r):
    return r * (dn - n * jnp.mean(dn * n, axis=-1, keepdims=True))


def _whole(shape):
    nd = len(shape)
    return pl.BlockSpec(shape, lambda *_: (0,) * nd, pipeline_mode=pl.Buffered(1))


def _inv_count(t0, tm, w):
    t = t0 + lax.broadcasted_iota(jnp.int32, (tm, 1), 0)
    return 1.0 / jnp.minimum(t + 1, w).astype(F32)


def _late_rows(hbm_ref, buf, sem, i, tm):
    return pltpu.make_async_copy(hbm_ref.at[pl.ds(pl.multiple_of(i * tm, tm), tm), :], buf, sem)


def _window_sum(ext, w, back):
    n = ext.shape[0]
    s, shift = ext, 1
    while shift < w:
        s = s + pltpu.roll(s, shift if back else n - shift, 0)
        shift *= 2
    return s


class _Comm(NamedTuple):
    arrays: Any
    out_shape: Any
    aliases: Any
    scratch: Any
    hooks: Any


def _coords():
    return lax.axis_index("x"), lax.axis_index("y"), lax.axis_index("c")


def _other_chips(x, y):
    return [(1 - x, y), (x, 1 - y), (1 - x, 1 - y)]


def _device_index(dev):
    return 4 * dev[0] + 2 * dev[1] + dev[2]


def _host_call(body, *, name, grid, in_specs, out_specs, out_shape, args, scratch_shapes=(), comm=None):
    sem = ("arbitrary",) * len(grid)
    in_specs, out_specs, out_shape, scratch_shapes = list(in_specs), list(out_specs), list(out_shape), list(scratch_shapes)
    if comm is None:
        res = pl.pallas_call(body, name=name, grid=grid, in_specs=in_specs, out_specs=out_specs, out_shape=out_shape,
                             scratch_shapes=scratch_shapes, compiler_params=_params(*sem))(*args)
        return res, []
    n_in, n_out, n_scr = len(in_specs), len(out_specs), len(scratch_shapes)
    n_cin, n_cout = len(comm.arrays), len(comm.out_shape)
    total = functools.reduce(lambda a, b: a * b, grid)

    def wrapped(*refs):
        ins, cin = refs[:n_in], refs[n_in:n_in + n_cin]
        o0 = n_in + n_cin
        outs, cout = refs[o0:o0 + n_out], refs[o0 + n_out:o0 + n_out + n_cout]
        s0 = o0 + n_out + n_cout
        scr, sems = refs[s0:s0 + n_scr], refs[s0 + n_scr:]
        step = pl.program_id(0)
        for d in range(1, len(grid)):
            step = step * grid[d] + pl.program_id(d)
        for when, before, fn in comm.hooks:
            if before:
                pl.when(step == when % total)(functools.partial(fn, cin, cout, sems))
        body(*ins, *outs, *scr)
        for when, before, fn in comm.hooks:
            if not before:
                pl.when(step == when % total)(functools.partial(fn, cin, cout, sems))

    res = pl.pallas_call(
        wrapped, name=name, grid=grid,
        in_specs=in_specs + [HBM_SPEC] * n_cin, out_specs=out_specs + [HBM_SPEC] * n_cout,
        out_shape=out_shape + list(comm.out_shape), scratch_shapes=scratch_shapes + list(comm.scratch),
        input_output_aliases={n_in + i: n_out + o for i, o in comm.aliases.items()},
        compiler_params=_params(*sem),
    )(*args, *comm.arrays)
    return res[:n_out], res[n_out:]


def _gather_steps(n, view, own_src, send_sems, recv_sems):
    x, y, c = _coords()
    me, sibling = (x, y, c), (x, y, 1 - c)
    chips = _other_chips(x, y)

    def copy(a, k, block, to, src=None):
        return pltpu.make_async_remote_copy(
            src_ref=view(a, block) if src is None else src, dst_ref=view(a, block),
            send_sem=send_sems.at[a, k], recv_sem=recv_sems.at[a, k], device_id=to, device_id_type=MESH)

    def first_copies():
        cps = []
        for a in range(n):
            cps.append(copy(a, 0, me, sibling, src=own_src(a)))
            cps += [copy(a, 1 + j, me, (*chip, c), src=own_src(a)) for j, chip in enumerate(chips)]
        return cps

    def passed_copies():
        return [copy(a, 4 + j, (*chip, c), sibling) for j, chip in enumerate(chips) for a in range(n)]

    def first():
        for cp in first_copies():
            cp.start()

    def forward():
        for j, chip in enumerate(chips):
            for a in range(n):
                copy(a, 1 + j, (*chip, c), me).wait_recv()
                copy(a, 4 + j, (*chip, c), sibling).start()

    def finish():
        for a in range(n):
            copy(a, 0, sibling, me).wait_recv()
            for j, chip in enumerate(chips):
                copy(a, 4 + j, (*chip, 1 - c), me).wait_recv()
        for cp in first_copies() + passed_copies():
            cp.wait_send()

    return first, forward, finish


def _gather_comm(arrays, forward_step):
    n = len(arrays)

    def steps(cout, sems):
        view = lambda a, dev: cout[a].at[_device_index(dev)]
        return _gather_steps(n, view, lambda a: view(a, _coords()), sems[0], sems[1])

    hooks = [(0, True, lambda cin, cout, sems: steps(cout, sems)[0]()),
             (forward_step, True, lambda cin, cout, sems: steps(cout, sems)[1]()),
             (-1, False, lambda cin, cout, sems: steps(cout, sems)[2]())]
    return _Comm(list(arrays), [jax.ShapeDtypeStruct(a.shape, a.dtype) for a in arrays], {i: i for i in range(n)},
                 [pltpu.SemaphoreType.DMA((n, 7)), pltpu.SemaphoreType.DMA((n, 7))], hooks)


def _scatter_comm(grads):
    n = len(grads)

    def copies(cin, cout, sems):
        send_sems, recv_sems, local_sems = sems
        x, y, c = _coords()
        me = _device_index((x, y, c))
        mine = [pltpu.make_async_copy(cin[a].at[me], cout[a].at[me], local_sems.at[a]) for a in range(n)]
        sends, recvs = [], []
        for a in range(n):
            for mask in range(1, NDEV):
                peer = (1 - x if mask & 4 else x, 1 - y if mask & 2 else y, 1 - c if mask & 1 else c)
                p = _device_index(peer)
                kw = dict(send_sem=send_sems.at[a, mask - 1], recv_sem=recv_sems.at[a, mask - 1],
                          device_id=peer, device_id_type=MESH)
                sends.append(pltpu.make_async_remote_copy(src_ref=cin[a].at[p], dst_ref=cout[a].at[me], **kw))
                recvs.append(pltpu.make_async_remote_copy(src_ref=cin[a].at[p], dst_ref=cout[a].at[p], **kw))
        return mine, sends, recvs

    def start(cin, cout, sems):
        mine, sends, _ = copies(cin, cout, sems)
        for cp in mine + sends:
            cp.start()

    def finish(cin, cout, sems):
        mine, sends, recvs = copies(cin, cout, sems)
        for cp in recvs:
            cp.wait_recv()
        for cp in sends:
            cp.wait_send()
        for cp in mine:
            cp.wait()

    return _Comm(list(grads), [jax.ShapeDtypeStruct(g.shape, g.dtype) for g in grads], {},
                 [pltpu.SemaphoreType.DMA((n, NDEV - 1)), pltpu.SemaphoreType.DMA((n, NDEV - 1)),
                  pltpu.SemaphoreType.DMA((n,))],
                 [(0, True, start), (-1, False, finish)])


NOW_ITEMS = (0, 5, 6)
POOL_ITEM = 5


def _cast_gather_first(shards):
    n = len(shards)
    dtypes = [BF16] * 6 + [F32]
    out_shapes = [(NDEV,) + s.shape for s in shards]
    g, rows, cols = shards[POOL_ITEM].shape
    out_shapes[POOL_ITEM] = (g, rows * NDEV, cols)

    later = [a for a in range(n) if a not in NOW_ITEMS]

    def body(*refs):
        ins, outs, raw, stage = refs[:n], refs[n:2 * n], refs[2 * n:3 * n], refs[3 * n:4 * n]
        send_sems, recv_sems, local_sems, load_sems = refs[4 * n:]

        def view(a, dev):
            i = _device_index(dev)
            if a == POOL_ITEM:
                return outs[a].at[:, pl.ds(i * rows, rows), :]
            return outs[a].at[i]

        loads = [pltpu.make_async_copy(ins[a], raw[a], load_sems.at[a]) for a in range(n)]
        mine = [pltpu.make_async_copy(stage[a], view(a, _coords()), local_sems.at[a]) for a in range(n)]
        for a in list(NOW_ITEMS) + later:
            loads[a].start()
        first, forward, finish = _gather_steps(
            len(NOW_ITEMS), lambda k, dev: view(NOW_ITEMS[k], dev), lambda k: stage[NOW_ITEMS[k]], send_sems, recv_sems)
        for a in list(NOW_ITEMS) + later:
            loads[a].wait()
            stage[a][...] = raw[a][...].astype(dtypes[a])
            mine[a].start()
            if a == NOW_ITEMS[-1]:
                first()
        forward()
        finish()
        for cp in mine:
            cp.wait()

    return pl.pallas_call(
        body, name="cast_gather_first",
        in_specs=[HBM_SPEC] * n, out_specs=[HBM_SPEC] * n,
        out_shape=[jax.ShapeDtypeStruct(s, d) for s, d in zip(out_shapes, dtypes)],
        scratch_shapes=[pltpu.VMEM(s.shape, s.dtype) for s in shards]
        + [pltpu.VMEM(s.shape, d) for s, d in zip(shards, dtypes)]
        + [pltpu.SemaphoreType.DMA((len(NOW_ITEMS), 7)), pltpu.SemaphoreType.DMA((len(NOW_ITEMS), 7)),
           pltpu.SemaphoreType.DMA((n,)), pltpu.SemaphoreType.DMA((n,))],
        compiler_params=pltpu.CompilerParams(vmem_limit_bytes=VMEM_LIMIT),
    )(*shards)


def _inproj(x, g1, win, comm, tm=1024):
    T, D = x.shape
    nb, _, bn = win.shape

    def body(x_ref, g_ref, w_ref, proj_ref, h_ref):
        @pl.when(pl.program_id(1) == 0)
        def _():
            xv = x_ref[...]
            h_ref[...] = (xv * _rsq(xv) * g_ref[...]).astype(BF16)

        proj_ref[...] = jnp.dot(h_ref[...], w_ref[0], preferred_element_type=F32).astype(BF16)

    return _host_call(
        body, name="inproj", grid=(T // tm, nb), comm=comm, args=(x, g1, win),
        in_specs=[pl.BlockSpec((tm, D), lambda i, j: (i, 0)),
                  pl.BlockSpec((1, D), lambda i, j: (0, 0)),
                  pl.BlockSpec((1, D, bn), lambda i, j: (j, 0, 0))],
        out_specs=[pl.BlockSpec((tm, bn), lambda i, j: (i, j)),
                   pl.BlockSpec((tm, D), lambda i, j: (i, 0))],
        out_shape=[jax.ShapeDtypeStruct((T, nb * bn), BF16), jax.ShapeDtypeStruct((T, D), BF16)])


def _mixer_fwd(proj, x, cw, pw, ps, wout, g2, g3, comm, tm=256):
    T, D = x.shape
    P = proj.shape[1]

    def body(proj_ref, x_ref, cw_ref, pw_ref, ps_ref, wout_ref, g2_ref, g3_ref,
             x1_ref, hf_ref, mixed_ref, mo_ref, cu_carry, v_carry):
        i = pl.program_id(0)

        @pl.when(i == 0)
        def _():
            cu_carry[...] = jnp.zeros_like(cu_carry)
            v_carry[...] = jnp.zeros_like(v_carry)

        for h in range(CONV_HEADS):
            lo = h * HEAD_DIM
            gate_b = proj_ref[:, lo:lo + HEAD_DIM].astype(F32)
            cu = proj_ref[:, CONV_WIDTH + lo:CONV_WIDTH + lo + HEAD_DIM].astype(F32) * \
                proj_ref[:, 2 * CONV_WIDTH + lo:2 * CONV_WIDTH + lo + HEAD_DIM].astype(F32)
            ext = jnp.concatenate([cu_carry[:, lo:lo + HEAD_DIM], cu], axis=0)
            c1 = pltpu.roll(ext, 1, 0)[HALO:]
            c2 = pltpu.roll(ext, 2, 0)[HALO:]
            ya = gate_b * (cw_ref[h, 2:3, :] * cu + cw_ref[h, 1:2, :] * c1 + cw_ref[h, 0:1, :] * c2)
            mixed_ref[:, lo:lo + HEAD_DIM] = (ya * _rsq(ya)).astype(BF16)
            cu_carry[:, lo:lo + HEAD_DIM] = cu[tm - HALO:]

        for gi, w in enumerate(POOL_WINDOWS):
            lo = gi * POOL_GROUP_DIM
            v = proj_ref[:, 3 * CONV_WIDTH + lo:3 * CONV_WIDTH + lo + POOL_GROUP_DIM].astype(F32)
            ext = jnp.concatenate([v_carry[:, lo:lo + POOL_GROUP_DIM], v], axis=0)
            pooled = _window_sum(ext, w, True)[HALO:] * _inv_count(i * tm, tm, w) - v
            y = jnp.dot(pooled.astype(BF16), pw_ref[gi], preferred_element_type=F32)
            yb = y * _rsq(y) * ps_ref[:, lo:lo + POOL_GROUP_DIM]
            mixed_ref[:, CONV_WIDTH + lo:CONV_WIDTH + lo + POOL_GROUP_DIM] = yb.astype(BF16)
            v_carry[:, lo:lo + POOL_GROUP_DIM] = v[tm - HALO:]

        mo = jnp.dot(mixed_ref[...], wout_ref[...], preferred_element_type=F32)
        mo_ref[...] = mo
        x1 = x_ref[...] + mo * _rsq(mo) * g2_ref[...]
        x1_ref[...] = x1
        hf_ref[...] = (x1 * _rsq(x1) * g3_ref[...]).astype(BF16)

    row = lambda n: pl.BlockSpec((tm, n), lambda i: (i, 0))
    return _host_call(
        body, name="mixer_fwd", grid=(T // tm,), comm=comm, args=(proj, x, cw, pw, ps, wout, g2, g3),
        in_specs=[row(P), row(D), _whole(cw.shape), _whole(pw.shape), _whole(ps.shape),
                  _whole(wout.shape), _whole(g2.shape), _whole(g3.shape)],
        out_specs=[row(D), row(D), row(D), row(D)],
        out_shape=[jax.ShapeDtypeStruct((T, D), F32), jax.ShapeDtypeStruct((T, D), BF16),
                   jax.ShapeDtypeStruct((T, D), BF16), jax.ShapeDtypeStruct((T, D), F32)],
        scratch_shapes=[pltpu.VMEM((HALO, CONV_WIDTH), F32), pltpu.VMEM((HALO, CONV_WIDTH), F32)])


def _ffn_up(hf, wg, wu, comm, tm=1024):
    T, D = hf.shape
    nb, _, bf = wg.shape

    def body(hf_ref, wg_ref, wu_ref, g_ref, u_ref, a_ref):
        hv = hf_ref[...]
        g = jnp.dot(hv, wg_ref[0], preferred_element_type=F32)
        u = jnp.dot(hv, wu_ref[0], preferred_element_type=F32)
        g_ref[0] = g.astype(BF16)
        u_ref[0] = u.astype(BF16)
        a_ref[0] = (g * jax.nn.sigmoid(g) * u).astype(BF16)

    wspec = pl.BlockSpec((1, D, bf), lambda i, j: (j, 0, 0))
    ospec = pl.BlockSpec((1, tm, bf), lambda i, j: (j, i, 0))
    oshape = jax.ShapeDtypeStruct((nb, T, bf), BF16)
    return _host_call(
        body, name="ffn_up", grid=(T // tm, nb), comm=comm, args=(hf, wg, wu),
        in_specs=[pl.BlockSpec((tm, D), lambda i, j: (i, 0)), wspec, wspec],
        out_specs=[ospec, ospec, ospec], out_shape=[oshape, oshape, oshape])


def _ffn_down_loss(a, wd, x1, tgt, g4, tm=512, kb=2):
    nblk, T, bf = a.shape
    nb = nblk // kb
    D = x1.shape[1]
    nt = T // tm

    def body(a_ref, wd_ref, x1_hbm, tgt_hbm, g4_ref, dy_ref, dff_ref, loss_ref, dg4_ref,
             acc_ref, lacc_ref, x1_ref, tgt_ref, late_sems):
        i, j = pl.program_id(0), pl.program_id(1)
        late = [_late_rows(x1_hbm, x1_ref, late_sems.at[0], i, tm), _late_rows(tgt_hbm, tgt_ref, late_sems.at[1], i, tm)]

        @pl.when((i == 0) & (j == 0))
        def _():
            lacc_ref[...] = jnp.zeros_like(lacc_ref)
            dg4_ref[...] = jnp.zeros_like(dg4_ref)

        @pl.when(j == 0)
        def _():
            for cp in late:
                cp.start()
            acc_ref[...] = jnp.zeros_like(acc_ref)

        part = jnp.dot(a_ref[0], wd_ref[0], preferred_element_type=F32)
        for k in range(1, kb):
            part = part + jnp.dot(a_ref[k], wd_ref[k], preferred_element_type=F32)
        acc_ref[...] += part

        @pl.when(j == nb - 1)
        def _():
            for cp in late:
                cp.wait()
            ff = acc_ref[...]
            r = _rsq(ff)
            n = ff * r
            g4v = g4_ref[...]
            e = x1_ref[...] + n * g4v - tgt_ref[...]
            lacc_ref[...] += jnp.sum(e * e, axis=0, keepdims=True)
            dy = e * (1.0 / D)
            dy_ref[...] = dy
            dg4_ref[...] += jnp.sum(dy * n, axis=0, keepdims=True)
            dff_ref[...] = _norm_bwd(dy * g4v, n, r).astype(BF16)

        @pl.when((i == nt - 1) & (j == nb - 1))
        def _():
            loss_ref[...] = jnp.full(loss_ref.shape, (0.5 / D) * jnp.sum(lacc_ref[...]), F32)

    row = pl.BlockSpec((tm, D), lambda i, j: (i, 0))
    vec = pl.BlockSpec((1, D), lambda i, j: (0, 0))
    return _host_call(
        body, name="ffn_down_loss", grid=(nt, nb), args=(a, wd, x1, tgt, g4),
        in_specs=[pl.BlockSpec((kb, tm, bf), lambda i, j: (j, i, 0)),
                  pl.BlockSpec((kb, bf, D), lambda i, j: (j, 0, 0)), HBM_SPEC, HBM_SPEC, vec],
        out_specs=[row, row, pl.BlockSpec((1, 128), lambda i, j: (0, 0)), vec],
        out_shape=[jax.ShapeDtypeStruct((T, D), F32), jax.ShapeDtypeStruct((T, D), BF16),
                   jax.ShapeDtypeStruct((1, 128), F32), jax.ShapeDtypeStruct((1, D), F32)],
        scratch_shapes=[pltpu.VMEM((tm, D), F32), pltpu.VMEM((1, D), F32), pltpu.VMEM((tm, D), F32),
                        pltpu.VMEM((tm, D), F32), pltpu.SemaphoreType.DMA((2,))])[0]


def _ffn_bwd_act(dff, wd, g, u, comm, tm=1024):
    T, D = dff.shape
    nb, bf, _ = wd.shape

    def body(dff_ref, wd_ref, g_ref, u_ref, dg_ref, du_ref):
        da = lax.dot_general(dff_ref[...], wd_ref[0], NT_DIMS, preferred_element_type=F32)
        gv = g_ref[0].astype(F32)
        s = jax.nn.sigmoid(gv)
        du_ref[0] = (da * (gv * s)).astype(BF16)
        dg_ref[0] = (da * u_ref[0].astype(F32) * (s * (1.0 + gv * (1.0 - s)))).astype(BF16)

    blk = pl.BlockSpec((1, tm, bf), lambda i, j: (j, i, 0))
    oshape = jax.ShapeDtypeStruct((nb, T, bf), BF16)
    return _host_call(
        body, name="ffn_bwd_act", grid=(T // tm, nb), comm=comm, args=(dff, wd, g, u),
        in_specs=[pl.BlockSpec((tm, D), lambda i, j: (i, 0)),
                  pl.BlockSpec((1, bf, D), lambda i, j: (j, 0, 0)), blk, blk],
        out_specs=[blk, blk], out_shape=[oshape, oshape])


def _ffn_bwd_in(dg, du, wg, wu, x1, dy, g3, comm, tm=512):
    nb, T, bf = dg.shape
    D = x1.shape[1]

    def body(dg_ref, du_ref, wg_ref, wu_ref, x1_ref, dy_ref, g3_ref, dx1_ref, dg3_ref, acc_ref):
        i, j = pl.program_id(0), pl.program_id(1)

        @pl.when((i == 0) & (j == 0))
        def _():
            dg3_ref[...] = jnp.zeros_like(dg3_ref)

        @pl.when(j == 0)
        def _():
            acc_ref[...] = jnp.zeros_like(acc_ref)

        acc_ref[...] += (lax.dot_general(dg_ref[0], wg_ref[0], NT_DIMS, preferred_element_type=F32)
                         + lax.dot_general(du_ref[0], wu_ref[0], NT_DIMS, preferred_element_type=F32))

        @pl.when(j == nb - 1)
        def _():
            dhf = acc_ref[...]
            x1v = x1_ref[...]
            r = _rsq(x1v)
            n = x1v * r
            dg3_ref[...] += jnp.sum(dhf * n, axis=0, keepdims=True)
            dx1_ref[...] = dy_ref[...] + _norm_bwd(dhf * g3_ref[...], n, r)

    row = pl.BlockSpec((tm, D), lambda i, j: (i, 0))
    vec = pl.BlockSpec((1, D), lambda i, j: (0, 0))
    ablk = pl.BlockSpec((1, tm, bf), lambda i, j: (j, i, 0))
    wblk = pl.BlockSpec((1, D, bf), lambda i, j: (j, 0, 0))
    return _host_call(
        body, name="ffn_bwd_in", grid=(T // tm, nb), comm=comm, args=(dg, du, wg, wu, x1, dy, g3),
        in_specs=[ablk, ablk, wblk, wblk, row, row, vec],
        out_specs=[row, vec],
        out_shape=[jax.ShapeDtypeStruct((T, D), F32), jax.ShapeDtypeStruct((1, D), F32)],
        scratch_shapes=[pltpu.VMEM((tm, D), F32)])


def _wgrad(name, lhs, rhs, lhs_spec, rhs_spec, n_rhs, M, N, nb, nk, comm=None):
    def body(*refs):
        l_ref, r_refs = refs[0], refs[1:1 + n_rhs]
        o_refs, acc_refs = refs[1 + n_rhs:1 + 2 * n_rhs], refs[1 + 2 * n_rhs:]
        k = pl.program_id(1)
        tile = lambda ref: ref[0] if len(ref.shape) == 3 else ref[...]

        @pl.when(k == 0)
        def _():
            for acc_ref in acc_refs:
                acc_ref[...] = jnp.zeros_like(acc_ref)

        for r_ref, acc_ref in zip(r_refs, acc_refs):
            acc_ref[...] += lax.dot_general(tile(l_ref), tile(r_ref), TN_DIMS, preferred_element_type=F32)

        @pl.when(k == nk - 1)
        def _():
            for o_ref, acc_ref in zip(o_refs, acc_refs):
                o_ref[0] = acc_ref[...].astype(BF16)

    oblk = pl.BlockSpec((1, M, N), lambda j, k: (j, 0, 0))
    oshape = jax.ShapeDtypeStruct((nb, M, N), BF16)
    return _host_call(
        body, name=name, grid=(nb, nk), comm=comm, args=(lhs, *rhs),
        in_specs=[lhs_spec] + [rhs_spec] * n_rhs, out_specs=[oblk] * n_rhs, out_shape=[oshape] * n_rhs,
        scratch_shapes=[pltpu.VMEM((M, N), F32)] * n_rhs)


def _wgrad_down(a, dff, tk=1024):
    nb, T, M = a.shape
    N = dff.shape[1]
    return _wgrad("wgrad_down", a, [dff], pl.BlockSpec((1, tk, M), lambda j, k: (j, k, 0)),
                  pl.BlockSpec((tk, N), lambda j, k: (k, 0)), 1, M, N, nb, T // tk)[0][0]


def _wgrad_gate_up(hf, dg, du, tk=1024):
    T, M = hf.shape
    nb, _, N = dg.shape
    return _wgrad("wgrad_gate_up", hf, [dg, du], pl.BlockSpec((tk, M), lambda j, k: (k, 0)),
                  pl.BlockSpec((1, tk, N), lambda j, k: (j, k, 0)), 2, M, N, nb, T // tk)[0]


def _wgrad_wide(name, lhs, rhs, n_split, kb, tk, comm=None):
    T, M = lhs.shape
    N = rhs.shape[1]
    slab = N // n_split
    nk = T // tk

    def body(l_ref, r_ref, o_ref, acc_ref):
        k = pl.program_id(1)

        @pl.when(k == 0)
        def _():
            acc_ref[...] = jnp.zeros_like(acc_ref)

        acc_ref[...] += lax.dot_general(l_ref[...], r_ref[...], TN_DIMS, preferred_element_type=F32)

        @pl.when(k == nk - 1)
        def _():
            if kb == 0:
                o_ref[...] = acc_ref[...].astype(BF16)
            for b in range(kb):
                o_ref[b] = acc_ref[:, b * (slab // kb):(b + 1) * (slab // kb)].astype(BF16)

    if kb == 0:
        out_spec, out_shape = pl.BlockSpec((M, slab), lambda j, k: (0, j)), (M, N)
    else:
        out_spec, out_shape = pl.BlockSpec((kb, M, slab // kb), lambda j, k: (j, 0, 0)), (n_split * kb, M, slab // kb)
    return _host_call(
        body, name=name, grid=(n_split, nk), comm=comm, args=(lhs, rhs),
        in_specs=[pl.BlockSpec((tk, M), lambda j, k: (k, 0)), pl.BlockSpec((tk, slab), lambda j, k: (k, j))],
        out_specs=[out_spec], out_shape=[jax.ShapeDtypeStruct(out_shape, BF16)],
        scratch_shapes=[pltpu.VMEM((M, slab), F32)])


def _wgrad_out(mixed, dmo, nb=NDEV, tk=1024):
    D = mixed.shape[1]
    res, _ = _wgrad_wide("wgrad_out", mixed, dmo, 2, 0, tk)
    return res[0].reshape(nb, D // nb, D)


def _wgrad_in(h, dproj, comm, nb=NDEV, tk=1024):
    res, cres = _wgrad_wide("wgrad_in", h, dproj, nb // 2, 2, tk, comm=comm)
    return res[0], cres


def _mixer_bwd(dx1, mo, proj, cw, pw, ps, wout, g2, tm=256):
    T, D = dx1.shape
    P = proj.shape[1]
    nt = T // tm
    n_ext = tm + HALO
    hb = tm // HALO

    def body(dx1_ref, mo_ref, proj_ref, hc_ref, hu_ref, hv_ref, cw_ref, pw_ref, ps_ref, wout_ref, g2_ref,
             dmo_ref, dproj_ref, dg2_ref, dcw_ref, dps_ref, dpw_ref, dmix_ref, dconv_carry, q_carry):
        i = pl.program_id(0)
        tile = nt - 1 - i

        @pl.when(i == 0)
        def _():
            dconv_carry[...] = jnp.zeros_like(dconv_carry)
            q_carry[...] = jnp.zeros_like(q_carry)
            dg2_ref[...] = jnp.zeros_like(dg2_ref)
            dcw_ref[...] = jnp.zeros_like(dcw_ref)
            dps_ref[...] = jnp.zeros_like(dps_ref)
            dpw_ref[...] = jnp.zeros_like(dpw_ref)

        mov = mo_ref[...]
        r2 = _rsq(mov)
        n2 = mov * r2
        dx1v = dx1_ref[...]
        dg2_ref[...] += jnp.sum(dx1v * n2, axis=0, keepdims=True)
        dmo = _norm_bwd(dx1v * g2_ref[...], n2, r2).astype(BF16)
        dmo_ref[...] = dmo
        dmix_ref[...] = lax.dot_general(dmo, wout_ref[...], NT_DIMS, preferred_element_type=F32)

        has_prev = (tile > 0).astype(F32)

        for h in range(CONV_HEADS):
            lo = h * HEAD_DIM
            sl = slice(lo, lo + HEAD_DIM)
            gate_b = proj_ref[:, lo:lo + HEAD_DIM].astype(F32)
            gate_c = proj_ref[:, CONV_WIDTH + lo:CONV_WIDTH + lo + HEAD_DIM].astype(F32)
            uu = proj_ref[:, 2 * CONV_WIDTH + lo:2 * CONV_WIDTH + lo + HEAD_DIM].astype(F32)
            cu = gate_c * uu
            ext = jnp.concatenate([hc_ref[:, sl].astype(F32) * hu_ref[:, sl].astype(F32) * has_prev, cu], axis=0)
            c1 = pltpu.roll(ext, 1, 0)[HALO:]
            c2 = pltpu.roll(ext, 2, 0)[HALO:]
            w0, w1, w2 = cw_ref[h, 0:1, :], cw_ref[h, 1:2, :], cw_ref[h, 2:3, :]
            conv = w2 * cu + w1 * c1 + w0 * c2
            ya = gate_b * conv
            ra = _rsq(ya)
            dya = _norm_bwd(dmix_ref[:, sl], ya * ra, ra)
            dconv = dya * gate_b
            dcw_ref[h, 0:1, :] += jnp.sum(dconv * c2, axis=0, keepdims=True)
            dcw_ref[h, 1:2, :] += jnp.sum(dconv * c1, axis=0, keepdims=True)
            dcw_ref[h, 2:3, :] += jnp.sum(dconv * cu, axis=0, keepdims=True)
            extd = jnp.concatenate([dconv, dconv_carry[:, sl]], axis=0)
            d1 = pltpu.roll(extd, n_ext - 1, 0)[:tm]
            d2 = pltpu.roll(extd, n_ext - 2, 0)[:tm]
            dcu = w2 * dconv + w1 * d1 + w0 * d2
            dconv_carry[:, sl] = dconv[:HALO]
            dproj_ref[:, lo:lo + HEAD_DIM] = (dya * conv).astype(BF16)
            dproj_ref[:, CONV_WIDTH + lo:CONV_WIDTH + lo + HEAD_DIM] = (dcu * uu).astype(BF16)
            dproj_ref[:, 2 * CONV_WIDTH + lo:2 * CONV_WIDTH + lo + HEAD_DIM] = (dcu * gate_c).astype(BF16)

        for gi, w in enumerate(POOL_WINDOWS):
            lo = gi * POOL_GROUP_DIM
            sl = slice(lo, lo + POOL_GROUP_DIM)
            v = proj_ref[:, 3 * CONV_WIDTH + lo:3 * CONV_WIDTH + lo + POOL_GROUP_DIM].astype(F32)
            inv = _inv_count(tile * tm, tm, w)
            ext = jnp.concatenate([hv_ref[:, sl].astype(F32) * has_prev, v], axis=0)
            pooled = (_window_sum(ext, w, True)[HALO:] * inv - v).astype(BF16)
            y = jnp.dot(pooled, pw_ref[gi], preferred_element_type=F32)
            rp = _rsq(y)
            nb_ = y * rp
            dyb = dmix_ref[:, CONV_WIDTH + lo:CONV_WIDTH + lo + POOL_GROUP_DIM]
            dps_ref[:, sl] += jnp.sum(dyb * nb_, axis=0, keepdims=True)
            dy = _norm_bwd(dyb * ps_ref[:, sl], nb_, rp).astype(BF16)
            dpw_ref[gi] += lax.dot_general(pooled, dy, TN_DIMS, preferred_element_type=F32)
            dpooled = lax.dot_general(dy, pw_ref[gi], NT_DIMS, preferred_element_type=F32)
            q = dpooled * inv
            extq = jnp.concatenate([q, q_carry[:, sl]], axis=0)
            dv = _window_sum(extq, w, False)[:tm] - dpooled
            q_carry[:, sl] = q[:HALO]
            dproj_ref[:, 3 * CONV_WIDTH + lo:3 * CONV_WIDTH + lo + POOL_GROUP_DIM] = dv.astype(BF16)

    rev = lambda n: pl.BlockSpec((tm, n), lambda i: (nt - 1 - i, 0))

    def halo(col):
        return pl.BlockSpec((HALO, CONV_WIDTH), lambda i: (jnp.maximum((nt - 1 - i) * hb - 1, 0), col))

    return _host_call(
        body, name="mixer_bwd", grid=(nt,), args=(dx1, mo, proj, proj, proj, proj, cw, pw, ps, wout, g2),
        in_specs=[rev(D), rev(D), rev(P), halo(1), halo(2), halo(3), _whole(cw.shape), _whole(pw.shape),
                  _whole(ps.shape), _whole(wout.shape), _whole(g2.shape)],
        out_specs=[rev(D), rev(P), pl.BlockSpec((1, D), lambda i: (0, 0)),
                   pl.BlockSpec(cw.shape, lambda i: (0, 0, 0)), pl.BlockSpec(ps.shape, lambda i: (0, 0)),
                   pl.BlockSpec(pw.shape, lambda i: (0, 0, 0))],
        out_shape=[jax.ShapeDtypeStruct((T, D), BF16), jax.ShapeDtypeStruct((T, P), BF16),
                   jax.ShapeDtypeStruct((1, D), F32), jax.ShapeDtypeStruct(cw.shape, F32),
                   jax.ShapeDtypeStruct(ps.shape, F32), jax.ShapeDtypeStruct(pw.shape, F32)],
        scratch_shapes=[pltpu.VMEM((tm, D), F32), pltpu.VMEM((HALO, CONV_WIDTH), F32),
                        pltpu.VMEM((HALO, CONV_WIDTH), F32)])[0]


def _inproj_bwd(dproj, win, x, dx1, g1, comm, tm=512, kb=2):
    T, D = x.shape
    nblk, _, bn = win.shape
    nb = nblk // kb

    def body(dp_ref, w_ref, x_ref, dx1_ref, g1_ref, gx_ref, dg1_ref, acc_ref):
        i, j = pl.program_id(0), pl.program_id(1)

        @pl.when((i == 0) & (j == 0))
        def _():
            dg1_ref[...] = jnp.zeros_like(dg1_ref)

        @pl.when(j == 0)
        def _():
            acc_ref[...] = jnp.zeros_like(acc_ref)

        part = lax.dot_general(dp_ref[:, 0:bn], w_ref[0], NT_DIMS, preferred_element_type=F32)
        for k in range(1, kb):
            part = part + lax.dot_general(dp_ref[:, k * bn:(k + 1) * bn], w_ref[k], NT_DIMS,
                                          preferred_element_type=F32)
        acc_ref[...] += part

        @pl.when(j == nb - 1)
        def _():
            dh = acc_ref[...]
            xv = x_ref[...]
            r = _rsq(xv)
            n = xv * r
            dg1_ref[...] += jnp.sum(dh * n, axis=0, keepdims=True)
            gx_ref[...] = dx1_ref[...] + _norm_bwd(dh * g1_ref[...], n, r)

    row = pl.BlockSpec((tm, D), lambda i, j: (i, 0))
    vec = pl.BlockSpec((1, D), lambda i, j: (0, 0))
    return _host_call(
        body, name="inproj_bwd", grid=(T // tm, nb), comm=comm, args=(dproj, win, x, dx1, g1),
        in_specs=[pl.BlockSpec((tm, kb * bn), lambda i, j: (i, j)),
                  pl.BlockSpec((kb, D, bn), lambda i, j: (j, 0, 0)), row, row, vec],
        out_specs=[row, vec],
        out_shape=[jax.ShapeDtypeStruct((T, D), F32), jax.ShapeDtypeStruct((1, D), F32)],
        scratch_shapes=[pltpu.VMEM((tm, D), F32)])


def _adamw(w, g, m, v):
    m = ADAM_B1 * m + (1.0 - ADAM_B1) * g
    v = ADAM_B2 * v + (1.0 - ADAM_B2) * jnp.square(g)
    m_hat = m / (1.0 - ADAM_B1 ** ADAM_STEP)
    v_hat = v / (1.0 - ADAM_B2 ** ADAM_STEP)
    delta = -ADAM_LR * (m_hat / (jnp.sqrt(v_hat) + ADAM_EPS) + ADAM_WD * w)
    return delta, m, v


def _sum_adamw(parts, w, m, v, name, tr):
    r, cd = w.shape

    def body(p_ref, w_ref, m_ref, v_ref, g_ref, d_ref, mo_ref, vo_ref):
        g = p_ref[0].astype(F32)
        for k in range(1, NDEV):
            g = g + p_ref[k].astype(F32)
        g_ref[...] = g
        d_ref[...], mo_ref[...], vo_ref[...] = _adamw(w_ref[...], g, m_ref[...], v_ref[...])

    blk = pl.BlockSpec((tr, cd), lambda i: (i, 0))
    shp = jax.ShapeDtypeStruct((r, cd), F32)
    return pl.pallas_call(
        body, name=name, grid=(r // tr,),
        in_specs=[pl.BlockSpec((NDEV, tr, cd), lambda i: (0, i, 0)), blk, blk, blk],
        out_specs=[blk] * 4, out_shape=[shp] * 4,
        compiler_params=_params("arbitrary"),
    )(parts, w, m, v)


def _small_reduce_adamw(vec_grads, dps, dcw, dpw, vec_state, ps_state, cw_state, pw_state):
    D = vec_grads[0].shape[1]
    pw_rows = pw_state[0].shape[1]
    states = list(vec_state) + [ps_state, cw_state, pw_state]
    n_in = 4 + 3 + 3 * len(states)
    n_out = 4 * len(states)

    def body(*refs):
        dg = refs[0:4]
        dps_ref, dcw_ref, dpw_ref = refs[4:7]
        st = refs[7:n_in]
        outs = refs[n_in:n_in + n_out]
        pack, gat, cbuf, pbuf, send_sems, recv_sems, local_sems = refs[n_in + n_out:]
        x, y, c = _coords()
        me = _device_index((x, y, c))

        pack[...] = jnp.zeros_like(pack)
        for k in range(4):
            pack[k:k + 1, :] = dg[k][...]
        pack[4:5, 0:dps_ref.shape[1]] = dps_ref[...]

        def pw_slice(i):
            return dpw_ref.at[:, pl.ds(i * pw_rows, pw_rows), :]

        mine = [pltpu.make_async_copy(pack, gat.at[me], local_sems.at[0]),
                pltpu.make_async_copy(dcw_ref.at[me], cbuf.at[me], local_sems.at[1]),
                pltpu.make_async_copy(pw_slice(me), pbuf.at[me], local_sems.at[2])]
        for cp in mine:
            cp.start()
        sends, recvs = [], []
        for mask in range(1, NDEV):
            peer = (1 - x if mask & 4 else x, 1 - y if mask & 2 else y, 1 - c if mask & 1 else c)
            p = _device_index(peer)
            for k, (src, buf) in enumerate(((pack, gat), (dcw_ref.at[p], cbuf), (pw_slice(p), pbuf))):
                kw = dict(send_sem=send_sems.at[mask, k], recv_sem=recv_sems.at[mask, k],
                          device_id=peer, device_id_type=MESH)
                sends.append(pltpu.make_async_remote_copy(src_ref=src, dst_ref=buf.at[me], **kw))
                recvs.append(pltpu.make_async_remote_copy(src_ref=src, dst_ref=buf.at[p], **kw))
                sends[-1].start()
        for cp in recvs:
            cp.wait_recv()
        for cp in sends:
            cp.wait_send()
        for cp in mine:
            cp.wait()

        def slot_sum(buf):
            s = buf[0]
            for k in range(1, NDEV):
                s = s + buf[k]
            return s

        vec = slot_sum(gat)
        grads = [vec[k:k + 1, :] for k in range(4)] + [vec[4:5, 0:dps_ref.shape[1]], slot_sum(cbuf), slot_sum(pbuf)]
        for k, g in enumerate(grads):
            w_ref, m_ref, v_ref = st[3 * k:3 * k + 3]
            outs[4 * k][...] = g
            outs[4 * k + 1][...], outs[4 * k + 2][...], outs[4 * k + 3][...] = _adamw(
                w_ref[...], g, m_ref[...], v_ref[...])

    flat_state = [a for s in states for a in s]
    out_shape = [jax.ShapeDtypeStruct(s[0].shape, F32) for s in states for _ in range(4)]
    return pl.pallas_call(
        body, name="small_reduce_adamw",
        in_specs=[VMEM_SPEC] * n_in, out_specs=[VMEM_SPEC] * n_out, out_shape=out_shape,
        scratch_shapes=[pltpu.VMEM((NDEV, D), F32), pltpu.VMEM((NDEV, NDEV, D), F32),
                        pltpu.VMEM((NDEV,) + cw_state[0].shape, F32), pltpu.VMEM((NDEV,) + pw_state[0].shape, F32),
                        pltpu.SemaphoreType.DMA((NDEV, 3)), pltpu.SemaphoreType.DMA((NDEV, 3)),
                        pltpu.SemaphoreType.DMA((3,))],
        compiler_params=pltpu.CompilerParams(vmem_limit_bytes=VMEM_LIMIT),
    )(*vec_grads, dps, dcw, dpw, *flat_state)


ROW_TILE = dict(w_in=512, w_gate=256, w_up=256, w_down=176, w_out=128)
FORWARD_STEP = dict(inproj=56, mixer_fwd=26, ffn_up=32)


def kernel(x, ln_mix_pre, w_in, conv_w, pool_w, pool_scale, w_out, ln_mix_post, ln_ffn_pre, w_gate, w_up, w_down, ln_ffn_post, loss_target, m_ln_mix_pre, m_w_in, m_conv_w, m_pool_w, m_pool_scale, m_w_out, m_ln_mix_post, m_ln_ffn_pre, m_w_gate, m_w_up, m_w_down, m_ln_ffn_post, v_ln_mix_pre, v_w_in, v_conv_w, v_pool_w, v_pool_scale, v_w_out, v_ln_mix_post, v_ln_ffn_pre, v_w_gate, v_w_up, v_w_down, v_ln_ffn_post):
    D = x.shape[2]
    xs, tgt = x[0], loss_target[0]
    win, wg, wu, wd, wout, pw, cw = _cast_gather_first(
        [w_in[0], w_gate[0], w_up[0], w_down[0], w_out[0], pool_w[0], conv_w[0]])

    (proj, h), (wout, wg) = _inproj(xs, ln_mix_pre, win, _gather_comm([wout, wg], FORWARD_STEP["inproj"]))
    wout2 = wout.reshape(D, D)
    (x1, hf, mixed, mo), (wu,) = _mixer_fwd(proj, xs, cw, pw, pool_scale, wout2, ln_mix_post, ln_ffn_pre,
                                            _gather_comm([wu], FORWARD_STEP["mixer_fwd"]))
    (g, u, a), (wd,) = _ffn_up(hf, wg, wu, _gather_comm([wd], FORWARD_STEP["ffn_up"]))
    dy, dff, loss, dg4 = _ffn_down_loss(a, wd, x1, tgt, ln_ffn_post)

    dwd = _wgrad_down(a, dff)
    (dg, du), (dwd_parts,) = _ffn_bwd_act(dff, wd, g, u, _scatter_comm([dwd]))
    dwg, dwu = _wgrad_gate_up(hf, dg, du)
    (dx1, dg3), (dwg_parts, dwu_parts) = _ffn_bwd_in(dg, du, wg, wu, x1, dy, ln_ffn_pre, _scatter_comm([dwg, dwu]))
    dmo, dproj, dg2, dcw, dps, dpw = _mixer_bwd(dx1, mo, proj, cw, pw, pool_scale, wout2, ln_mix_post)
    dwout = _wgrad_out(mixed, dmo)
    dwin, (dwout_parts,) = _wgrad_in(h, dproj, _scatter_comm([dwout]))
    (gx, dg1), (dwin_parts,) = _inproj_bwd(dproj, win, xs, dx1, ln_mix_pre, _scatter_comm([dwin]))

    loss = lax.psum(loss[0, 0], ("x", "y", "c"))
    res = {}
    for k, parts, w, m, v in (("w_down", dwd_parts, w_down, m_w_down, v_w_down),
                              ("w_gate", dwg_parts, w_gate, m_w_gate, v_w_gate),
                              ("w_up", dwu_parts, w_up, m_w_up, v_w_up),
                              ("w_out", dwout_parts, w_out, m_w_out, v_w_out),
                              ("w_in", dwin_parts, w_in, m_w_in, v_w_in)):
        res[k] = [o.reshape(w.shape) for o in _sum_adamw(parts, w[0], m[0], v[0], "sum_adamw_" + k, ROW_TILE[k])]

    small = _small_reduce_adamw(
        [dg1, dg2, dg3, dg4], dps, dcw, dpw,
        [(ln_mix_pre, m_ln_mix_pre, v_ln_mix_pre), (ln_mix_post, m_ln_mix_post, v_ln_mix_post),
         (ln_ffn_pre, m_ln_ffn_pre, v_ln_ffn_pre), (ln_ffn_post, m_ln_ffn_post, v_ln_ffn_post)],
        (pool_scale, m_pool_scale, v_pool_scale), (conv_w[0], m_conv_w[0], v_conv_w[0]),
        (pool_w[0], m_pool_w[0], v_pool_w[0]))
    small_names = ["ln_mix_pre", "ln_mix_post", "ln_ffn_pre", "ln_ffn_post", "pool_scale", "conv_w", "pool_w"]
    shapes = dict(conv_w=conv_w.shape, pool_w=pool_w.shape)
    for i, k in enumerate(small_names):
        res[k] = [o.reshape(shapes[k]) if k in shapes else o for o in small[4 * i:4 * i + 4]]

    order = ["ln_mix_pre", "w_in", "conv_w", "pool_w", "pool_scale", "w_out", "ln_mix_post", "ln_ffn_pre",
             "w_gate", "w_up", "w_down", "ln_ffn_post"]
    return (loss, gx[None], *[res[k][0] for k in order], *[res[k][1] for k in order],
            *[res[k][2] for k in order], *[res[k][3] for k in order])
```

```python
import functools
from typing import Any, NamedTuple

import jax
import jax.numpy as jnp
from jax import lax
from jax.experimental import pallas as pl
from jax.experimental.pallas import tpu as pltpu

EPS = 1e-6
NDEV = 8
CONV_HEADS = 8
HEAD_DIM = 128
CONV_WIDTH = CONV_HEADS * HEAD_DIM
POOL_WINDOWS = (2, 4, 8, 16)
POOL_GROUP_DIM = 256
HALO = 16

ADAM_LR = 0.001
ADAM_B1 = 0.9
ADAM_B2 = 0.999
ADAM_EPS = 1e-08
ADAM_WD = 0.01
ADAM_STEP = 10

F32 = jnp.float32
BF16 = jnp.bfloat16
VMEM_LIMIT = 58 * 1024 * 1024
MESH = pl.DeviceIdType.MESH
HBM_SPEC = pl.BlockSpec(memory_space=pl.ANY)
VMEM_SPEC = pl.BlockSpec(memory_space=pltpu.VMEM)

NT_DIMS = (((1,), (1,)), ((), ()))
TN_DIMS = (((0,), (0,)), ((), ()))


def _params(*sem):
    return pltpu.CompilerParams(dimension_semantics=sem, vmem_limit_bytes=VMEM_LIMIT)


def _rsq(v):
    return lax.rsqrt(jnp.mean(v * v, axis=-1, keepdims=True) + EPS)


def _norm_bwd(dn, n, r):
    return r * (dn - n * jnp.mean(dn * n, axis=-1, keepdims=True))


def _whole(shape):
    nd = len(shape)
    return pl.BlockSpec(shape, lambda *_: (0,) * nd, pipeline_mode=pl.Buffered(1))


def _inv_count(t0, tm, w):
    t = t0 + lax.broadcasted_iota(jnp.int32, (tm, 1), 0)
    return 1.0 / jnp.minimum(t + 1, w).astype(F32)


def _window_sum(ext, w, back):
    n = ext.shape[0]
    s, shift = ext, 1
    while shift < w:
        s = s + pltpu.roll(s, shift if back else n - shift, 0)
        shift *= 2
    return s


class _Comm(NamedTuple):
    arrays: Any
    out_shape: Any
    aliases: Any
    scratch: Any
    hooks: Any


def _coords():
    return lax.axis_index("x"), lax.axis_index("y"), lax.axis_index("c")


def _other_chips(x, y):
    return [(1 - x, y), (x, 1 - y), (1 - x, 1 - y)]


def _device_index(dev):
    return 4 * dev[0] + 2 * dev[1] + dev[2]


def _host_call(body, *, name, grid, in_specs, out_specs, out_shape, args, scratch_shapes=(), comm=None):
    sem = ("arbitrary",) * len(grid)
    in_specs, out_specs, out_shape, scratch_shapes = list(in_specs), list(out_specs), list(out_shape), list(scratch_shapes)
    if comm is None:
        res = pl.pallas_call(body, name=name, grid=grid, in_specs=in_specs, out_specs=out_specs, out_shape=out_shape,
                             scratch_shapes=scratch_shapes, compiler_params=_params(*sem))(*args)
        return res, []
    n_in, n_out, n_scr = len(in_specs), len(out_specs), len(scratch_shapes)
    n_cin, n_cout = len(comm.arrays), len(comm.out_shape)
    total = functools.reduce(lambda a, b: a * b, grid)

    def wrapped(*refs):
        ins, cin = refs[:n_in], refs[n_in:n_in + n_cin]
        o0 = n_in + n_cin
        outs, cout = refs[o0:o0 + n_out], refs[o0 + n_out:o0 + n_out + n_cout]
        s0 = o0 + n_out + n_cout
        scr, sems = refs[s0:s0 + n_scr], refs[s0 + n_scr:]
        step = pl.program_id(0)
        for d in range(1, len(grid)):
            step = step * grid[d] + pl.program_id(d)
        for when, before, fn in comm.hooks:
            if before:
                pl.when(step == when % total)(functools.partial(fn, cin, cout, sems))
        body(*ins, *outs, *scr)
        for when, before, fn in comm.hooks:
            if not before:
                pl.when(step == when % total)(functools.partial(fn, cin, cout, sems))

    res = pl.pallas_call(
        wrapped, name=name, grid=grid,
        in_specs=in_specs + [HBM_SPEC] * n_cin, out_specs=out_specs + [HBM_SPEC] * n_cout,
        out_shape=out_shape + list(comm.out_shape), scratch_shapes=scratch_shapes + list(comm.scratch),
        input_output_aliases={n_in + i: n_out + o for i, o in comm.aliases.items()},
        compiler_params=_params(*sem),
    )(*args, *comm.arrays)
    return res[:n_out], res[n_out:]


def _gather_steps(n, view, own_src, send_sems, recv_sems):
    x, y, c = _coords()
    me, sibling = (x, y, c), (x, y, 1 - c)
    chips = _other_chips(x, y)

    def copy(a, k, block, to, src=None):
        return pltpu.make_async_remote_copy(
            src_ref=view(a, block) if src is None else src, dst_ref=view(a, block),
            send_sem=send_sems.at[a, k], recv_sem=recv_sems.at[a, k], device_id=to, device_id_type=MESH)

    def first_copies():
        cps = []
        for a in range(n):
            cps.append(copy(a, 0, me, sibling, src=own_src(a)))
            cps += [copy(a, 1 + j, me, (*chip, c), src=own_src(a)) for j, chip in enumerate(chips)]
        return cps

    def passed_copies():
        return [copy(a, 4 + j, (*chip, c), sibling) for j, chip in enumerate(chips) for a in range(n)]

    def first():
        for cp in first_copies():
            cp.start()

    def forward():
        for j, chip in enumerate(chips):
            for a in range(n):
                copy(a, 1 + j, (*chip, c), me).wait_recv()
                copy(a, 4 + j, (*chip, c), sibling).start()

    def finish():
        for a in range(n):
            copy(a, 0, sibling, me).wait_recv()
            for j, chip in enumerate(chips):
                copy(a, 4 + j, (*chip, 1 - c), me).wait_recv()
        for cp in first_copies() + passed_copies():
            cp.wait_send()

    return first, forward, finish


def _gather_comm(arrays, forward_step):
    n = len(arrays)

    def steps(cout, sems):
        view = lambda a, dev: cout[a].at[_device_index(dev)]
        return _gather_steps(n, view, lambda a: view(a, _coords()), sems[0], sems[1])

    hooks = [(0, True, lambda cin, cout, sems: steps(cout, sems)[0]()),
             (forward_step, True, lambda cin, cout, sems: steps(cout, sems)[1]()),
             (-1, False, lambda cin, cout, sems: steps(cout, sems)[2]())]
    return _Comm(list(arrays), [jax.ShapeDtypeStruct(a.shape, a.dtype) for a in arrays], {i: i for i in range(n)},
                 [pltpu.SemaphoreType.DMA((n, 7)), pltpu.SemaphoreType.DMA((n, 7))], hooks)


def _scatter_comm(grads):
    n = len(grads)

    def copies(cin, cout, sems):
        send_sems, recv_sems, local_sems = sems
        x, y, c = _coords()
        me = _device_index((x, y, c))
        mine = [pltpu.make_async_copy(cin[a].at[me], cout[a].at[me], local_sems.at[a]) for a in range(n)]
        sends, recvs = [], []
        for a in range(n):
            for mask in range(1, NDEV):
                peer = (1 - x if mask & 4 else x, 1 - y if mask & 2 else y, 1 - c if mask & 1 else c)
                p = _device_index(peer)
                kw = dict(send_sem=send_sems.at[a, mask - 1], recv_sem=recv_sems.at[a, mask - 1],
                          device_id=peer, device_id_type=MESH)
                sends.append(pltpu.make_async_remote_copy(src_ref=cin[a].at[p], dst_ref=cout[a].at[me], **kw))
                recvs.append(pltpu.make_async_remote_copy(src_ref=cin[a].at[p], dst_ref=cout[a].at[p], **kw))
        return mine, sends, recvs

    def start(cin, cout, sems):
        mine, sends, _ = copies(cin, cout, sems)
        for cp in mine + sends:
            cp.start()

    def finish(cin, cout, sems):
        mine, sends, recvs = copies(cin, cout, sems)
        for cp in recvs:
            cp.wait_recv()
        for cp in sends:
            cp.wait_send()
        for cp in mine:
            cp.wait()

    return _Comm(list(grads), [jax.ShapeDtypeStruct(g.shape, g.dtype) for g in grads], {},
                 [pltpu.SemaphoreType.DMA((n, NDEV - 1)), pltpu.SemaphoreType.DMA((n, NDEV - 1)),
                  pltpu.SemaphoreType.DMA((n,))],
                 [(0, True, start), (-1, False, finish)])


NOW_ITEMS = (0, 5, 6)
POOL_ITEM = 5


def _cast_gather_first(shards):
    n = len(shards)
    dtypes = [BF16] * 6 + [F32]
    out_shapes = [(NDEV,) + s.shape for s in shards]
    g, rows, cols = shards[POOL_ITEM].shape
    out_shapes[POOL_ITEM] = (g, rows * NDEV, cols)

    later = [a for a in range(n) if a not in NOW_ITEMS]

    def body(*refs):
        ins, outs, raw, stage = refs[:n], refs[n:2 * n], refs[2 * n:3 * n], refs[3 * n:4 * n]
        send_sems, recv_sems, local_sems, load_sems = refs[4 * n:]

        def view(a, dev):
            i = _device_index(dev)
            if a == POOL_ITEM:
                return outs[a].at[:, pl.ds(i * rows, rows), :]
            return outs[a].at[i]

        loads = [pltpu.make_async_copy(ins[a], raw[a], load_sems.at[a]) for a in range(n)]
        mine = [pltpu.make_async_copy(stage[a], view(a, _coords()), local_sems.at[a]) for a in range(n)]
        for a in list(NOW_ITEMS) + later:
            loads[a].start()
        first, forward, finish = _gather_steps(
            len(NOW_ITEMS), lambda k, dev: view(NOW_ITEMS[k], dev), lambda k: stage[NOW_ITEMS[k]], send_sems, recv_sems)
        for a in list(NOW_ITEMS) + later:
            loads[a].wait()
            stage[a][...] = raw[a][...].astype(dtypes[a])
            mine[a].start()
            if a == NOW_ITEMS[-1]:
                first()
        forward()
        finish()
        for cp in mine:
            cp.wait()

    return pl.pallas_call(
        body, name="cast_gather_first",
        in_specs=[HBM_SPEC] * n, out_specs=[HBM_SPEC] * n,
        out_shape=[jax.ShapeDtypeStruct(s, d) for s, d in zip(out_shapes, dtypes)],
        scratch_shapes=[pltpu.VMEM(s.shape, s.dtype) for s in shards]
        + [pltpu.VMEM(s.shape, d) for s, d in zip(shards, dtypes)]
        + [pltpu.SemaphoreType.DMA((len(NOW_ITEMS), 7)), pltpu.SemaphoreType.DMA((len(NOW_ITEMS), 7)),
           pltpu.SemaphoreType.DMA((n,)), pltpu.SemaphoreType.DMA((n,))],
        compiler_params=pltpu.CompilerParams(vmem_limit_bytes=VMEM_LIMIT),
    )(*shards)


def _inproj(x, g1, win, comm, tm=1024):
    T, D = x.shape
    nb, _, bn = win.shape

    def body(x_ref, g_ref, w_ref, proj_ref, h_ref):
        @pl.when(pl.program_id(1) == 0)
        def _():
            xv = x_ref[...]
            h_ref[...] = (xv * _rsq(xv) * g_ref[...]).astype(BF16)

        proj_ref[...] = jnp.dot(h_ref[...], w_ref[0], preferred_element_type=F32).astype(BF16)

    return _host_call(
        body, name="inproj", grid=(T // tm, nb), comm=comm, args=(x, g1, win),
        in_specs=[pl.BlockSpec((tm, D), lambda i, j: (i, 0)),
                  pl.BlockSpec((1, D), lambda i, j: (0, 0)),
                  pl.BlockSpec((1, D, bn), lambda i, j: (j, 0, 0))],
        out_specs=[pl.BlockSpec((tm, bn), lambda i, j: (i, j)),
                   pl.BlockSpec((tm, D), lambda i, j: (i, 0))],
        out_shape=[jax.ShapeDtypeStruct((T, nb * bn), BF16), jax.ShapeDtypeStruct((T, D), BF16)])


def _mixer_fwd(proj, x, cw, pw, ps, wout, g2, g3, comm, tm=256):
    T, D = x.shape
    P = proj.shape[1]

    def body(proj_ref, x_ref, cw_ref, pw_ref, ps_ref, wout_ref, g2_ref, g3_ref,
             x1_ref, hf_ref, mixed_ref, mo_ref, cu_carry, v_carry):
        i = pl.program_id(0)

        @pl.when(i == 0)
        def _():
            cu_carry[...] = jnp.zeros_like(cu_carry)
            v_carry[...] = jnp.zeros_like(v_carry)

        for h in range(CONV_HEADS):
            lo = h * HEAD_DIM
            gate_b = proj_ref[:, lo:lo + HEAD_DIM].astype(F32)
            cu = proj_ref[:, CONV_WIDTH + lo:CONV_WIDTH + lo + HEAD_DIM].astype(F32) * \
                proj_ref[:, 2 * CONV_WIDTH + lo:2 * CONV_WIDTH + lo + HEAD_DIM].astype(F32)
            ext = jnp.concatenate([cu_carry[:, lo:lo + HEAD_DIM], cu], axis=0)
            c1 = pltpu.roll(ext, 1, 0)[HALO:]
            c2 = pltpu.roll(ext, 2, 0)[HALO:]
            ya = gate_b * (cw_ref[h, 2:3, :] * cu + cw_ref[h, 1:2, :] * c1 + cw_ref[h, 0:1, :] * c2)
            mixed_ref[:, lo:lo + HEAD_DIM] = (ya * _rsq(ya)).astype(BF16)
            cu_carry[:, lo:lo + HEAD_DIM] = cu[tm - HALO:]

        for gi, w in enumerate(POOL_WINDOWS):
            lo = gi * POOL_GROUP_DIM
            v = proj_ref[:, 3 * CONV_WIDTH + lo:3 * CONV_WIDTH + lo + POOL_GROUP_DIM].astype(F32)
            ext = jnp.concatenate([v_carry[:, lo:lo + POOL_GROUP_DIM], v], axis=0)
            pooled = _window_sum(ext, w, True)[HALO:] * _inv_count(i * tm, tm, w) - v
            y = jnp.dot(pooled.astype(BF16), pw_ref[gi], preferred_element_type=F32)
            yb = y * _rsq(y) * ps_ref[:, lo:lo + POOL_GROUP_DIM]
            mixed_ref[:, CONV_WIDTH + lo:CONV_WIDTH + lo + POOL_GROUP_DIM] = yb.astype(BF16)
            v_carry[:, lo:lo + POOL_GROUP_DIM] = v[tm - HALO:]

        mo = jnp.dot(mixed_ref[...], wout_ref[...], preferred_element_type=F32)
        mo_ref[...] = mo
        x1 = x_ref[...] + mo * _rsq(mo) * g2_ref[...]
        x1_ref[...] = x1
        hf_ref[...] = (x1 * _rsq(x1) * g3_ref[...]).astype(BF16)

    row = lambda n: pl.BlockSpec((tm, n), lambda i: (i, 0))
    return _host_call(
        body, name="mixer_fwd", grid=(T // tm,), comm=comm, args=(proj, x, cw, pw, ps, wout, g2, g3),
        in_specs=[row(P), row(D), _whole(cw.shape), _whole(pw.shape), _whole(ps.shape),
                  _whole(wout.shape), _whole(g2.shape), _whole(g3.shape)],
        out_specs=[row(D), row(D), row(D), row(D)],
        out_shape=[jax.ShapeDtypeStruct((T, D), F32), jax.ShapeDtypeStruct((T, D), BF16),
                   jax.ShapeDtypeStruct((T, D), BF16), jax.ShapeDtypeStruct((T, D), F32)],
        scratch_shapes=[pltpu.VMEM((HALO, CONV_WIDTH), F32), pltpu.VMEM((HALO, CONV_WIDTH), F32)])


def _ffn_up(hf, wg, wu, comm, tm=1024):
    T, D = hf.shape
    nb, _, bf = wg.shape

    def body(hf_ref, wg_ref, wu_ref, g_ref, u_ref, a_ref):
        hv = hf_ref[...]
        g = jnp.dot(hv, wg_ref[0], preferred_element_type=F32)
        u = jnp.dot(hv, wu_ref[0], preferred_element_type=F32)
        g_ref[0] = g.astype(BF16)
        u_ref[0] = u.astype(BF16)
        a_ref[0] = (g * jax.nn.sigmoid(g) * u).astype(BF16)

    wspec = pl.BlockSpec((1, D, bf), lambda i, j: (j, 0, 0))
    ospec = pl.BlockSpec((1, tm, bf), lambda i, j: (j, i, 0))
    oshape = jax.ShapeDtypeStruct((nb, T, bf), BF16)
    return _host_call(
        body, name="ffn_up", grid=(T // tm, nb), comm=comm, args=(hf, wg, wu),
        in_specs=[pl.BlockSpec((tm, D), lambda i, j: (i, 0)), wspec, wspec],
        out_specs=[ospec, ospec, ospec], out_shape=[oshape, oshape, oshape])


def _ffn_down_loss(a, wd, x1, tgt, g4, tm=256):
    nblk, T, bf = a.shape
    D = x1.shape[1]
    nt = T // tm

    def body(a_ref, wd_ref, x1_ref, tgt_ref, g4_ref, dy_ref, dff_ref, loss_ref, dg4_ref, lacc_ref):
        i = pl.program_id(0)

        @pl.when(i == 0)
        def _():
            lacc_ref[...] = jnp.zeros_like(lacc_ref)
            dg4_ref[...] = jnp.zeros_like(dg4_ref)

        ff = jnp.dot(a_ref[0], wd_ref[0], preferred_element_type=F32)
        for k in range(1, nblk):
            ff = ff + jnp.dot(a_ref[k], wd_ref[k], preferred_element_type=F32)
        r = _rsq(ff)
        n = ff * r
        g4v = g4_ref[...]
        e = x1_ref[...] + n * g4v - tgt_ref[...]
        lacc_ref[...] += jnp.sum(e * e, axis=0, keepdims=True)
        dy = e * (1.0 / D)
        dy_ref[...] = dy
        dg4_ref[...] += jnp.sum(dy * n, axis=0, keepdims=True)
        dff_ref[...] = _norm_bwd(dy * g4v, n, r).astype(BF16)

        @pl.when(i == nt - 1)
        def _():
            loss_ref[...] = jnp.full(loss_ref.shape, (0.5 / D) * jnp.sum(lacc_ref[...]), F32)

    row = pl.BlockSpec((tm, D), lambda i: (i, 0))
    vec = pl.BlockSpec((1, D), lambda i: (0, 0))
    return _host_call(
        body, name="ffn_down_loss", grid=(nt,), args=(a, wd, x1, tgt, g4),
        in_specs=[pl.BlockSpec((nblk, tm, bf), lambda i: (0, i, 0)), _whole(wd.shape), row, row, vec],
        out_specs=[row, row, pl.BlockSpec((1, 128), lambda i: (0, 0)), vec],
        out_shape=[jax.ShapeDtypeStruct((T, D), F32), jax.ShapeDtypeStruct((T, D), BF16),
                   jax.ShapeDtypeStruct((1, 128), F32), jax.ShapeDtypeStruct((1, D), F32)],
        scratch_shapes=[pltpu.VMEM((1, D), F32)])[0]


def _ffn_bwd_act(dff, wd, g, u, comm, tm=256):
    T, D = dff.shape
    nb, bf, _ = wd.shape

    def body(dff_ref, wd_ref, g_ref, u_ref, dg_ref, du_ref):
        dv = dff_ref[...]
        for k in range(nb):
            da = lax.dot_general(dv, wd_ref[k], NT_DIMS, preferred_element_type=F32)
            gv = g_ref[k].astype(F32)
            s = jax.nn.sigmoid(gv)
            du_ref[k] = (da * (gv * s)).astype(BF16)
            dg_ref[k] = (da * u_ref[k].astype(F32) * (s * (1.0 + gv * (1.0 - s)))).astype(BF16)

    blk = pl.BlockSpec((nb, tm, bf), lambda i: (0, i, 0))
    oshape = jax.ShapeDtypeStruct((nb, T, bf), BF16)
    return _host_call(
        body, name="ffn_bwd_act", grid=(T // tm,), comm=comm, args=(dff, wd, g, u),
        in_specs=[pl.BlockSpec((tm, D), lambda i: (i, 0)), _whole(wd.shape), blk, blk],
        out_specs=[blk, blk], out_shape=[oshape, oshape])


def _ffn_bwd_in(dg, du, wg, wu, x1, dy, g3, comms, tm=256):
    nb, T, bf = dg.shape
    D = x1.shape[1]
    hb = nb // 2

    def partial_sum(dg_ref, du_ref, wg_ref, wu_ref):
        s = lax.dot_general(dg_ref[0], wg_ref[0], NT_DIMS, preferred_element_type=F32)
        s = s + lax.dot_general(du_ref[0], wu_ref[0], NT_DIMS, preferred_element_type=F32)
        for k in range(1, hb):
            s = s + lax.dot_general(dg_ref[k], wg_ref[k], NT_DIMS, preferred_element_type=F32)
            s = s + lax.dot_general(du_ref[k], wu_ref[k], NT_DIMS, preferred_element_type=F32)
        return s

    def first(dg_ref, du_ref, wg_ref, wu_ref, part_ref):
        part_ref[...] = partial_sum(dg_ref, du_ref, wg_ref, wu_ref)

    def second(dg_ref, du_ref, wg_ref, wu_ref, part_ref, x1_ref, dy_ref, g3_ref, dx1_ref, dg3_ref):
        @pl.when(pl.program_id(0) == 0)
        def _():
            dg3_ref[...] = jnp.zeros_like(dg3_ref)

        dhf = part_ref[...] + partial_sum(dg_ref, du_ref, wg_ref, wu_ref)
        x1v = x1_ref[...]
        r = _rsq(x1v)
        n = x1v * r
        dg3_ref[...] += jnp.sum(dhf * n, axis=0, keepdims=True)
        dx1_ref[...] = dy_ref[...] + _norm_bwd(dhf * g3_ref[...], n, r)

    row = pl.BlockSpec((tm, D), lambda i: (i, 0))
    vec = pl.BlockSpec((1, D), lambda i: (0, 0))
    rowshape = jax.ShapeDtypeStruct((T, D), F32)

    def specs(half):
        ablk = pl.BlockSpec((hb, tm, bf), lambda i: (half, i, 0))
        wblk = pl.BlockSpec((hb, D, bf), lambda i: (half, 0, 0), pipeline_mode=pl.Buffered(1))
        return [ablk, ablk, wblk, wblk]

    (part,), c0 = _host_call(first, name="ffn_bwd_in_a", grid=(T // tm,), comm=comms[0], args=(dg, du, wg, wu),
                             in_specs=specs(0), out_specs=[row], out_shape=[rowshape])
    res, c1 = _host_call(second, name="ffn_bwd_in_b", grid=(T // tm,), comm=comms[1],
                         args=(dg, du, wg, wu, part, x1, dy, g3),
                         in_specs=specs(1) + [row, row, row, vec], out_specs=[row, vec],
                         out_shape=[rowshape, jax.ShapeDtypeStruct((1, D), F32)])
    return res, list(c0) + list(c1)


def _wgrad(name, lhs, rhs, lhs_spec, rhs_spec, n_rhs, M, N, nb, nk, comm=None):
    def body(*refs):
        l_ref, r_refs = refs[0], refs[1:1 + n_rhs]
        o_refs, acc_refs = refs[1 + n_rhs:1 + 2 * n_rhs], refs[1 + 2 * n_rhs:]
        k = pl.program_id(1)
        tile = lambda ref: ref[0] if len(ref.shape) == 3 else ref[...]

        @pl.when(k == 0)
        def _():
            for acc_ref in acc_refs:
                acc_ref[...] = jnp.zeros_like(acc_ref)

        for r_ref, acc_ref in zip(r_refs, acc_refs):
            acc_ref[...] += lax.dot_general(tile(l_ref), tile(r_ref), TN_DIMS, preferred_element_type=F32)

        @pl.when(k == nk - 1)
        def _():
            for o_ref, acc_ref in zip(o_refs, acc_refs):
                o_ref[0] = acc_ref[...].astype(BF16)

    oblk = pl.BlockSpec((1, M, N), lambda j, k: (j, 0, 0))
    oshape = jax.ShapeDtypeStruct((nb, M, N), BF16)
    return _host_call(
        body, name=name, grid=(nb, nk), comm=comm, args=(lhs, *rhs),
        in_specs=[lhs_spec] + [rhs_spec] * n_rhs, out_specs=[oblk] * n_rhs, out_shape=[oshape] * n_rhs,
        scratch_shapes=[pltpu.VMEM((M, N), F32)] * n_rhs)


def _wgrad_down(a, dff, tk=1024):
    nb, T, M = a.shape
    N = dff.shape[1]
    return _wgrad("wgrad_down", a, [dff], pl.BlockSpec((1, tk, M), lambda j, k: (j, k, 0)),
                  pl.BlockSpec((tk, N), lambda j, k: (k, 0)), 1, M, N, nb, T // tk)[0][0]


def _wgrad_gate_up(hf, dg, du, tk=1024):
    T, M = hf.shape
    nb, _, N = dg.shape
    return _wgrad("wgrad_gate_up", hf, [dg, du], pl.BlockSpec((tk, M), lambda j, k: (k, 0)),
                  pl.BlockSpec((1, tk, N), lambda j, k: (j, k, 0)), 2, M, N, nb, T // tk)[0]


def _wgrad_wide(name, lhs, rhs, n_split, kb, tk, comm=None):
    T, M = lhs.shape
    N = rhs.shape[1]
    slab = N // n_split
    nk = T // tk

    def body(l_ref, r_ref, o_ref, acc_ref):
        k = pl.program_id(1)

        @pl.when(k == 0)
        def _():
            acc_ref[...] = jnp.zeros_like(acc_ref)

        acc_ref[...] += lax.dot_general(l_ref[...], r_ref[...], TN_DIMS, preferred_element_type=F32)

        @pl.when(k == nk - 1)
        def _():
            if kb == 0:
                o_ref[...] = acc_ref[...].astype(BF16)
            for b in range(kb):
                o_ref[b] = acc_ref[:, b * (slab // kb):(b + 1) * (slab // kb)].astype(BF16)

    if kb == 0:
        out_spec, out_shape = pl.BlockSpec((M, slab), lambda j, k: (0, j)), (M, N)
    else:
        out_spec, out_shape = pl.BlockSpec((kb, M, slab // kb), lambda j, k: (j, 0, 0)), (n_split * kb, M, slab // kb)
    return _host_call(
        body, name=name, grid=(n_split, nk), comm=comm, args=(lhs, rhs),
        in_specs=[pl.BlockSpec((tk, M), lambda j, k: (k, 0)), pl.BlockSpec((tk, slab), lambda j, k: (k, j))],
        out_specs=[out_spec], out_shape=[jax.ShapeDtypeStruct(out_shape, BF16)],
        scratch_shapes=[pltpu.VMEM((M, slab), F32)])


def _wgrad_out(mixed, dmo, nb=NDEV, tk=1024):
    D = mixed.shape[1]
    res, _ = _wgrad_wide("wgrad_out", mixed, dmo, 2, 0, tk)
    return res[0].reshape(nb, D // nb, D)


def _wgrad_in(h, dproj, comm, nb=NDEV, tk=1024):
    res, cres = _wgrad_wide("wgrad_in", h, dproj, nb // 2, 2, tk, comm=comm)
    return res[0], cres


def _mixer_bwd(dx1, mo, proj, cw, pw, ps, wout, g2, tm=256):
    T, D = dx1.shape
    P = proj.shape[1]
    nt = T // tm
    n_ext = tm + HALO
    hb = tm // HALO

    def body(dx1_ref, mo_ref, proj_ref, hc_ref, hu_ref, hv_ref, cw_ref, pw_ref, ps_ref, wout_ref, g2_ref,
             dmo_ref, dproj_ref, dg2_ref, dcw_ref, dps_ref, dpw_ref, dmix_ref, dconv_carry, q_carry):
        i = pl.program_id(0)
        tile = nt - 1 - i

        @pl.when(i == 0)
        def _():
            dconv_carry[...] = jnp.zeros_like(dconv_carry)
            q_carry[...] = jnp.zeros_like(q_carry)
            dg2_ref[...] = jnp.zeros_like(dg2_ref)
            dcw_ref[...] = jnp.zeros_like(dcw_ref)
            dps_ref[...] = jnp.zeros_like(dps_ref)
            dpw_ref[...] = jnp.zeros_like(dpw_ref)

        mov = mo_ref[...]
        r2 = _rsq(mov)
        n2 = mov * r2
        dx1v = dx1_ref[...]
        dg2_ref[...] += jnp.sum(dx1v * n2, axis=0, keepdims=True)
        dmo = _norm_bwd(dx1v * g2_ref[...], n2, r2).astype(BF16)
        dmo_ref[...] = dmo
        dmix_ref[...] = lax.dot_general(dmo, wout_ref[...], NT_DIMS, preferred_element_type=F32)

        has_prev = (tile > 0).astype(F32)

        for h in range(CONV_HEADS):
            lo = h * HEAD_DIM
            sl = slice(lo, lo + HEAD_DIM)
            gate_b = proj_ref[:, lo:lo + HEAD_DIM].astype(F32)
            gate_c = proj_ref[:, CONV_WIDTH + lo:CONV_WIDTH + lo + HEAD_DIM].astype(F32)
            uu = proj_ref[:, 2 * CONV_WIDTH + lo:2 * CONV_WIDTH + lo + HEAD_DIM].astype(F32)
            cu = gate_c * uu
            ext = jnp.concatenate([hc_ref[:, sl].astype(F32) * hu_ref[:, sl].astype(F32) * has_prev, cu], axis=0)
            c1 = pltpu.roll(ext, 1, 0)[HALO:]
            c2 = pltpu.roll(ext, 2, 0)[HALO:]
            w0, w1, w2 = cw_ref[h, 0:1, :], cw_ref[h, 1:2, :], cw_ref[h, 2:3, :]
            conv = w2 * cu + w1 * c1 + w0 * c2
            ya = gate_b * conv
            ra = _rsq(ya)
            dya = _norm_bwd(dmix_ref[:, sl], ya * ra, ra)
            dconv = dya * gate_b
            dcw_ref[h, 0:1, :] += jnp.sum(dconv * c2, axis=0, keepdims=True)
            dcw_ref[h, 1:2, :] += jnp.sum(dconv * c1, axis=0, keepdims=True)
            dcw_ref[h, 2:3, :] += jnp.sum(dconv * cu, axis=0, keepdims=True)
            extd = jnp.concatenate([dconv, dconv_carry[:, sl]], axis=0)
            d1 = pltpu.roll(extd, n_ext - 1, 0)[:tm]
            d2 = pltpu.roll(extd, n_ext - 2, 0)[:tm]
            dcu = w2 * dconv + w1 * d1 + w0 * d2
            dconv_carry[:, sl] = dconv[:HALO]
            dproj_ref[:, lo:lo + HEAD_DIM] = (dya * conv).astype(BF16)
            dproj_ref[:, CONV_WIDTH + lo:CONV_WIDTH + lo + HEAD_DIM] = (dcu * uu).astype(BF16)
            dproj_ref[:, 2 * CONV_WIDTH + lo:2 * CONV_WIDTH + lo + HEAD_DIM] = (dcu * gate_c).astype(BF16)

        for gi, w in enumerate(POOL_WINDOWS):
            lo = gi * POOL_GROUP_DIM
            sl = slice(lo, lo + POOL_GROUP_DIM)
            v = proj_ref[:, 3 * CONV_WIDTH + lo:3 * CONV_WIDTH + lo + POOL_GROUP_DIM].astype(F32)
            inv = _inv_count(tile * tm, tm, w)
            ext = jnp.concatenate([hv_ref[:, sl].astype(F32) * has_prev, v], axis=0)
            pooled = (_window_sum(ext, w, True)[HALO:] * inv - v).astype(BF16)
            y = jnp.dot(pooled, pw_ref[gi], preferred_element_type=F32)
            rp = _rsq(y)
            nb_ = y * rp
            dyb = dmix_ref[:, CONV_WIDTH + lo:CONV_WIDTH + lo + POOL_GROUP_DIM]
            dps_ref[:, sl] += jnp.sum(dyb * nb_, axis=0, keepdims=True)
            dy = _norm_bwd(dyb * ps_ref[:, sl], nb_, rp).astype(BF16)
            dpw_ref[gi] += lax.dot_general(pooled, dy, TN_DIMS, preferred_element_type=F32)
            dpooled = lax.dot_general(dy, pw_ref[gi], NT_DIMS, preferred_element_type=F32)
            q = dpooled * inv
            extq = jnp.concatenate([q, q_carry[:, sl]], axis=0)
            dv = _window_sum(extq, w, False)[:tm] - dpooled
            q_carry[:, sl] = q[:HALO]
            dproj_ref[:, 3 * CONV_WIDTH + lo:3 * CONV_WIDTH + lo + POOL_GROUP_DIM] = dv.astype(BF16)

    rev = lambda n: pl.BlockSpec((tm, n), lambda i: (nt - 1 - i, 0))

    def halo(col):
        return pl.BlockSpec((HALO, CONV_WIDTH), lambda i: (jnp.maximum((nt - 1 - i) * hb - 1, 0), col))

    return _host_call(
        body, name="mixer_bwd", grid=(nt,), args=(dx1, mo, proj, proj, proj, proj, cw, pw, ps, wout, g2),
        in_specs=[rev(D), rev(D), rev(P), halo(1), halo(2), halo(3), _whole(cw.shape), _whole(pw.shape),
                  _whole(ps.shape), _whole(wout.shape), _whole(g2.shape)],
        out_specs=[rev(D), rev(P), pl.BlockSpec((1, D), lambda i: (0, 0)),
                   pl.BlockSpec(cw.shape, lambda i: (0, 0, 0)), pl.BlockSpec(ps.shape, lambda i: (0, 0)),
                   pl.BlockSpec(pw.shape, lambda i: (0, 0, 0))],
        out_shape=[jax.ShapeDtypeStruct((T, D), BF16), jax.ShapeDtypeStruct((T, P), BF16),
                   jax.ShapeDtypeStruct((1, D), F32), jax.ShapeDtypeStruct(cw.shape, F32),
                   jax.ShapeDtypeStruct(ps.shape, F32), jax.ShapeDtypeStruct(pw.shape, F32)],
        scratch_shapes=[pltpu.VMEM((tm, D), F32), pltpu.VMEM((HALO, CONV_WIDTH), F32),
                        pltpu.VMEM((HALO, CONV_WIDTH), F32)])[0]


def _inproj_bwd(dproj, win, x, dx1, g1, comm, tm=256):
    T, D = x.shape
    nblk, _, bn = win.shape

    def body(dp_ref, w_ref, x_ref, dx1_ref, g1_ref, gx_ref, dg1_ref):
        @pl.when(pl.program_id(0) == 0)
        def _():
            dg1_ref[...] = jnp.zeros_like(dg1_ref)

        dh = lax.dot_general(dp_ref[:, 0:bn], w_ref[0], NT_DIMS, preferred_element_type=F32)
        for k in range(1, nblk):
            dh = dh + lax.dot_general(dp_ref[:, k * bn:(k + 1) * bn], w_ref[k], NT_DIMS,
                                      preferred_element_type=F32)
        xv = x_ref[...]
        r = _rsq(xv)
        n = xv * r
        dg1_ref[...] += jnp.sum(dh * n, axis=0, keepdims=True)
        gx_ref[...] = dx1_ref[...] + _norm_bwd(dh * g1_ref[...], n, r)

    row = pl.BlockSpec((tm, D), lambda i: (i, 0))
    vec = pl.BlockSpec((1, D), lambda i: (0, 0))
    return _host_call(
        body, name="inproj_bwd", grid=(T // tm,), comm=comm, args=(dproj, win, x, dx1, g1),
        in_specs=[pl.BlockSpec((tm, nblk * bn), lambda i: (i, 0)), _whole(win.shape), row, row, vec],
        out_specs=[row, vec],
        out_shape=[jax.ShapeDtypeStruct((T, D), F32), jax.ShapeDtypeStruct((1, D), F32)])


def _adamw(w, g, m, v):
    m = ADAM_B1 * m + (1.0 - ADAM_B1) * g
    v = ADAM_B2 * v + (1.0 - ADAM_B2) * jnp.square(g)
    m_hat = m / (1.0 - ADAM_B1 ** ADAM_STEP)
    v_hat = v / (1.0 - ADAM_B2 ** ADAM_STEP)
    delta = -ADAM_LR * (m_hat / (jnp.sqrt(v_hat) + ADAM_EPS) + ADAM_WD * w)
    return delta, m, v


def _sum_adamw(parts, w, m, v, name, tr):
    r, cd = w.shape

    def body(p_ref, w_ref, m_ref, v_ref, g_ref, d_ref, mo_ref, vo_ref):
        g = p_ref[0].astype(F32)
        for k in range(1, NDEV):
            g = g + p_ref[k].astype(F32)
        g_ref[...] = g
        d_ref[...], mo_ref[...], vo_ref[...] = _adamw(w_ref[...], g, m_ref[...], v_ref[...])

    blk = pl.BlockSpec((tr, cd), lambda i: (i, 0))
    shp = jax.ShapeDtypeStruct((r, cd), F32)
    return pl.pallas_call(
        body, name=name, grid=(r // tr,),
        in_specs=[pl.BlockSpec((NDEV, tr, cd), lambda i: (0, i, 0)), blk, blk, blk],
        out_specs=[blk] * 4, out_shape=[shp] * 4,
        compiler_params=_params("arbitrary"),
    )(parts, w, m, v)


def _small_reduce_adamw(loss_part, vec_grads, dps, dcw, dpw, vec_state, ps_state, cw_state, pw_state):
    D = vec_grads[0].shape[1]
    pw_rows = pw_state[0].shape[1]
    states = list(vec_state) + [ps_state, cw_state, pw_state]
    n_in = 1 + 4 + 3 + 3 * len(states)
    n_out = 1 + 4 * len(states)

    def body(*refs):
        loss_ref, dg = refs[0], refs[1:5]
        dps_ref, dcw_ref, dpw_ref = refs[5:8]
        st = refs[8:n_in]
        loss_out, outs = refs[n_in], refs[n_in + 1:n_in + n_out]
        pack, gat, cbuf, pbuf, send_sems, recv_sems, local_sems = refs[n_in + n_out:]
        x, y, c = _coords()
        me = _device_index((x, y, c))

        pack[...] = jnp.zeros_like(pack)
        for k in range(4):
            pack[k:k + 1, :] = dg[k][...]
        pack[4:5, 0:dps_ref.shape[1]] = dps_ref[...]
        pack[5:6, 0:loss_ref.shape[1]] = loss_ref[...]

        def pw_slice(i):
            return dpw_ref.at[:, pl.ds(i * pw_rows, pw_rows), :]

        mine = [pltpu.make_async_copy(pack, gat.at[me], local_sems.at[0]),
                pltpu.make_async_copy(dcw_ref.at[me], cbuf.at[me], local_sems.at[1]),
                pltpu.make_async_copy(pw_slice(me), pbuf.at[me], local_sems.at[2])]
        for cp in mine:
            cp.start()
        sends, recvs = [], []
        for mask in range(1, NDEV):
            peer = (1 - x if mask & 4 else x, 1 - y if mask & 2 else y, 1 - c if mask & 1 else c)
            p = _device_index(peer)
            for k, (src, buf) in enumerate(((pack, gat), (dcw_ref.at[p], cbuf), (pw_slice(p), pbuf))):
                kw = dict(send_sem=send_sems.at[mask, k], recv_sem=recv_sems.at[mask, k],
                          device_id=peer, device_id_type=MESH)
                sends.append(pltpu.make_async_remote_copy(src_ref=src, dst_ref=buf.at[me], **kw))
                recvs.append(pltpu.make_async_remote_copy(src_ref=src, dst_ref=buf.at[p], **kw))
                sends[-1].start()
        for cp in recvs:
            cp.wait_recv()
        for cp in sends:
            cp.wait_send()
        for cp in mine:
            cp.wait()

        def slot_sum(buf):
            s = buf[0]
            for k in range(1, NDEV):
                s = s + buf[k]
            return s

        vec = slot_sum(gat)
        loss_out[...] = vec[5:6, 0:loss_ref.shape[1]]
        grads = [vec[k:k + 1, :] for k in range(4)] + [vec[4:5, 0:dps_ref.shape[1]], slot_sum(cbuf), slot_sum(pbuf)]
        for k, g in enumerate(grads):
            w_ref, m_ref, v_ref = st[3 * k:3 * k + 3]
            outs[4 * k][...] = g
            outs[4 * k + 1][...], outs[4 * k + 2][...], outs[4 * k + 3][...] = _adamw(
                w_ref[...], g, m_ref[...], v_ref[...])

    flat_state = [a for s in states for a in s]
    out_shape = [jax.ShapeDtypeStruct(loss_part.shape, F32)]
    out_shape += [jax.ShapeDtypeStruct(s[0].shape, F32) for s in states for _ in range(4)]
    return pl.pallas_call(
        body, name="small_reduce_adamw",
        in_specs=[VMEM_SPEC] * n_in, out_specs=[VMEM_SPEC] * n_out, out_shape=out_shape,
        scratch_shapes=[pltpu.VMEM((NDEV, D), F32), pltpu.VMEM((NDEV, NDEV, D), F32),
                        pltpu.VMEM((NDEV,) + cw_state[0].shape, F32), pltpu.VMEM((NDEV,) + pw_state[0].shape, F32),
                        pltpu.SemaphoreType.DMA((NDEV, 3)), pltpu.SemaphoreType.DMA((NDEV, 3)),
                        pltpu.SemaphoreType.DMA((3,))],
        compiler_params=pltpu.CompilerParams(vmem_limit_bytes=VMEM_LIMIT),
    )(loss_part, *vec_grads, dps, dcw, dpw, *flat_state)


ROW_TILE = dict(w_in=512, w_gate=256, w_up=256, w_down=176, w_out=128)
FORWARD_STEP = dict(inproj=56, mixer_fwd=26, ffn_up=32)


def kernel(x, ln_mix_pre, w_in, conv_w, pool_w, pool_scale, w_out, ln_mix_post, ln_ffn_pre, w_gate, w_up, w_down, ln_ffn_post, loss_target, m_ln_mix_pre, m_w_in, m_conv_w, m_pool_w, m_pool_scale, m_w_out, m_ln_mix_post, m_ln_ffn_pre, m_w_gate, m_w_up, m_w_down, m_ln_ffn_post, v_ln_mix_pre, v_w_in, v_conv_w, v_pool_w, v_pool_scale, v_w_out, v_ln_mix_post, v_ln_ffn_pre, v_w_gate, v_w_up, v_w_down, v_ln_ffn_post):
    D = x.shape[2]
    xs, tgt = x[0], loss_target[0]
    win, wg, wu, wd, wout, pw, cw = _cast_gather_first(
        [w_in[0], w_gate[0], w_up[0], w_down[0], w_out[0], pool_w[0], conv_w[0]])

    (proj, h), (wout, wg) = _inproj(xs, ln_mix_pre, win, _gather_comm([wout, wg], FORWARD_STEP["inproj"]))
    wout2 = wout.reshape(D, D)
    (x1, hf, mixed, mo), (wu,) = _mixer_fwd(proj, xs, cw, pw, pool_scale, wout2, ln_mix_post, ln_ffn_pre,
                                            _gather_comm([wu], FORWARD_STEP["mixer_fwd"]))
    (g, u, a), (wd,) = _ffn_up(hf, wg, wu, _gather_comm([wd], FORWARD_STEP["ffn_up"]))
    dy, dff, loss, dg4 = _ffn_down_loss(a, wd, x1, tgt, ln_ffn_post)

    dwd = _wgrad_down(a, dff)
    (dg, du), (dwd_parts,) = _ffn_bwd_act(dff, wd, g, u, _scatter_comm([dwd]))
    dwg, dwu = _wgrad_gate_up(hf, dg, du)
    (dx1, dg3), (dwg_parts, dwu_parts) = _ffn_bwd_in(dg, du, wg, wu, x1, dy, ln_ffn_pre,
                                                     [_scatter_comm([dwg]), _scatter_comm([dwu])])
    dmo, dproj, dg2, dcw, dps, dpw = _mixer_bwd(dx1, mo, proj, cw, pw, pool_scale, wout2, ln_mix_post)
    dwout = _wgrad_out(mixed, dmo)
    dwin, (dwout_parts,) = _wgrad_in(h, dproj, _scatter_comm([dwout]))
    (gx, dg1), (dwin_parts,) = _inproj_bwd(dproj, win, xs, dx1, ln_mix_pre, _scatter_comm([dwin]))

    res = {}
    for k, parts, w, m, v in (("w_down", dwd_parts, w_down, m_w_down, v_w_down),
                              ("w_gate", dwg_parts, w_gate, m_w_gate, v_w_gate),
                              ("w_up", dwu_parts, w_up, m_w_up, v_w_up),
                              ("w_out", dwout_parts, w_out, m_w_out, v_w_out),
                              ("w_in", dwin_parts, w_in, m_w_in, v_w_in)):
        res[k] = [o.reshape(w.shape) for o in _sum_adamw(parts, w[0], m[0], v[0], "sum_adamw_" + k, ROW_TILE[k])]

    loss_sum, *small = _small_reduce_adamw(
        loss, [dg1, dg2, dg3, dg4], dps, dcw, dpw,
        [(ln_mix_pre, m_ln_mix_pre, v_ln_mix_pre), (ln_mix_post, m_ln_mix_post, v_ln_mix_post),
         (ln_ffn_pre, m_ln_ffn_pre, v_ln_ffn_pre), (ln_ffn_post, m_ln_ffn_post, v_ln_ffn_post)],
        (pool_scale, m_pool_scale, v_pool_scale), (conv_w[0], m_conv_w[0], v_conv_w[0]),
        (pool_w[0], m_pool_w[0], v_pool_w[0]))
    small_names = ["ln_mix_pre", "ln_mix_post", "ln_ffn_pre", "ln_ffn_post", "pool_scale", "conv_w", "pool_w"]
    shapes = dict(conv_w=conv_w.shape, pool_w=pool_w.shape)
    for i, k in enumerate(small_names):
        res[k] = [o.reshape(shapes[k]) if k in shapes else o for o in small[4 * i:4 * i + 4]]

    order = ["ln_mix_pre", "w_in", "conv_w", "pool_w", "pool_scale", "w_out", "ln_mix_post", "ln_ffn_pre",
             "w_gate", "w_up", "w_down", "ln_ffn_post"]
    return (loss_sum[0, 0], gx[None], *[res[k][0] for k in order], *[res[k][1] for k in order],
            *[res[k][2] for k in order], *[res[k][3] for k in order])
```

```python
import functools
from typing import Any, NamedTuple

import jax
import jax.numpy as jnp
from jax import lax
from jax.experimental import pallas as pl
from jax.experimental.pallas import tpu as pltpu

EPS = 1e-6
NDEV = 8
CONV_HEADS = 8
HEAD_DIM = 128
CONV_WIDTH = CONV_HEADS * HEAD_DIM
POOL_WINDOWS = (2, 4, 8, 16)
POOL_GROUP_DIM = 256
HALO = 16

ADAM_LR = 0.001
ADAM_B1 = 0.9
ADAM_B2 = 0.999
ADAM_EPS = 1e-08
ADAM_WD = 0.01
ADAM_STEP = 10

F32 = jnp.float32
BF16 = jnp.bfloat16
VMEM_LIMIT = 58 * 1024 * 1024
MESH = pl.DeviceIdType.MESH
HBM_SPEC = pl.BlockSpec(memory_space=pl.ANY)
VMEM_SPEC = pl.BlockSpec(memory_space=pltpu.VMEM)

NT_DIMS = (((1,), (1,)), ((), ()))
TN_DIMS = (((0,), (0,)), ((), ()))


def _params(*sem):
    return pltpu.CompilerParams(dimension_semantics=sem, vmem_limit_bytes=VMEM_LIMIT)


def _rsq(v):
    return lax.rsqrt(jnp.mean(v * v, axis=-1, keepdims=True) + EPS)


def _norm_bwd(dn, n, r):
    return r * (dn - n * jnp.mean(dn * n, axis=-1, keepdims=True))


def _whole(shape):
    nd = len(shape)
    return pl.BlockSpec(shape, lambda *_: (0,) * nd, pipeline_mode=pl.Buffered(1))


def _inv_count(t0, tm, w):
    t = t0 + lax.broadcasted_iota(jnp.int32, (tm, 1), 0)
    return 1.0 / jnp.minimum(t + 1, w).astype(F32)


def _window_sum(ext, w, back):
    n = ext.shape[0]
    s, shift = ext, 1
    while shift < w:
        s = s + pltpu.roll(s, shift if back else n - shift, 0)
        shift *= 2
    return s


class _Comm(NamedTuple):
    arrays: Any
    out_shape: Any
    aliases: Any
    scratch: Any
    hooks: Any


def _coords():
    return lax.axis_index("x"), lax.axis_index("y"), lax.axis_index("c")


def _other_chips(x, y):
    return [(1 - x, y), (x, 1 - y), (1 - x, 1 - y)]


def _device_index(dev):
    return 4 * dev[0] + 2 * dev[1] + dev[2]


def _host_call(body, *, name, grid, in_specs, out_specs, out_shape, args, scratch_shapes=(), comm=None):
    sem = ("arbitrary",) * len(grid)
    in_specs, out_specs, out_shape, scratch_shapes = list(in_specs), list(out_specs), list(out_shape), list(scratch_shapes)
    if comm is None:
        res = pl.pallas_call(body, name=name, grid=grid, in_specs=in_specs, out_specs=out_specs, out_shape=out_shape,
                             scratch_shapes=scratch_shapes, compiler_params=_params(*sem))(*args)
        return res, []
    n_in, n_out, n_scr = len(in_specs), len(out_specs), len(scratch_shapes)
    n_cin, n_cout = len(comm.arrays), len(comm.out_shape)
    total = functools.reduce(lambda a, b: a * b, grid)

    def wrapped(*refs):
        ins, cin = refs[:n_in], refs[n_in:n_in + n_cin]
        o0 = n_in + n_cin
        outs, cout = refs[o0:o0 + n_out], refs[o0 + n_out:o0 + n_out + n_cout]
        s0 = o0 + n_out + n_cout
        scr, sems = refs[s0:s0 + n_scr], refs[s0 + n_scr:]
        step = pl.program_id(0)
        for d in range(1, len(grid)):
            step = step * grid[d] + pl.program_id(d)
        for when, before, fn in comm.hooks:
            if before:
                pl.when(step == when % total)(functools.partial(fn, cin, cout, sems))
        body(*ins, *outs, *scr)
        for when, before, fn in comm.hooks:
            if not before:
                pl.when(step == when % total)(functools.partial(fn, cin, cout, sems))

    res = pl.pallas_call(
        wrapped, name=name, grid=grid,
        in_specs=in_specs + [HBM_SPEC] * n_cin, out_specs=out_specs + [HBM_SPEC] * n_cout,
        out_shape=out_shape + list(comm.out_shape), scratch_shapes=scratch_shapes + list(comm.scratch),
        input_output_aliases={n_in + i: n_out + o for i, o in comm.aliases.items()},
        compiler_params=_params(*sem),
    )(*args, *comm.arrays)
    return res[:n_out], res[n_out:]


def _gather_steps(n, view, own_src, send_sems, recv_sems):
    x, y, c = _coords()
    me, sibling = (x, y, c), (x, y, 1 - c)
    chips = _other_chips(x, y)

    def copy(a, k, block, to, src=None):
        return pltpu.make_async_remote_copy(
            src_ref=view(a, block) if src is None else src, dst_ref=view(a, block),
            send_sem=send_sems.at[a, k], recv_sem=recv_sems.at[a, k], device_id=to, device_id_type=MESH)

    def first_copies():
        cps = []
        for a in range(n):
            cps.append(copy(a, 0, me, sibling, src=own_src(a)))
            cps += [copy(a, 1 + j, me, (*chip, c), src=own_src(a)) for j, chip in enumerate(chips)]
        return cps

    def passed_copies():
        return [copy(a, 4 + j, (*chip, c), sibling) for j, chip in enumerate(chips) for a in range(n)]

    def first():
        for cp in first_copies():
            cp.start()

    def forward():
        for j, chip in enumerate(chips):
            for a in range(n):
                copy(a, 1 + j, (*chip, c), me).wait_recv()
                copy(a, 4 + j, (*chip, c), sibling).start()

    def finish():
        for a in range(n):
            copy(a, 0, sibling, me).wait_recv()
            for j, chip in enumerate(chips):
                copy(a, 4 + j, (*chip, 1 - c), me).wait_recv()
        for cp in first_copies() + passed_copies():
            cp.wait_send()

    return first, forward, finish


def _gather_comm(arrays, forward_step):
    n = len(arrays)

    def steps(cout, sems):
        view = lambda a, dev: cout[a].at[_device_index(dev)]
        return _gather_steps(n, view, lambda a: view(a, _coords()), sems[0], sems[1])

    hooks = [(0, True, lambda cin, cout, sems: steps(cout, sems)[0]()),
             (forward_step, True, lambda cin, cout, sems: steps(cout, sems)[1]()),
             (-1, False, lambda cin, cout, sems: steps(cout, sems)[2]())]
    return _Comm(list(arrays), [jax.ShapeDtypeStruct(a.shape, a.dtype) for a in arrays], {i: i for i in range(n)},
                 [pltpu.SemaphoreType.DMA((n, 7)), pltpu.SemaphoreType.DMA((n, 7))], hooks)


def _scatter_comm(items):
    n = len(items)
    arrays, aliases, grad_at = [], {}, []
    for k, (grad, slots, _, _) in enumerate(items):
        grad_at.append(len(arrays))
        arrays.append(grad)
        if slots is not None:
            aliases[len(arrays)] = k
            arrays.append(slots)

    def copies(cin, cout, sems):
        send_sems, recv_sems, local_sems = sems
        x, y, c = _coords()
        me = _device_index((x, y, c))
        src = lambda a, p: cin[grad_at[a]].at[p, pl.ds(items[a][2], items[a][3]), :]
        dst = lambda a, p: cout[a].at[p, pl.ds(items[a][2], items[a][3]), :]
        mine = [pltpu.make_async_copy(src(a, me), dst(a, me), local_sems.at[a]) for a in range(n)]
        sends, recvs = [], []
        for a in range(n):
            for mask in range(1, NDEV):
                peer = (1 - x if mask & 4 else x, 1 - y if mask & 2 else y, 1 - c if mask & 1 else c)
                p = _device_index(peer)
                kw = dict(send_sem=send_sems.at[a, mask - 1], recv_sem=recv_sems.at[a, mask - 1],
                          device_id=peer, device_id_type=MESH)
                sends.append(pltpu.make_async_remote_copy(src_ref=src(a, p), dst_ref=dst(a, me), **kw))
                recvs.append(pltpu.make_async_remote_copy(src_ref=src(a, p), dst_ref=dst(a, p), **kw))
        return mine, sends, recvs

    def start(cin, cout, sems):
        mine, sends, _ = copies(cin, cout, sems)
        for cp in mine + sends:
            cp.start()

    def finish(cin, cout, sems):
        mine, sends, recvs = copies(cin, cout, sems)
        for cp in recvs:
            cp.wait_recv()
        for cp in sends:
            cp.wait_send()
        for cp in mine:
            cp.wait()

    return _Comm(arrays, [jax.ShapeDtypeStruct(it[0].shape, it[0].dtype) for it in items], aliases,
                 [pltpu.SemaphoreType.DMA((n, NDEV - 1)), pltpu.SemaphoreType.DMA((n, NDEV - 1)),
                  pltpu.SemaphoreType.DMA((n,))],
                 [(0, True, start), (-1, False, finish)])


NOW_ITEMS = (0, 5, 6)
POOL_ITEM = 5


def _cast_gather_first(shards):
    n = len(shards)
    dtypes = [BF16] * 6 + [F32]
    out_shapes = [(NDEV,) + s.shape for s in shards]
    g, rows, cols = shards[POOL_ITEM].shape
    out_shapes[POOL_ITEM] = (g, rows * NDEV, cols)

    later = [a for a in range(n) if a not in NOW_ITEMS]

    def body(*refs):
        ins, outs, raw, stage = refs[:n], refs[n:2 * n], refs[2 * n:3 * n], refs[3 * n:4 * n]
        send_sems, recv_sems, local_sems, load_sems = refs[4 * n:]

        def view(a, dev):
            i = _device_index(dev)
            if a == POOL_ITEM:
                return outs[a].at[:, pl.ds(i * rows, rows), :]
            return outs[a].at[i]

        loads = [pltpu.make_async_copy(ins[a], raw[a], load_sems.at[a]) for a in range(n)]
        mine = [pltpu.make_async_copy(stage[a], view(a, _coords()), local_sems.at[a]) for a in range(n)]
        for a in list(NOW_ITEMS) + later:
            loads[a].start()
        first, forward, finish = _gather_steps(
            len(NOW_ITEMS), lambda k, dev: view(NOW_ITEMS[k], dev), lambda k: stage[NOW_ITEMS[k]], send_sems, recv_sems)
        for a in list(NOW_ITEMS) + later:
            loads[a].wait()
            stage[a][...] = raw[a][...].astype(dtypes[a])
            mine[a].start()
            if a == NOW_ITEMS[-1]:
                first()
        forward()
        finish()
        for cp in mine:
            cp.wait()

    return pl.pallas_call(
        body, name="cast_gather_first",
        in_specs=[HBM_SPEC] * n, out_specs=[HBM_SPEC] * n,
        out_shape=[jax.ShapeDtypeStruct(s, d) for s, d in zip(out_shapes, dtypes)],
        scratch_shapes=[pltpu.VMEM(s.shape, s.dtype) for s in shards]
        + [pltpu.VMEM(s.shape, d) for s, d in zip(shards, dtypes)]
        + [pltpu.SemaphoreType.DMA((len(NOW_ITEMS), 7)), pltpu.SemaphoreType.DMA((len(NOW_ITEMS), 7)),
           pltpu.SemaphoreType.DMA((n,)), pltpu.SemaphoreType.DMA((n,))],
        compiler_params=pltpu.CompilerParams(vmem_limit_bytes=VMEM_LIMIT),
    )(*shards)


def _inproj(x, g1, win, comm, tm=1024):
    T, D = x.shape
    nb, _, bn = win.shape

    def body(x_ref, g_ref, w_ref, proj_ref, h_ref):
        @pl.when(pl.program_id(1) == 0)
        def _():
            xv = x_ref[...]
            h_ref[...] = (xv * _rsq(xv) * g_ref[...]).astype(BF16)

        proj_ref[...] = jnp.dot(h_ref[...], w_ref[0], preferred_element_type=F32).astype(BF16)

    return _host_call(
        body, name="inproj", grid=(T // tm, nb), comm=comm, args=(x, g1, win),
        in_specs=[pl.BlockSpec((tm, D), lambda i, j: (i, 0)),
                  pl.BlockSpec((1, D), lambda i, j: (0, 0)),
                  pl.BlockSpec((1, D, bn), lambda i, j: (j, 0, 0))],
        out_specs=[pl.BlockSpec((tm, bn), lambda i, j: (i, j)),
                   pl.BlockSpec((tm, D), lambda i, j: (i, 0))],
        out_shape=[jax.ShapeDtypeStruct((T, nb * bn), BF16), jax.ShapeDtypeStruct((T, D), BF16)])


def _mixer_fwd(proj, x, cw, pw, ps, wout, g2, g3, comm, tm=256):
    T, D = x.shape
    P = proj.shape[1]

    def body(proj_ref, x_ref, cw_ref, pw_ref, ps_ref, wout_ref, g2_ref, g3_ref,
             x1_ref, hf_ref, mixed_ref, mo_ref, cu_carry, v_carry):
        i = pl.program_id(0)

        @pl.when(i == 0)
        def _():
            cu_carry[...] = jnp.zeros_like(cu_carry)
            v_carry[...] = jnp.zeros_like(v_carry)

        for h in range(CONV_HEADS):
            lo = h * HEAD_DIM
            gate_b = proj_ref[:, lo:lo + HEAD_DIM].astype(F32)
            cu = proj_ref[:, CONV_WIDTH + lo:CONV_WIDTH + lo + HEAD_DIM].astype(F32) * \
                proj_ref[:, 2 * CONV_WIDTH + lo:2 * CONV_WIDTH + lo + HEAD_DIM].astype(F32)
            ext = jnp.concatenate([cu_carry[:, lo:lo + HEAD_DIM], cu], axis=0)
            c1 = pltpu.roll(ext, 1, 0)[HALO:]
            c2 = pltpu.roll(ext, 2, 0)[HALO:]
            ya = gate_b * (cw_ref[h, 2:3, :] * cu + cw_ref[h, 1:2, :] * c1 + cw_ref[h, 0:1, :] * c2)
            mixed_ref[:, lo:lo + HEAD_DIM] = (ya * _rsq(ya)).astype(BF16)
            cu_carry[:, lo:lo + HEAD_DIM] = cu[tm - HALO:]

        for gi, w in enumerate(POOL_WINDOWS):
            lo = gi * POOL_GROUP_DIM
            v = proj_ref[:, 3 * CONV_WIDTH + lo:3 * CONV_WIDTH + lo + POOL_GROUP_DIM].astype(F32)
            ext = jnp.concatenate([v_carry[:, lo:lo + POOL_GROUP_DIM], v], axis=0)
            pooled = _window_sum(ext, w, True)[HALO:] * _inv_count(i * tm, tm, w) - v
            y = jnp.dot(pooled.astype(BF16), pw_ref[gi], preferred_element_type=F32)
            yb = y * _rsq(y) * ps_ref[:, lo:lo + POOL_GROUP_DIM]
            mixed_ref[:, CONV_WIDTH + lo:CONV_WIDTH + lo + POOL_GROUP_DIM] = yb.astype(BF16)
            v_carry[:, lo:lo + POOL_GROUP_DIM] = v[tm - HALO:]

        mo = jnp.dot(mixed_ref[...], wout_ref[...], preferred_element_type=F32)
        mo_ref[...] = mo
        x1 = x_ref[...] + mo * _rsq(mo) * g2_ref[...]
        x1_ref[...] = x1
        hf_ref[...] = (x1 * _rsq(x1) * g3_ref[...]).astype(BF16)

    row = lambda n: pl.BlockSpec((tm, n), lambda i: (i, 0))
    return _host_call(
        body, name="mixer_fwd", grid=(T // tm,), comm=comm, args=(proj, x, cw, pw, ps, wout, g2, g3),
        in_specs=[row(P), row(D), _whole(cw.shape), _whole(pw.shape), _whole(ps.shape),
                  _whole(wout.shape), _whole(g2.shape), _whole(g3.shape)],
        out_specs=[row(D), row(D), row(D), row(D)],
        out_shape=[jax.ShapeDtypeStruct((T, D), F32), jax.ShapeDtypeStruct((T, D), BF16),
                   jax.ShapeDtypeStruct((T, D), BF16), jax.ShapeDtypeStruct((T, D), F32)],
        scratch_shapes=[pltpu.VMEM((HALO, CONV_WIDTH), F32), pltpu.VMEM((HALO, CONV_WIDTH), F32)])


def _ffn_up(hf, wg, wu, comm, tm=1024):
    T, D = hf.shape
    nb, _, bf = wg.shape

    def body(hf_ref, wg_ref, wu_ref, g_ref, u_ref, a_ref):
        hv = hf_ref[...]
        g = jnp.dot(hv, wg_ref[0], preferred_element_type=F32)
        u = jnp.dot(hv, wu_ref[0], preferred_element_type=F32)
        g_ref[0] = g.astype(BF16)
        u_ref[0] = u.astype(BF16)
        a_ref[0] = (g * jax.nn.sigmoid(g) * u).astype(BF16)

    wspec = pl.BlockSpec((1, D, bf), lambda i, j: (j, 0, 0))
    ospec = pl.BlockSpec((1, tm, bf), lambda i, j: (j, i, 0))
    oshape = jax.ShapeDtypeStruct((nb, T, bf), BF16)
    return _host_call(
        body, name="ffn_up", grid=(T // tm, nb), comm=comm, args=(hf, wg, wu),
        in_specs=[pl.BlockSpec((tm, D), lambda i, j: (i, 0)), wspec, wspec],
        out_specs=[ospec, ospec, ospec], out_shape=[oshape, oshape, oshape])


def _ffn_down_loss(a, wd, x1, tgt, g4, tm=256):
    nblk, T, bf = a.shape
    D = x1.shape[1]
    nt = T // tm

    def body(a_ref, wd_ref, x1_ref, tgt_ref, g4_ref, dy_ref, dff_ref, loss_ref, dg4_ref, lacc_ref):
        i = pl.program_id(0)

        @pl.when(i == 0)
        def _():
            lacc_ref[...] = jnp.zeros_like(lacc_ref)
            dg4_ref[...] = jnp.zeros_like(dg4_ref)

        ff = jnp.dot(a_ref[0], wd_ref[0], preferred_element_type=F32)
        for k in range(1, nblk):
            ff = ff + jnp.dot(a_ref[k], wd_ref[k], preferred_element_type=F32)
        r = _rsq(ff)
        n = ff * r
        g4v = g4_ref[...]
        e = x1_ref[...] + n * g4v - tgt_ref[...]
        lacc_ref[...] += jnp.sum(e * e, axis=0, keepdims=True)
        dy = e * (1.0 / D)
        dy_ref[...] = dy
        dg4_ref[...] += jnp.sum(dy * n, axis=0, keepdims=True)
        dff_ref[...] = _norm_bwd(dy * g4v, n, r).astype(BF16)

        @pl.when(i == nt - 1)
        def _():
            loss_ref[...] = jnp.full(loss_ref.shape, (0.5 / D) * jnp.sum(lacc_ref[...]), F32)

    row = pl.BlockSpec((tm, D), lambda i: (i, 0))
    vec = pl.BlockSpec((1, D), lambda i: (0, 0))
    return _host_call(
        body, name="ffn_down_loss", grid=(nt,), args=(a, wd, x1, tgt, g4),
        in_specs=[pl.BlockSpec((nblk, tm, bf), lambda i: (0, i, 0)), _whole(wd.shape), row, row, vec],
        out_specs=[row, row, pl.BlockSpec((1, 128), lambda i: (0, 0)), vec],
        out_shape=[jax.ShapeDtypeStruct((T, D), F32), jax.ShapeDtypeStruct((T, D), BF16),
                   jax.ShapeDtypeStruct((1, 128), F32), jax.ShapeDtypeStruct((1, D), F32)],
        scratch_shapes=[pltpu.VMEM((1, D), F32)])[0]


def _ffn_bwd_act(dff, wd, g, u, comm, tm=256):
    T, D = dff.shape
    nb, bf, _ = wd.shape

    def body(dff_ref, wd_ref, g_ref, u_ref, dg_ref, du_ref):
        dv = dff_ref[...]
        for k in range(nb):
            da = lax.dot_general(dv, wd_ref[k], NT_DIMS, preferred_element_type=F32)
            gv = g_ref[k].astype(F32)
            s = jax.nn.sigmoid(gv)
            du_ref[k] = (da * (gv * s)).astype(BF16)
            dg_ref[k] = (da * u_ref[k].astype(F32) * (s * (1.0 + gv * (1.0 - s)))).astype(BF16)

    blk = pl.BlockSpec((nb, tm, bf), lambda i: (0, i, 0))
    oshape = jax.ShapeDtypeStruct((nb, T, bf), BF16)
    return _host_call(
        body, name="ffn_bwd_act", grid=(T // tm,), comm=comm, args=(dff, wd, g, u),
        in_specs=[pl.BlockSpec((tm, D), lambda i: (i, 0)), _whole(wd.shape), blk, blk],
        out_specs=[blk, blk], out_shape=[oshape, oshape])


def _ffn_bwd_in(dg, du, wg, wu, x1, dy, g3, comm_a, comm_b_of, tm=256):
    nb, T, bf = dg.shape
    D = x1.shape[1]
    hb = nb // 2

    def partial_sum(dg_ref, du_ref, wg_ref, wu_ref):
        s = lax.dot_general(dg_ref[0], wg_ref[0], NT_DIMS, preferred_element_type=F32)
        s = s + lax.dot_general(du_ref[0], wu_ref[0], NT_DIMS, preferred_element_type=F32)
        for k in range(1, hb):
            s = s + lax.dot_general(dg_ref[k], wg_ref[k], NT_DIMS, preferred_element_type=F32)
            s = s + lax.dot_general(du_ref[k], wu_ref[k], NT_DIMS, preferred_element_type=F32)
        return s

    def first(dg_ref, du_ref, wg_ref, wu_ref, part_ref):
        part_ref[...] = partial_sum(dg_ref, du_ref, wg_ref, wu_ref)

    def second(dg_ref, du_ref, wg_ref, wu_ref, part_ref, x1_ref, dy_ref, g3_ref, dx1_ref, dg3_ref):
        @pl.when(pl.program_id(0) == 0)
        def _():
            dg3_ref[...] = jnp.zeros_like(dg3_ref)

        dhf = part_ref[...] + partial_sum(dg_ref, du_ref, wg_ref, wu_ref)
        x1v = x1_ref[...]
        r = _rsq(x1v)
        n = x1v * r
        dg3_ref[...] += jnp.sum(dhf * n, axis=0, keepdims=True)
        dx1_ref[...] = dy_ref[...] + _norm_bwd(dhf * g3_ref[...], n, r)

    row = pl.BlockSpec((tm, D), lambda i: (i, 0))
    vec = pl.BlockSpec((1, D), lambda i: (0, 0))
    rowshape = jax.ShapeDtypeStruct((T, D), F32)

    def specs(half):
        ablk = pl.BlockSpec((hb, tm, bf), lambda i: (half, i, 0))
        wblk = pl.BlockSpec((hb, D, bf), lambda i: (half, 0, 0), pipeline_mode=pl.Buffered(1))
        return [ablk, ablk, wblk, wblk]

    (part,), c0 = _host_call(first, name="ffn_bwd_in_a", grid=(T // tm,), comm=comm_a, args=(dg, du, wg, wu),
                             in_specs=specs(0), out_specs=[row], out_shape=[rowshape])
    res, c1 = _host_call(second, name="ffn_bwd_in_b", grid=(T // tm,), comm=comm_b_of(c0),
                         args=(dg, du, wg, wu, part, x1, dy, g3),
                         in_specs=specs(1) + [row, row, row, vec], out_specs=[row, vec],
                         out_shape=[rowshape, jax.ShapeDtypeStruct((1, D), F32)])
    return res, c1


def _wgrad(name, lhs, rhs, lhs_spec, rhs_spec, n_rhs, M, N, nb, nk, comm=None):
    def body(*refs):
        l_ref, r_refs = refs[0], refs[1:1 + n_rhs]
        o_refs, acc_refs = refs[1 + n_rhs:1 + 2 * n_rhs], refs[1 + 2 * n_rhs:]
        k = pl.program_id(1)
        tile = lambda ref: ref[0] if len(ref.shape) == 3 else ref[...]

        @pl.when(k == 0)
        def _():
            for acc_ref in acc_refs:
                acc_ref[...] = jnp.zeros_like(acc_ref)

        for r_ref, acc_ref in zip(r_refs, acc_refs):
            acc_ref[...] += lax.dot_general(tile(l_ref), tile(r_ref), TN_DIMS, preferred_element_type=F32)

        @pl.when(k == nk - 1)
        def _():
            for o_ref, acc_ref in zip(o_refs, acc_refs):
                o_ref[0] = acc_ref[...].astype(BF16)

    oblk = pl.BlockSpec((1, M, N), lambda j, k: (j, 0, 0))
    oshape = jax.ShapeDtypeStruct((nb, M, N), BF16)
    return _host_call(
        body, name=name, grid=(nb, nk), comm=comm, args=(lhs, *rhs),
        in_specs=[lhs_spec] + [rhs_spec] * n_rhs, out_specs=[oblk] * n_rhs, out_shape=[oshape] * n_rhs,
        scratch_shapes=[pltpu.VMEM((M, N), F32)] * n_rhs)


def _wgrad_down(a, dff, tk=1024):
    nb, T, M = a.shape
    N = dff.shape[1]
    return _wgrad("wgrad_down", a, [dff], pl.BlockSpec((1, tk, M), lambda j, k: (j, k, 0)),
                  pl.BlockSpec((tk, N), lambda j, k: (k, 0)), 1, M, N, nb, T // tk)[0][0]


def _wgrad_gate_up(hf, dg, du, comm, tk=1024):
    T, M = hf.shape
    nb, _, N = dg.shape
    return _wgrad("wgrad_gate_up", hf, [dg, du], pl.BlockSpec((tk, M), lambda j, k: (k, 0)),
                  pl.BlockSpec((1, tk, N), lambda j, k: (j, k, 0)), 2, M, N, nb, T // tk, comm=comm)


def _wgrad_wide(name, lhs, rhs, n_split, kb, tk, comm=None):
    T, M = lhs.shape
    N = rhs.shape[1]
    slab = N // n_split
    nk = T // tk

    def body(l_ref, r_ref, o_ref, acc_ref):
        k = pl.program_id(1)

        @pl.when(k == 0)
        def _():
            acc_ref[...] = jnp.zeros_like(acc_ref)

        acc_ref[...] += lax.dot_general(l_ref[...], r_ref[...], TN_DIMS, preferred_element_type=F32)

        @pl.when(k == nk - 1)
        def _():
            if kb == 0:
                o_ref[...] = acc_ref[...].astype(BF16)
            for b in range(kb):
                o_ref[b] = acc_ref[:, b * (slab // kb):(b + 1) * (slab // kb)].astype(BF16)

    if kb == 0:
        out_spec, out_shape = pl.BlockSpec((M, slab), lambda j, k: (0, j)), (M, N)
    else:
        out_spec, out_shape = pl.BlockSpec((kb, M, slab // kb), lambda j, k: (j, 0, 0)), (n_split * kb, M, slab // kb)
    return _host_call(
        body, name=name, grid=(n_split, nk), comm=comm, args=(lhs, rhs),
        in_specs=[pl.BlockSpec((tk, M), lambda j, k: (k, 0)), pl.BlockSpec((tk, slab), lambda j, k: (k, j))],
        out_specs=[out_spec], out_shape=[jax.ShapeDtypeStruct(out_shape, BF16)],
        scratch_shapes=[pltpu.VMEM((M, slab), F32)])


def _wgrad_out(mixed, dmo, nb=NDEV, tk=1024):
    D = mixed.shape[1]
    res, _ = _wgrad_wide("wgrad_out", mixed, dmo, 2, 0, tk)
    return res[0].reshape(nb, D // nb, D)


def _wgrad_in(h, dproj, comm, nb=NDEV, tk=1024):
    res, cres = _wgrad_wide("wgrad_in", h, dproj, nb // 2, 2, tk, comm=comm)
    return res[0], cres


def _mixer_bwd(dx1, mo, proj, cw, pw, ps, wout, g2, comm, tm=256):
    T, D = dx1.shape
    P = proj.shape[1]
    nt = T // tm
    n_ext = tm + HALO
    hb = tm // HALO

    def body(dx1_ref, mo_ref, proj_ref, hc_ref, hu_ref, hv_ref, cw_ref, pw_ref, ps_ref, wout_ref, g2_ref,
             dmo_ref, dproj_ref, dg2_ref, dcw_ref, dps_ref, dpw_ref, dmix_ref, dconv_carry, q_carry):
        i = pl.program_id(0)
        tile = nt - 1 - i

        @pl.when(i == 0)
        def _():
            dconv_carry[...] = jnp.zeros_like(dconv_carry)
            q_carry[...] = jnp.zeros_like(q_carry)
            dg2_ref[...] = jnp.zeros_like(dg2_ref)
            dcw_ref[...] = jnp.zeros_like(dcw_ref)
            dps_ref[...] = jnp.zeros_like(dps_ref)
            dpw_ref[...] = jnp.zeros_like(dpw_ref)

        mov = mo_ref[...]
        r2 = _rsq(mov)
        n2 = mov * r2
        dx1v = dx1_ref[...]
        dg2_ref[...] += jnp.sum(dx1v * n2, axis=0, keepdims=True)
        dmo = _norm_bwd(dx1v * g2_ref[...], n2, r2).astype(BF16)
        dmo_ref[...] = dmo
        dmix_ref[...] = lax.dot_general(dmo, wout_ref[...], NT_DIMS, preferred_element_type=F32)

        has_prev = (tile > 0).astype(F32)

        for h in range(CONV_HEADS):
            lo = h * HEAD_DIM
            sl = slice(lo, lo + HEAD_DIM)
            gate_b = proj_ref[:, lo:lo + HEAD_DIM].astype(F32)
            gate_c = proj_ref[:, CONV_WIDTH + lo:CONV_WIDTH + lo + HEAD_DIM].astype(F32)
            uu = proj_ref[:, 2 * CONV_WIDTH + lo:2 * CONV_WIDTH + lo + HEAD_DIM].astype(F32)
            cu = gate_c * uu
            ext = jnp.concatenate([hc_ref[:, sl].astype(F32) * hu_ref[:, sl].astype(F32) * has_prev, cu], axis=0)
            c1 = pltpu.roll(ext, 1, 0)[HALO:]
            c2 = pltpu.roll(ext, 2, 0)[HALO:]
            w0, w1, w2 = cw_ref[h, 0:1, :], cw_ref[h, 1:2, :], cw_ref[h, 2:3, :]
            conv = w2 * cu + w1 * c1 + w0 * c2
            ya = gate_b * conv
            ra = _rsq(ya)
            dya = _norm_bwd(dmix_ref[:, sl], ya * ra, ra)
            dconv = dya * gate_b
            dcw_ref[h, 0:1, :] += jnp.sum(dconv * c2, axis=0, keepdims=True)
            dcw_ref[h, 1:2, :] += jnp.sum(dconv * c1, axis=0, keepdims=True)
            dcw_ref[h, 2:3, :] += jnp.sum(dconv * cu, axis=0, keepdims=True)
            extd = jnp.concatenate([dconv, dconv_carry[:, sl]], axis=0)
            d1 = pltpu.roll(extd, n_ext - 1, 0)[:tm]
            d2 = pltpu.roll(extd, n_ext - 2, 0)[:tm]
            dcu = w2 * dconv + w1 * d1 + w0 * d2
            dconv_carry[:, sl] = dconv[:HALO]
            dproj_ref[:, lo:lo + HEAD_DIM] = (dya * conv).astype(BF16)
            dproj_ref[:, CONV_WIDTH + lo:CONV_WIDTH + lo + HEAD_DIM] = (dcu * uu).astype(BF16)
            dproj_ref[:, 2 * CONV_WIDTH + lo:2 * CONV_WIDTH + lo + HEAD_DIM] = (dcu * gate_c).astype(BF16)

        for gi, w in enumerate(POOL_WINDOWS):
            lo = gi * POOL_GROUP_DIM
            sl = slice(lo, lo + POOL_GROUP_DIM)
            v = proj_ref[:, 3 * CONV_WIDTH + lo:3 * CONV_WIDTH + lo + POOL_GROUP_DIM].astype(F32)
            inv = _inv_count(tile * tm, tm, w)
            ext = jnp.concatenate([hv_ref[:, sl].astype(F32) * has_prev, v], axis=0)
            pooled = (_window_sum(ext, w, True)[HALO:] * inv - v).astype(BF16)
            y = jnp.dot(pooled, pw_ref[gi], preferred_element_type=F32)
            rp = _rsq(y)
            nb_ = y * rp
            dyb = dmix_ref[:, CONV_WIDTH + lo:CONV_WIDTH + lo + POOL_GROUP_DIM]
            dps_ref[:, sl] += jnp.sum(dyb * nb_, axis=0, keepdims=True)
            dy = _norm_bwd(dyb * ps_ref[:, sl], nb_, rp).astype(BF16)
            dpw_ref[gi] += lax.dot_general(pooled, dy, TN_DIMS, preferred_element_type=F32)
            dpooled = lax.dot_general(dy, pw_ref[gi], NT_DIMS, preferred_element_type=F32)
            q = dpooled * inv
            extq = jnp.concatenate([q, q_carry[:, sl]], axis=0)
            dv = _window_sum(extq, w, False)[:tm] - dpooled
            q_carry[:, sl] = q[:HALO]
            dproj_ref[:, 3 * CONV_WIDTH + lo:3 * CONV_WIDTH + lo + POOL_GROUP_DIM] = dv.astype(BF16)

    rev = lambda n: pl.BlockSpec((tm, n), lambda i: (nt - 1 - i, 0))

    def halo(col):
        return pl.BlockSpec((HALO, CONV_WIDTH), lambda i: (jnp.maximum((nt - 1 - i) * hb - 1, 0), col))

    return _host_call(
        body, name="mixer_bwd", grid=(nt,), comm=comm, args=(dx1, mo, proj, proj, proj, proj, cw, pw, ps, wout, g2),
        in_specs=[rev(D), rev(D), rev(P), halo(1), halo(2), halo(3), _whole(cw.shape), _whole(pw.shape),
                  _whole(ps.shape), _whole(wout.shape), _whole(g2.shape)],
        out_specs=[rev(D), rev(P), pl.BlockSpec((1, D), lambda i: (0, 0)),
                   pl.BlockSpec(cw.shape, lambda i: (0, 0, 0)), pl.BlockSpec(ps.shape, lambda i: (0, 0)),
                   pl.BlockSpec(pw.shape, lambda i: (0, 0, 0))],
        out_shape=[jax.ShapeDtypeStruct((T, D), BF16), jax.ShapeDtypeStruct((T, P), BF16),
                   jax.ShapeDtypeStruct((1, D), F32), jax.ShapeDtypeStruct(cw.shape, F32),
                   jax.ShapeDtypeStruct(ps.shape, F32), jax.ShapeDtypeStruct(pw.shape, F32)],
        scratch_shapes=[pltpu.VMEM((tm, D), F32), pltpu.VMEM((HALO, CONV_WIDTH), F32),
                        pltpu.VMEM((HALO, CONV_WIDTH), F32)])


def _inproj_bwd(dproj, win, x, dx1, g1, comm, tm=256):
    T, D = x.shape
    nblk, _, bn = win.shape

    def body(dp_ref, w_ref, x_ref, dx1_ref, g1_ref, gx_ref, dg1_ref):
        @pl.when(pl.program_id(0) == 0)
        def _():
            dg1_ref[...] = jnp.zeros_like(dg1_ref)

        dh = lax.dot_general(dp_ref[:, 0:bn], w_ref[0], NT_DIMS, preferred_element_type=F32)
        for k in range(1, nblk):
            dh = dh + lax.dot_general(dp_ref[:, k * bn:(k + 1) * bn], w_ref[k], NT_DIMS,
                                      preferred_element_type=F32)
        xv = x_ref[...]
        r = _rsq(xv)
        n = xv * r
        dg1_ref[...] += jnp.sum(dh * n, axis=0, keepdims=True)
        gx_ref[...] = dx1_ref[...] + _norm_bwd(dh * g1_ref[...], n, r)

    row = pl.BlockSpec((tm, D), lambda i: (i, 0))
    vec = pl.BlockSpec((1, D), lambda i: (0, 0))
    return _host_call(
        body, name="inproj_bwd", grid=(T // tm,), comm=comm, args=(dproj, win, x, dx1, g1),
        in_specs=[pl.BlockSpec((tm, nblk * bn), lambda i: (i, 0)), _whole(win.shape), row, row, vec],
        out_specs=[row, vec],
        out_shape=[jax.ShapeDtypeStruct((T, D), F32), jax.ShapeDtypeStruct((1, D), F32)])


def _adamw(w, g, m, v):
    m = ADAM_B1 * m + (1.0 - ADAM_B1) * g
    v = ADAM_B2 * v + (1.0 - ADAM_B2) * jnp.square(g)
    m_hat = m / (1.0 - ADAM_B1 ** ADAM_STEP)
    v_hat = v / (1.0 - ADAM_B2 ** ADAM_STEP)
    delta = -ADAM_LR * (m_hat / (jnp.sqrt(v_hat) + ADAM_EPS) + ADAM_WD * w)
    return delta, m, v


def _sum_adamw(parts, w, m, v, name, tr):
    r, cd = w.shape

    def body(p_ref, w_ref, m_ref, v_ref, g_ref, d_ref, mo_ref, vo_ref):
        g = p_ref[0].astype(F32)
        for k in range(1, NDEV):
            g = g + p_ref[k].astype(F32)
        g_ref[...] = g
        d_ref[...], mo_ref[...], vo_ref[...] = _adamw(w_ref[...], g, m_ref[...], v_ref[...])

    blk = pl.BlockSpec((tr, cd), lambda i: (i, 0))
    shp = jax.ShapeDtypeStruct((r, cd), F32)
    return pl.pallas_call(
        body, name=name, grid=(r // tr,),
        in_specs=[pl.BlockSpec((NDEV, tr, cd), lambda i: (0, i, 0)), blk, blk, blk],
        out_specs=[blk] * 4, out_shape=[shp] * 4,
        compiler_params=_params("arbitrary"),
    )(parts, w, m, v)


def _small_reduce_adamw(loss_part, vec_grads, dps, dcw, dpw, vec_state, ps_state, cw_state, pw_state):
    D = vec_grads[0].shape[1]
    pw_rows = pw_state[0].shape[1]
    states = list(vec_state) + [ps_state, cw_state, pw_state]
    n_in = 1 + 4 + 3 + 3 * len(states)
    n_out = 1 + 4 * len(states)

    def body(*refs):
        loss_ref, dg = refs[0], refs[1:5]
        dps_ref, dcw_ref, dpw_ref = refs[5:8]
        st = refs[8:n_in]
        loss_out, outs = refs[n_in], refs[n_in + 1:n_in + n_out]
        pack, gat, cbuf, pbuf, send_sems, recv_sems, local_sems = refs[n_in + n_out:]
        x, y, c = _coords()
        me = _device_index((x, y, c))

        pack[...] = jnp.zeros_like(pack)
        for k in range(4):
            pack[k:k + 1, :] = dg[k][...]
        pack[4:5, 0:dps_ref.shape[1]] = dps_ref[...]
        pack[5:6, 0:loss_ref.shape[1]] = loss_ref[...]

        def pw_slice(i):
            return dpw_ref.at[:, pl.ds(i * pw_rows, pw_rows), :]

        mine = [pltpu.make_async_copy(pack, gat.at[me], local_sems.at[0]),
                pltpu.make_async_copy(dcw_ref.at[me], cbuf.at[me], local_sems.at[1]),
                pltpu.make_async_copy(pw_slice(me), pbuf.at[me], local_sems.at[2])]
        for cp in mine:
            cp.start()
        sends, recvs = [], []
        for mask in range(1, NDEV):
            peer = (1 - x if mask & 4 else x, 1 - y if mask & 2 else y, 1 - c if mask & 1 else c)
            p = _device_index(peer)
            for k, (src, buf) in enumerate(((pack, gat), (dcw_ref.at[p], cbuf), (pw_slice(p), pbuf))):
                kw = dict(send_sem=send_sems.at[mask, k], recv_sem=recv_sems.at[mask, k],
                          device_id=peer, device_id_type=MESH)
                sends.append(pltpu.make_async_remote_copy(src_ref=src, dst_ref=buf.at[me], **kw))
                recvs.append(pltpu.make_async_remote_copy(src_ref=src, dst_ref=buf.at[p], **kw))
                sends[-1].start()
        for cp in recvs:
            cp.wait_recv()
        for cp in sends:
            cp.wait_send()
        for cp in mine:
            cp.wait()

        def slot_sum(buf):
            s = buf[0]
            for k in range(1, NDEV):
                s = s + buf[k]
            return s

        vec = slot_sum(gat)
        loss_out[...] = vec[5:6, 0:loss_ref.shape[1]]
        grads = [vec[k:k + 1, :] for k in range(4)] + [vec[4:5, 0:dps_ref.shape[1]], slot_sum(cbuf), slot_sum(pbuf)]
        for k, g in enumerate(grads):
            w_ref, m_ref, v_ref = st[3 * k:3 * k + 3]
            outs[4 * k][...] = g
            outs[4 * k + 1][...], outs[4 * k + 2][...], outs[4 * k + 3][...] = _adamw(
                w_ref[...], g, m_ref[...], v_ref[...])

    flat_state = [a for s in states for a in s]
    out_shape = [jax.ShapeDtypeStruct(loss_part.shape, F32)]
    out_shape += [jax.ShapeDtypeStruct(s[0].shape, F32) for s in states for _ in range(4)]
    return pl.pallas_call(
        body, name="small_reduce_adamw",
        in_specs=[VMEM_SPEC] * n_in, out_specs=[VMEM_SPEC] * n_out, out_shape=out_shape,
        scratch_shapes=[pltpu.VMEM((NDEV, D), F32), pltpu.VMEM((NDEV, NDEV, D), F32),
                        pltpu.VMEM((NDEV,) + cw_state[0].shape, F32), pltpu.VMEM((NDEV,) + pw_state[0].shape, F32),
                        pltpu.SemaphoreType.DMA((NDEV, 3)), pltpu.SemaphoreType.DMA((NDEV, 3)),
                        pltpu.SemaphoreType.DMA((3,))],
        compiler_params=pltpu.CompilerParams(vmem_limit_bytes=VMEM_LIMIT),
    )(loss_part, *vec_grads, dps, dcw, dpw, *flat_state)


ROW_TILE = dict(w_in=512, w_gate=256, w_up=256, w_down=176, w_out=128)
FORWARD_STEP = dict(inproj=56, mixer_fwd=26, ffn_up=32)


def kernel(x, ln_mix_pre, w_in, conv_w, pool_w, pool_scale, w_out, ln_mix_post, ln_ffn_pre, w_gate, w_up, w_down, ln_ffn_post, loss_target, m_ln_mix_pre, m_w_in, m_conv_w, m_pool_w, m_pool_scale, m_w_out, m_ln_mix_post, m_ln_ffn_pre, m_w_gate, m_w_up, m_w_down, m_ln_ffn_post, v_ln_mix_pre, v_w_in, v_conv_w, v_pool_w, v_pool_scale, v_w_out, v_ln_mix_post, v_ln_ffn_pre, v_w_gate, v_w_up, v_w_down, v_ln_ffn_post):
    D = x.shape[2]
    xs, tgt = x[0], loss_target[0]
    win, wg, wu, wd, wout, pw, cw = _cast_gather_first(
        [w_in[0], w_gate[0], w_up[0], w_down[0], w_out[0], pool_w[0], conv_w[0]])

    (proj, h), (wout, wg) = _inproj(xs, ln_mix_pre, win, _gather_comm([wout, wg], FORWARD_STEP["inproj"]))
    wout2 = wout.reshape(D, D)
    (x1, hf, mixed, mo), (wu,) = _mixer_fwd(proj, xs, cw, pw, pool_scale, wout2, ln_mix_post, ln_ffn_pre,
                                            _gather_comm([wu], FORWARD_STEP["mixer_fwd"]))
    (g, u, a), (wd,) = _ffn_up(hf, wg, wu, _gather_comm([wd], FORWARD_STEP["ffn_up"]))
    dy, dff, loss, dg4 = _ffn_down_loss(a, wd, x1, tgt, ln_ffn_post)

    rows = lambda arr, lo, hi: (lo * arr.shape[1] // 4, (hi - lo) * arr.shape[1] // 4)
    dwd = _wgrad_down(a, dff)
    (dg, du), (dwd_parts,) = _ffn_bwd_act(dff, wd, g, u, _scatter_comm([(dwd, None, *rows(dwd, 0, 3))]))
    (dwg, dwu), (dwd_parts,) = _wgrad_gate_up(hf, dg, du, _scatter_comm([(dwd, dwd_parts, *rows(dwd, 3, 4))]))
    (dx1, dg3), (dwg_parts, dwu_parts) = _ffn_bwd_in(
        dg, du, wg, wu, x1, dy, ln_ffn_pre, _scatter_comm([(dwg, None, *rows(dwg, 0, 3))]),
        lambda c0: _scatter_comm([(dwg, c0[0], *rows(dwg, 3, 4)), (dwu, None, *rows(dwu, 0, 2))]))
    (dmo, dproj, dg2, dcw, dps, dpw), (dwu_parts,) = _mixer_bwd(
        dx1, mo, proj, cw, pw, pool_scale, wout2, ln_mix_post, _scatter_comm([(dwu, dwu_parts, *rows(dwu, 2, 4))]))
    dwout = _wgrad_out(mixed, dmo)
    dwin, (dwout_parts,) = _wgrad_in(h, dproj, _scatter_comm([(dwout, None, *rows(dwout, 0, 4))]))
    (gx, dg1), (dwin_parts,) = _inproj_bwd(dproj, win, xs, dx1, ln_mix_pre,
                                           _scatter_comm([(dwin, None, *rows(dwin, 0, 4))]))

    res = {}
    for k, parts, w, m, v in (("w_down", dwd_parts, w_down, m_w_down, v_w_down),
                              ("w_gate", dwg_parts, w_gate, m_w_gate, v_w_gate),
                              ("w_up", dwu_parts, w_up, m_w_up, v_w_up),
                              ("w_out", dwout_parts, w_out, m_w_out, v_w_out),
                              ("w_in", dwin_parts, w_in, m_w_in, v_w_in)):
        res[k] = [o.reshape(w.shape) for o in _sum_adamw(parts, w[0], m[0], v[0], "sum_adamw_" + k, ROW_TILE[k])]

    loss_sum, *small = _small_reduce_adamw(
        loss, [dg1, dg2, dg3, dg4], dps, dcw, dpw,
        [(ln_mix_pre, m_ln_mix_pre, v_ln_mix_pre), (ln_mix_post, m_ln_mix_post, v_ln_mix_post),
         (ln_ffn_pre, m_ln_ffn_pre, v_ln_ffn_pre), (ln_ffn_post, m_ln_ffn_post, v_ln_ffn_post)],
        (pool_scale, m_pool_scale, v_pool_scale), (conv_w[0], m_conv_w[0], v_conv_w[0]),
        (pool_w[0], m_pool_w[0], v_pool_w[0]))
    small_names = ["ln_mix_pre", "ln_mix_post", "ln_ffn_pre", "ln_ffn_post", "pool_scale", "conv_w", "pool_w"]
    shapes = dict(conv_w=conv_w.shape, pool_w=pool_w.shape)
    for i, k in enumerate(small_names):
        res[k] = [o.reshape(shapes[k]) if k in shapes else o for o in small[4 * i:4 * i + 4]]

    order = ["ln_mix_pre", "w_in", "conv_w", "pool_w", "pool_scale", "w_out", "ln_mix_post", "ln_ffn_pre",
             "w_gate", "w_up", "w_down", "ln_ffn_post"]
    return (loss_sum[0, 0], gx[None], *[res[k][0] for k in order], *[res[k][1] for k in order],
            *[res[k][2] for k in order], *[res[k][3] for k in order])
```

```python
import functools
from typing import Any, NamedTuple

import jax
import jax.numpy as jnp
from jax import lax
from jax.experimental import pallas as pl
from jax.experimental.pallas import tpu as pltpu

EPS = 1e-6
NDEV = 8
CONV_HEADS = 8
HEAD_DIM = 128
CONV_WIDTH = CONV_HEADS * HEAD_DIM
POOL_WINDOWS = (2, 4, 8, 16)
POOL_GROUP_DIM = 256
HALO = 16

ADAM_LR = 0.001
ADAM_B1 = 0.9
ADAM_B2 = 0.999
ADAM_EPS = 1e-08
ADAM_WD = 0.01
ADAM_STEP = 10

F32 = jnp.float32
BF16 = jnp.bfloat16
VMEM_LIMIT = 58 * 1024 * 1024
MESH = pl.DeviceIdType.MESH
HBM_SPEC = pl.BlockSpec(memory_space=pl.ANY)
VMEM_SPEC = pl.BlockSpec(memory_space=pltpu.VMEM)

NT_DIMS = (((1,), (1,)), ((), ()))
TN_DIMS = (((0,), (0,)), ((), ()))


def _params(*sem):
    return pltpu.CompilerParams(dimension_semantics=sem, vmem_limit_bytes=VMEM_LIMIT)


def _rsq(v):
    return lax.rsqrt(jnp.mean(v * v, axis=-1, keepdims=True) + EPS)


def _norm_bwd(dn, n, r):
    return r * (dn - n * jnp.mean(dn * n, axis=-1, keepdims=True))


def _whole(shape):
    nd = len(shape)
    return pl.BlockSpec(shape, lambda *_: (0,) * nd, pipeline_mode=pl.Buffered(1))


def _inv_count(t0, tm, w):
    t = t0 + lax.broadcasted_iota(jnp.int32, (tm, 1), 0)
    return 1.0 / jnp.minimum(t + 1, w).astype(F32)


def _window_sum(ext, w, back):
    n = ext.shape[0]
    s, shift = ext, 1
    while shift < w:
        s = s + pltpu.roll(s, shift if back else n - shift, 0)
        shift *= 2
    return s


class _Comm(NamedTuple):
    arrays: Any
    out_shape: Any
    aliases: Any
    scratch: Any
    hooks: Any


def _coords():
    return lax.axis_index("x"), lax.axis_index("y"), lax.axis_index("c")


def _other_chips(x, y):
    return [(1 - x, y), (x, 1 - y), (1 - x, 1 - y)]


def _device_index(dev):
    return 4 * dev[0] + 2 * dev[1] + dev[2]


def _host_call(body, *, name, grid, in_specs, out_specs, out_shape, args, scratch_shapes=(), comm=None):
    sem = ("arbitrary",) * len(grid)
    in_specs, out_specs, out_shape, scratch_shapes = list(in_specs), list(out_specs), list(out_shape), list(scratch_shapes)
    if comm is None:
        res = pl.pallas_call(body, name=name, grid=grid, in_specs=in_specs, out_specs=out_specs, out_shape=out_shape,
                             scratch_shapes=scratch_shapes, compiler_params=_params(*sem))(*args)
        return res, []
    n_in, n_out, n_scr = len(in_specs), len(out_specs), len(scratch_shapes)
    n_cin, n_cout = len(comm.arrays), len(comm.out_shape)
    total = functools.reduce(lambda a, b: a * b, grid)

    def wrapped(*refs):
        ins, cin = refs[:n_in], refs[n_in:n_in + n_cin]
        o0 = n_in + n_cin
        outs, cout = refs[o0:o0 + n_out], refs[o0 + n_out:o0 + n_out + n_cout]
        s0 = o0 + n_out + n_cout
        scr, sems = refs[s0:s0 + n_scr], refs[s0 + n_scr:]
        step = pl.program_id(0)
        for d in range(1, len(grid)):
            step = step * grid[d] + pl.program_id(d)
        for when, before, fn in comm.hooks:
            if before:
                pl.when(step == when % total)(functools.partial(fn, cin, cout, sems))
        body(*ins, *outs, *scr)
        for when, before, fn in comm.hooks:
            if not before:
                pl.when(step == when % total)(functools.partial(fn, cin, cout, sems))

    res = pl.pallas_call(
        wrapped, name=name, grid=grid,
        in_specs=in_specs + [HBM_SPEC] * n_cin, out_specs=out_specs + [HBM_SPEC] * n_cout,
        out_shape=out_shape + list(comm.out_shape), scratch_shapes=scratch_shapes + list(comm.scratch),
        input_output_aliases={n_in + i: n_out + o for i, o in comm.aliases.items()},
        compiler_params=_params(*sem),
    )(*args, *comm.arrays)
    return res[:n_out], res[n_out:]


def _gather_steps(n, view, own_src, send_sems, recv_sems):
    x, y, c = _coords()
    me, sibling = (x, y, c), (x, y, 1 - c)
    chips = _other_chips(x, y)

    def copy(a, k, block, to, src=None):
        return pltpu.make_async_remote_copy(
            src_ref=view(a, block) if src is None else src, dst_ref=view(a, block),
            send_sem=send_sems.at[a, k], recv_sem=recv_sems.at[a, k], device_id=to, device_id_type=MESH)

    def first_copies():
        cps = []
        for a in range(n):
            cps.append(copy(a, 0, me, sibling, src=own_src(a)))
            cps += [copy(a, 1 + j, me, (*chip, c), src=own_src(a)) for j, chip in enumerate(chips)]
        return cps

    def passed_copies():
        return [copy(a, 4 + j, (*chip, c), sibling) for j, chip in enumerate(chips) for a in range(n)]

    def first():
        for cp in first_copies():
            cp.start()

    def forward():
        for j, chip in enumerate(chips):
            for a in range(n):
                copy(a, 1 + j, (*chip, c), me).wait_recv()
                copy(a, 4 + j, (*chip, c), sibling).start()

    def finish():
        for a in range(n):
            copy(a, 0, sibling, me).wait_recv()
            for j, chip in enumerate(chips):
                copy(a, 4 + j, (*chip, 1 - c), me).wait_recv()
        for cp in first_copies() + passed_copies():
            cp.wait_send()

    return first, forward, finish


def _gather_comm(arrays, forward_step):
    n = len(arrays)

    def steps(cout, sems):
        view = lambda a, dev: cout[a].at[_device_index(dev)]
        return _gather_steps(n, view, lambda a: view(a, _coords()), sems[0], sems[1])

    hooks = [(0, True, lambda cin, cout, sems: steps(cout, sems)[0]()),
             (forward_step, True, lambda cin, cout, sems: steps(cout, sems)[1]()),
             (-1, False, lambda cin, cout, sems: steps(cout, sems)[2]())]
    return _Comm(list(arrays), [jax.ShapeDtypeStruct(a.shape, a.dtype) for a in arrays], {i: i for i in range(n)},
                 [pltpu.SemaphoreType.DMA((n, 7)), pltpu.SemaphoreType.DMA((n, 7))], hooks)


def _scatter_comm(items):
    n = len(items)
    arrays, aliases, grad_at = [], {}, []
    for k, (grad, slots, _, _) in enumerate(items):
        grad_at.append(len(arrays))
        arrays.append(grad)
        if slots is not None:
            aliases[len(arrays)] = k
            arrays.append(slots)

    def copies(cin, cout, sems):
        send_sems, recv_sems, local_sems = sems
        x, y, c = _coords()
        me = _device_index((x, y, c))
        src = lambda a, p: cin[grad_at[a]].at[p, pl.ds(items[a][2], items[a][3]), :]
        dst = lambda a, p: cout[a].at[p, pl.ds(items[a][2], items[a][3]), :]
        mine = [pltpu.make_async_copy(src(a, me), dst(a, me), local_sems.at[a]) for a in range(n)]
        sends, recvs = [], []
        for a in range(n):
            for mask in range(1, NDEV):
                peer = (1 - x if mask & 4 else x, 1 - y if mask & 2 else y, 1 - c if mask & 1 else c)
                p = _device_index(peer)
                kw = dict(send_sem=send_sems.at[a, mask - 1], recv_sem=recv_sems.at[a, mask - 1],
                          device_id=peer, device_id_type=MESH)
                sends.append(pltpu.make_async_remote_copy(src_ref=src(a, p), dst_ref=dst(a, me), **kw))
                recvs.append(pltpu.make_async_remote_copy(src_ref=src(a, p), dst_ref=dst(a, p), **kw))
        return mine, sends, recvs

    def start(cin, cout, sems):
        mine, sends, _ = copies(cin, cout, sems)
        for cp in mine + sends:
            cp.start()

    def finish(cin, cout, sems):
        mine, sends, recvs = copies(cin, cout, sems)
        for cp in recvs:
            cp.wait_recv()
        for cp in sends:
            cp.wait_send()
        for cp in mine:
            cp.wait()

    return _Comm(arrays, [jax.ShapeDtypeStruct(it[0].shape, it[0].dtype) for it in items], aliases,
                 [pltpu.SemaphoreType.DMA((n, NDEV - 1)), pltpu.SemaphoreType.DMA((n, NDEV - 1)),
                  pltpu.SemaphoreType.DMA((n,))],
                 [(0, True, start), (-1, False, finish)])


NOW_ITEMS = (0, 5, 6)
FLIPPED_ITEMS = (1, 2)
POOL_ITEM = 5


def _cast_gather_first(shards):
    n = len(shards)
    dtypes = [BF16] * 6 + [F32]
    block_shapes = [s.shape[::-1] if a in FLIPPED_ITEMS else s.shape for a, s in enumerate(shards)]
    out_shapes = [(NDEV,) + s for s in block_shapes]
    g, rows, cols = shards[POOL_ITEM].shape
    out_shapes[POOL_ITEM] = (g, rows * NDEV, cols)

    later = [a for a in range(n) if a not in NOW_ITEMS]

    def body(*refs):
        ins, outs, raw, stage = refs[:n], refs[n:2 * n], refs[2 * n:3 * n], refs[3 * n:4 * n]
        send_sems, recv_sems, local_sems, load_sems = refs[4 * n:]

        def view(a, dev):
            i = _device_index(dev)
            if a == POOL_ITEM:
                return outs[a].at[:, pl.ds(i * rows, rows), :]
            return outs[a].at[i]

        loads = [pltpu.make_async_copy(ins[a], raw[a], load_sems.at[a]) for a in range(n)]
        mine = [pltpu.make_async_copy(stage[a], view(a, _coords()), local_sems.at[a]) for a in range(n)]
        for a in list(NOW_ITEMS) + later:
            loads[a].start()
        first, forward, finish = _gather_steps(
            len(NOW_ITEMS), lambda k, dev: view(NOW_ITEMS[k], dev), lambda k: stage[NOW_ITEMS[k]], send_sems, recv_sems)
        for a in list(NOW_ITEMS) + later:
            loads[a].wait()
            if a in FLIPPED_ITEMS:
                k = raw[a].shape[0]
                eye = (lax.broadcasted_iota(jnp.int32, (k, k), 0) == lax.broadcasted_iota(jnp.int32, (k, k), 1))
                stage[a][...] = lax.dot_general(raw[a][...].astype(BF16), eye.astype(BF16), TN_DIMS,
                                                preferred_element_type=F32).astype(BF16)
            else:
                stage[a][...] = raw[a][...].astype(dtypes[a])
            mine[a].start()
            if a == NOW_ITEMS[-1]:
                first()
        forward()
        finish()
        for cp in mine:
            cp.wait()

    return pl.pallas_call(
        body, name="cast_gather_first",
        in_specs=[HBM_SPEC] * n, out_specs=[HBM_SPEC] * n,
        out_shape=[jax.ShapeDtypeStruct(s, d) for s, d in zip(out_shapes, dtypes)],
        scratch_shapes=[pltpu.VMEM(s.shape, s.dtype) for s in shards]
        + [pltpu.VMEM(s, d) for s, d in zip(block_shapes, dtypes)]
        + [pltpu.SemaphoreType.DMA((len(NOW_ITEMS), 7)), pltpu.SemaphoreType.DMA((len(NOW_ITEMS), 7)),
           pltpu.SemaphoreType.DMA((n,)), pltpu.SemaphoreType.DMA((n,))],
        compiler_params=pltpu.CompilerParams(vmem_limit_bytes=VMEM_LIMIT),
    )(*shards)


def _inproj(x, g1, win, comm, tm=1024):
    T, D = x.shape
    nb, _, bn = win.shape

    def body(x_ref, g_ref, w_ref, proj_ref, h_ref):
        @pl.when(pl.program_id(1) == 0)
        def _():
            xv = x_ref[...]
            h_ref[...] = (xv * _rsq(xv) * g_ref[...]).astype(BF16)

        proj_ref[...] = jnp.dot(h_ref[...], w_ref[0], preferred_element_type=F32).astype(BF16)

    return _host_call(
        body, name="inproj", grid=(T // tm, nb), comm=comm, args=(x, g1, win),
        in_specs=[pl.BlockSpec((tm, D), lambda i, j: (i, 0)),
                  pl.BlockSpec((1, D), lambda i, j: (0, 0)),
                  pl.BlockSpec((1, D, bn), lambda i, j: (j, 0, 0))],
        out_specs=[pl.BlockSpec((tm, bn), lambda i, j: (i, j)),
                   pl.BlockSpec((tm, D), lambda i, j: (i, 0))],
        out_shape=[jax.ShapeDtypeStruct((T, nb * bn), BF16), jax.ShapeDtypeStruct((T, D), BF16)])


def _mixer_fwd(proj, x, cw, pw, ps, wout, g2, g3, comm, tm=256):
    T, D = x.shape
    P = proj.shape[1]

    def body(proj_ref, x_ref, cw_ref, pw_ref, ps_ref, wout_ref, g2_ref, g3_ref,
             x1_ref, hf_ref, mixed_ref, mo_ref, cu_carry, v_carry):
        i = pl.program_id(0)

        @pl.when(i == 0)
        def _():
            cu_carry[...] = jnp.zeros_like(cu_carry)
            v_carry[...] = jnp.zeros_like(v_carry)

        for h in range(CONV_HEADS):
            lo = h * HEAD_DIM
            gate_b = proj_ref[:, lo:lo + HEAD_DIM].astype(F32)
            cu = proj_ref[:, CONV_WIDTH + lo:CONV_WIDTH + lo + HEAD_DIM].astype(F32) * \
                proj_ref[:, 2 * CONV_WIDTH + lo:2 * CONV_WIDTH + lo + HEAD_DIM].astype(F32)
            ext = jnp.concatenate([cu_carry[:, lo:lo + HEAD_DIM], cu], axis=0)
            c1 = pltpu.roll(ext, 1, 0)[HALO:]
            c2 = pltpu.roll(ext, 2, 0)[HALO:]
            ya = gate_b * (cw_ref[h, 2:3, :] * cu + cw_ref[h, 1:2, :] * c1 + cw_ref[h, 0:1, :] * c2)
            mixed_ref[:, lo:lo + HEAD_DIM] = (ya * _rsq(ya)).astype(BF16)
            cu_carry[:, lo:lo + HEAD_DIM] = cu[tm - HALO:]

        for gi, w in enumerate(POOL_WINDOWS):
            lo = gi * POOL_GROUP_DIM
            v = proj_ref[:, 3 * CONV_WIDTH + lo:3 * CONV_WIDTH + lo + POOL_GROUP_DIM].astype(F32)
            ext = jnp.concatenate([v_carry[:, lo:lo + POOL_GROUP_DIM], v], axis=0)
            pooled = _window_sum(ext, w, True)[HALO:] * _inv_count(i * tm, tm, w) - v
            y = jnp.dot(pooled.astype(BF16), pw_ref[gi], preferred_element_type=F32)
            yb = y * _rsq(y) * ps_ref[:, lo:lo + POOL_GROUP_DIM]
            mixed_ref[:, CONV_WIDTH + lo:CONV_WIDTH + lo + POOL_GROUP_DIM] = yb.astype(BF16)
            v_carry[:, lo:lo + POOL_GROUP_DIM] = v[tm - HALO:]

        mo = jnp.dot(mixed_ref[...], wout_ref[...], preferred_element_type=F32)
        mo_ref[...] = mo
        x1 = x_ref[...] + mo * _rsq(mo) * g2_ref[...]
        x1_ref[...] = x1
        hf_ref[...] = (x1 * _rsq(x1) * g3_ref[...]).astype(BF16)

    row = lambda n: pl.BlockSpec((tm, n), lambda i: (i, 0))
    return _host_call(
        body, name="mixer_fwd", grid=(T // tm,), comm=comm, args=(proj, x, cw, pw, ps, wout, g2, g3),
        in_specs=[row(P), row(D), _whole(cw.shape), _whole(pw.shape), _whole(ps.shape),
                  _whole(wout.shape), _whole(g2.shape), _whole(g3.shape)],
        out_specs=[row(D), row(D), row(D), row(D)],
        out_shape=[jax.ShapeDtypeStruct((T, D), F32), jax.ShapeDtypeStruct((T, D), BF16),
                   jax.ShapeDtypeStruct((T, D), BF16), jax.ShapeDtypeStruct((T, D), F32)],
        scratch_shapes=[pltpu.VMEM((HALO, CONV_WIDTH), F32), pltpu.VMEM((HALO, CONV_WIDTH), F32)])


def _ffn_up(hf, wg, wu, comm, tm=1024):
    T, D = hf.shape
    nb, _, bf = wg.shape

    def body(hf_ref, wg_ref, wu_ref, g_ref, u_ref, a_ref):
        hv = hf_ref[...]
        g = jnp.dot(hv, wg_ref[0], preferred_element_type=F32)
        u = jnp.dot(hv, wu_ref[0], preferred_element_type=F32)
        g_ref[0] = g.astype(BF16)
        u_ref[0] = u.astype(BF16)
        a_ref[0] = (g * jax.nn.sigmoid(g) * u).astype(BF16)

    wspec = pl.BlockSpec((1, D, bf), lambda i, j: (j, 0, 0))
    ospec = pl.BlockSpec((1, tm, bf), lambda i, j: (j, i, 0))
    oshape = jax.ShapeDtypeStruct((nb, T, bf), BF16)
    return _host_call(
        body, name="ffn_up", grid=(T // tm, nb), comm=comm, args=(hf, wg, wu),
        in_specs=[pl.BlockSpec((tm, D), lambda i, j: (i, 0)), wspec, wspec],
        out_specs=[ospec, ospec, ospec], out_shape=[oshape, oshape, oshape])


def _ffn_down_loss(a, wd, x1, tgt, g4, tm=256):
    nblk, T, bf = a.shape
    D = x1.shape[1]
    nt = T // tm

    def body(a_ref, wd_ref, x1_ref, tgt_ref, g4_ref, dy_ref, dff_ref, loss_ref, dg4_ref, lacc_ref):
        i = pl.program_id(0)

        @pl.when(i == 0)
        def _():
            lacc_ref[...] = jnp.zeros_like(lacc_ref)
            dg4_ref[...] = jnp.zeros_like(dg4_ref)

        ff = jnp.dot(a_ref[0], wd_ref[0], preferred_element_type=F32)
        for k in range(1, nblk):
            ff = ff + jnp.dot(a_ref[k], wd_ref[k], preferred_element_type=F32)
        r = _rsq(ff)
        n = ff * r
        g4v = g4_ref[...]
        e = x1_ref[...] + n * g4v - tgt_ref[...]
        lacc_ref[...] += jnp.sum(e * e, axis=0, keepdims=True)
        dy = e * (1.0 / D)
        dy_ref[...] = dy
        dg4_ref[...] += jnp.sum(dy * n, axis=0, keepdims=True)
        dff_ref[...] = _norm_bwd(dy * g4v, n, r).astype(BF16)

        @pl.when(i == nt - 1)
        def _():
            loss_ref[...] = jnp.full(loss_ref.shape, (0.5 / D) * jnp.sum(lacc_ref[...]), F32)

    row = pl.BlockSpec((tm, D), lambda i: (i, 0))
    vec = pl.BlockSpec((1, D), lambda i: (0, 0))
    return _host_call(
        body, name="ffn_down_loss", grid=(nt,), args=(a, wd, x1, tgt, g4),
        in_specs=[pl.BlockSpec((nblk, tm, bf), lambda i: (0, i, 0)), _whole(wd.shape), row, row, vec],
        out_specs=[row, row, pl.BlockSpec((1, 128), lambda i: (0, 0)), vec],
        out_shape=[jax.ShapeDtypeStruct((T, D), F32), jax.ShapeDtypeStruct((T, D), BF16),
                   jax.ShapeDtypeStruct((1, 128), F32), jax.ShapeDtypeStruct((1, D), F32)],
        scratch_shapes=[pltpu.VMEM((1, D), F32)])[0]


def _ffn_bwd_act(dff, wd, g, u, comm, tm=1024):
    T, D = dff.shape
    nb, bf, _ = wd.shape

    def body(dff_ref, wd_ref, g_ref, u_ref, dg_ref, du_ref):
        da = lax.dot_general(dff_ref[...], wd_ref[0], NT_DIMS, preferred_element_type=F32)
        gv = g_ref[0].astype(F32)
        s = jax.nn.sigmoid(gv)
        du_ref[0] = (da * (gv * s)).astype(BF16)
        dg_ref[0] = (da * u_ref[0].astype(F32) * (s * (1.0 + gv * (1.0 - s)))).astype(BF16)

    blk = pl.BlockSpec((1, tm, bf), lambda i, j: (j, i, 0))
    oshape = jax.ShapeDtypeStruct((nb, T, bf), BF16)
    return _host_call(
        body, name="ffn_bwd_act", grid=(T // tm, nb), comm=comm, args=(dff, wd, g, u),
        in_specs=[pl.BlockSpec((tm, D), lambda i, j: (i, 0)),
                  pl.BlockSpec((1, bf, D), lambda i, j: (j, 0, 0)), blk, blk],
        out_specs=[blk, blk], out_shape=[oshape, oshape])


def _ffn_bwd_in(dg, du, wg, wu, x1, dy, g3, comm_a, comm_b, tm=256):
    nb, T, bf = dg.shape
    D = x1.shape[1]
    hb = nb // 2

    def partial_sum(dg_ref, du_ref, wg_ref, wu_ref):
        s = lax.dot_general(dg_ref[0], wg_ref[0], NT_DIMS, preferred_element_type=F32)
        s = s + lax.dot_general(du_ref[0], wu_ref[0], NT_DIMS, preferred_element_type=F32)
        for k in range(1, hb):
            s = s + lax.dot_general(dg_ref[k], wg_ref[k], NT_DIMS, preferred_element_type=F32)
            s = s + lax.dot_general(du_ref[k], wu_ref[k], NT_DIMS, preferred_element_type=F32)
        return s

    def first(dg_ref, du_ref, wg_ref, wu_ref, part_ref):
        part_ref[...] = partial_sum(dg_ref, du_ref, wg_ref, wu_ref)

    def second(dg_ref, du_ref, wg_ref, wu_ref, part_ref, x1_ref, dy_ref, g3_ref, dx1_ref, dg3_ref):
        @pl.when(pl.program_id(0) == 0)
        def _():
            dg3_ref[...] = jnp.zeros_like(dg3_ref)

        dhf = part_ref[...] + partial_sum(dg_ref, du_ref, wg_ref, wu_ref)
        x1v = x1_ref[...]
        r = _rsq(x1v)
        n = x1v * r
        dg3_ref[...] += jnp.sum(dhf * n, axis=0, keepdims=True)
        dx1_ref[...] = dy_ref[...] + _norm_bwd(dhf * g3_ref[...], n, r)

    row = pl.BlockSpec((tm, D), lambda i: (i, 0))
    vec = pl.BlockSpec((1, D), lambda i: (0, 0))
    rowshape = jax.ShapeDtypeStruct((T, D), F32)

    def specs(half):
        ablk = pl.BlockSpec((hb, tm, bf), lambda i: (half, i, 0))
        wblk = pl.BlockSpec((hb, D, bf), lambda i: (half, 0, 0), pipeline_mode=pl.Buffered(1))
        return [ablk, ablk, wblk, wblk]

    (part,), c0 = _host_call(first, name="ffn_bwd_in_a", grid=(T // tm,), comm=comm_a, args=(dg, du, wg, wu),
                             in_specs=specs(0), out_specs=[row], out_shape=[rowshape])
    res, c1 = _host_call(second, name="ffn_bwd_in_b", grid=(T // tm,), comm=comm_b,
                         args=(dg, du, wg, wu, part, x1, dy, g3),
                         in_specs=specs(1) + [row, row, row, vec], out_specs=[row, vec],
                         out_shape=[rowshape, jax.ShapeDtypeStruct((1, D), F32)])
    return res, c0, c1


def _wgrad(name, lhs, rhs, lhs_spec, rhs_spec, n_rhs, M, N, nb, nk, comm=None):
    def body(*refs):
        l_ref, r_refs = refs[0], refs[1:1 + n_rhs]
        o_refs, acc_refs = refs[1 + n_rhs:1 + 2 * n_rhs], refs[1 + 2 * n_rhs:]
        k = pl.program_id(1)
        tile = lambda ref: ref[0] if len(ref.shape) == 3 else ref[...]

        @pl.when(k == 0)
        def _():
            for acc_ref in acc_refs:
                acc_ref[...] = jnp.zeros_like(acc_ref)

        for r_ref, acc_ref in zip(r_refs, acc_refs):
            acc_ref[...] += lax.dot_general(tile(l_ref), tile(r_ref), TN_DIMS, preferred_element_type=F32)

        @pl.when(k == nk - 1)
        def _():
            for o_ref, acc_ref in zip(o_refs, acc_refs):
                o_ref[0] = acc_ref[...].astype(BF16)

    oblk = pl.BlockSpec((1, M, N), lambda j, k: (j, 0, 0))
    oshape = jax.ShapeDtypeStruct((nb, M, N), BF16)
    return _host_call(
        body, name=name, grid=(nb, nk), comm=comm, args=(lhs, *rhs),
        in_specs=[lhs_spec] + [rhs_spec] * n_rhs, out_specs=[oblk] * n_rhs, out_shape=[oshape] * n_rhs,
        scratch_shapes=[pltpu.VMEM((M, N), F32)] * n_rhs)


def _wgrad_rows(name, blocks, rhs, comm=None, tk=1024):
    nb, T, M = blocks.shape
    N = rhs.shape[1]
    res, cres = _wgrad(name, blocks, [rhs], pl.BlockSpec((1, tk, M), lambda j, k: (j, k, 0)),
                       pl.BlockSpec((tk, N), lambda j, k: (k, 0)), 1, M, N, nb, T // tk, comm=comm)
    return res[0], cres


def _wgrad_wide(name, lhs, rhs, n_split, kb, tk, comm=None):
    T, M = lhs.shape
    N = rhs.shape[1]
    slab = N // n_split
    nk = T // tk

    def body(l_ref, r_ref, o_ref, acc_ref):
        k = pl.program_id(1)

        @pl.when(k == 0)
        def _():
            acc_ref[...] = jnp.zeros_like(acc_ref)

        acc_ref[...] += lax.dot_general(l_ref[...], r_ref[...], TN_DIMS, preferred_element_type=F32)

        @pl.when(k == nk - 1)
        def _():
            if kb == 0:
                o_ref[...] = acc_ref[...].astype(BF16)
            for b in range(kb):
                o_ref[b] = acc_ref[:, b * (slab // kb):(b + 1) * (slab // kb)].astype(BF16)

    if kb == 0:
        out_spec, out_shape = pl.BlockSpec((M, slab), lambda j, k: (0, j)), (M, N)
    else:
        out_spec, out_shape = pl.BlockSpec((kb, M, slab // kb), lambda j, k: (j, 0, 0)), (n_split * kb, M, slab // kb)
    return _host_call(
        body, name=name, grid=(n_split, nk), comm=comm, args=(lhs, rhs),
        in_specs=[pl.BlockSpec((tk, M), lambda j, k: (k, 0)), pl.BlockSpec((tk, slab), lambda j, k: (k, j))],
        out_specs=[out_spec], out_shape=[jax.ShapeDtypeStruct(out_shape, BF16)],
        scratch_shapes=[pltpu.VMEM((M, slab), F32)])


def _wgrad_out(mixed, dmo, nb=NDEV, tk=1024):
    D = mixed.shape[1]
    res, _ = _wgrad_wide("wgrad_out", mixed, dmo, 2, 0, tk)
    return res[0].reshape(nb, D // nb, D)


def _wgrad_in(h, dproj, comm, nb=NDEV, tk=1024):
    res, cres = _wgrad_wide("wgrad_in", h, dproj, nb // 2, 2, tk, comm=comm)
    return res[0], cres


def _mixer_bwd(dx1, mo, proj, cw, pw, ps, wout, g2, comm, tm=256):
    T, D = dx1.shape
    P = proj.shape[1]
    nt = T // tm
    n_ext = tm + HALO
    hb = tm // HALO

    def body(dx1_ref, mo_ref, proj_ref, hc_ref, hu_ref, hv_ref, cw_ref, pw_ref, ps_ref, wout_ref, g2_ref,
             dmo_ref, dproj_ref, dg2_ref, dcw_ref, dps_ref, dpw_ref, dmix_ref, dconv_carry, q_carry):
        i = pl.program_id(0)
        tile = nt - 1 - i

        @pl.when(i == 0)
        def _():
            dconv_carry[...] = jnp.zeros_like(dconv_carry)
            q_carry[...] = jnp.zeros_like(q_carry)
            dg2_ref[...] = jnp.zeros_like(dg2_ref)
            dcw_ref[...] = jnp.zeros_like(dcw_ref)
            dps_ref[...] = jnp.zeros_like(dps_ref)
            dpw_ref[...] = jnp.zeros_like(dpw_ref)

        mov = mo_ref[...]
        r2 = _rsq(mov)
        n2 = mov * r2
        dx1v = dx1_ref[...]
        dg2_ref[...] += jnp.sum(dx1v * n2, axis=0, keepdims=True)
        dmo = _norm_bwd(dx1v * g2_ref[...], n2, r2).astype(BF16)
        dmo_ref[...] = dmo
        dmix_ref[...] = lax.dot_general(dmo, wout_ref[...], NT_DIMS, preferred_element_type=F32)

        has_prev = (tile > 0).astype(F32)

        for h in range(CONV_HEADS):
            lo = h * HEAD_DIM
            sl = slice(lo, lo + HEAD_DIM)
            gate_b = proj_ref[:, lo:lo + HEAD_DIM].astype(F32)
            gate_c = proj_ref[:, CONV_WIDTH + lo:CONV_WIDTH + lo + HEAD_DIM].astype(F32)
            uu = proj_ref[:, 2 * CONV_WIDTH + lo:2 * CONV_WIDTH + lo + HEAD_DIM].astype(F32)
            cu = gate_c * uu
            ext = jnp.concatenate([hc_ref[:, sl].astype(F32) * hu_ref[:, sl].astype(F32) * has_prev, cu], axis=0)
            c1 = pltpu.roll(ext, 1, 0)[HALO:]
            c2 = pltpu.roll(ext, 2, 0)[HALO:]
            w0, w1, w2 = cw_ref[h, 0:1, :], cw_ref[h, 1:2, :], cw_ref[h, 2:3, :]
            conv = w2 * cu + w1 * c1 + w0 * c2
            ya = gate_b * conv
            ra = _rsq(ya)
            dya = _norm_bwd(dmix_ref[:, sl], ya * ra, ra)
            dconv = dya * gate_b
            dcw_ref[h, 0:1, :] += jnp.sum(dconv * c2, axis=0, keepdims=True)
            dcw_ref[h, 1:2, :] += jnp.sum(dconv * c1, axis=0, keepdims=True)
            dcw_ref[h, 2:3, :] += jnp.sum(dconv * cu, axis=0, keepdims=True)
            extd = jnp.concatenate([dconv, dconv_carry[:, sl]], axis=0)
            d1 = pltpu.roll(extd, n_ext - 1, 0)[:tm]
            d2 = pltpu.roll(extd, n_ext - 2, 0)[:tm]
            dcu = w2 * dconv + w1 * d1 + w0 * d2
            dconv_carry[:, sl] = dconv[:HALO]
            dproj_ref[:, lo:lo + HEAD_DIM] = (dya * conv).astype(BF16)
            dproj_ref[:, CONV_WIDTH + lo:CONV_WIDTH + lo + HEAD_DIM] = (dcu * uu).astype(BF16)
            dproj_ref[:, 2 * CONV_WIDTH + lo:2 * CONV_WIDTH + lo + HEAD_DIM] = (dcu * gate_c).astype(BF16)

        for gi, w in enumerate(POOL_WINDOWS):
            lo = gi * POOL_GROUP_DIM
            sl = slice(lo, lo + POOL_GROUP_DIM)
            v = proj_ref[:, 3 * CONV_WIDTH + lo:3 * CONV_WIDTH + lo + POOL_GROUP_DIM].astype(F32)
            inv = _inv_count(tile * tm, tm, w)
            ext = jnp.concatenate([hv_ref[:, sl].astype(F32) * has_prev, v], axis=0)
            pooled = (_window_sum(ext, w, True)[HALO:] * inv - v).astype(BF16)
            y = jnp.dot(pooled, pw_ref[gi], preferred_element_type=F32)
            rp = _rsq(y)
            nb_ = y * rp
            dyb = dmix_ref[:, CONV_WIDTH + lo:CONV_WIDTH + lo + POOL_GROUP_DIM]
            dps_ref[:, sl] += jnp.sum(dyb * nb_, axis=0, keepdims=True)
            dy = _norm_bwd(dyb * ps_ref[:, sl], nb_, rp).astype(BF16)
            dpw_ref[gi] += lax.dot_general(pooled, dy, TN_DIMS, preferred_element_type=F32)
            dpooled = lax.dot_general(dy, pw_ref[gi], NT_DIMS, preferred_element_type=F32)
            q = dpooled * inv
            extq = jnp.concatenate([q, q_carry[:, sl]], axis=0)
            dv = _window_sum(extq, w, False)[:tm] - dpooled
            q_carry[:, sl] = q[:HALO]
            dproj_ref[:, 3 * CONV_WIDTH + lo:3 * CONV_WIDTH + lo + POOL_GROUP_DIM] = dv.astype(BF16)

    rev = lambda n: pl.BlockSpec((tm, n), lambda i: (nt - 1 - i, 0))

    def halo(col):
        return pl.BlockSpec((HALO, CONV_WIDTH), lambda i: (jnp.maximum((nt - 1 - i) * hb - 1, 0), col))

    return _host_call(
        body, name="mixer_bwd", grid=(nt,), comm=comm, args=(dx1, mo, proj, proj, proj, proj, cw, pw, ps, wout, g2),
        in_specs=[rev(D), rev(D), rev(P), halo(1), halo(2), halo(3), _whole(cw.shape), _whole(pw.shape),
                  _whole(ps.shape), _whole(wout.shape), _whole(g2.shape)],
        out_specs=[rev(D), rev(P), pl.BlockSpec((1, D), lambda i: (0, 0)),
                   pl.BlockSpec(cw.shape, lambda i: (0, 0, 0)), pl.BlockSpec(ps.shape, lambda i: (0, 0)),
                   pl.BlockSpec(pw.shape, lambda i: (0, 0, 0))],
        out_shape=[jax.ShapeDtypeStruct((T, D), BF16), jax.ShapeDtypeStruct((T, P), BF16),
                   jax.ShapeDtypeStruct((1, D), F32), jax.ShapeDtypeStruct(cw.shape, F32),
                   jax.ShapeDtypeStruct(ps.shape, F32), jax.ShapeDtypeStruct(pw.shape, F32)],
        scratch_shapes=[pltpu.VMEM((tm, D), F32), pltpu.VMEM((HALO, CONV_WIDTH), F32),
                        pltpu.VMEM((HALO, CONV_WIDTH), F32)])


def _inproj_bwd(dproj, win, x, dx1, g1, comm, tm=256):
    T, D = x.shape
    nblk, _, bn = win.shape

    def body(dp_ref, w_ref, x_ref, dx1_ref, g1_ref, gx_ref, dg1_ref):
        @pl.when(pl.program_id(0) == 0)
        def _():
            dg1_ref[...] = jnp.zeros_like(dg1_ref)

        dh = lax.dot_general(dp_ref[:, 0:bn], w_ref[0], NT_DIMS, preferred_element_type=F32)
        for k in range(1, nblk):
            dh = dh + lax.dot_general(dp_ref[:, k * bn:(k + 1) * bn], w_ref[k], NT_DIMS,
                                      preferred_element_type=F32)
        xv = x_ref[...]
        r = _rsq(xv)
        n = xv * r
        dg1_ref[...] += jnp.sum(dh * n, axis=0, keepdims=True)
        gx_ref[...] = dx1_ref[...] + _norm_bwd(dh * g1_ref[...], n, r)

    row = pl.BlockSpec((tm, D), lambda i: (i, 0))
    vec = pl.BlockSpec((1, D), lambda i: (0, 0))
    return _host_call(
        body, name="inproj_bwd", grid=(T // tm,), comm=comm, args=(dproj, win, x, dx1, g1),
        in_specs=[pl.BlockSpec((tm, nblk * bn), lambda i: (i, 0)), _whole(win.shape), row, row, vec],
        out_specs=[row, vec],
        out_shape=[jax.ShapeDtypeStruct((T, D), F32), jax.ShapeDtypeStruct((1, D), F32)])


def _adamw(w, g, m, v):
    m = ADAM_B1 * m + (1.0 - ADAM_B1) * g
    v = ADAM_B2 * v + (1.0 - ADAM_B2) * jnp.square(g)
    m_hat = m / (1.0 - ADAM_B1 ** ADAM_STEP)
    v_hat = v / (1.0 - ADAM_B2 ** ADAM_STEP)
    delta = -ADAM_LR * (m_hat / (jnp.sqrt(v_hat) + ADAM_EPS) + ADAM_WD * w)
    return delta, m, v


def _sum_adamw(parts, w, m, v, name, tr):
    r, cd = w.shape

    def body(p_ref, w_ref, m_ref, v_ref, g_ref, d_ref, mo_ref, vo_ref):
        g = p_ref[0].astype(F32)
        for k in range(1, NDEV):
            g = g + p_ref[k].astype(F32)
        g_ref[...] = g
        d_ref[...], mo_ref[...], vo_ref[...] = _adamw(w_ref[...], g, m_ref[...], v_ref[...])

    blk = pl.BlockSpec((tr, cd), lambda i: (i, 0))
    shp = jax.ShapeDtypeStruct((r, cd), F32)
    return pl.pallas_call(
        body, name=name, grid=(r // tr,),
        in_specs=[pl.BlockSpec((NDEV, tr, cd), lambda i: (0, i, 0)), blk, blk, blk],
        out_specs=[blk] * 4, out_shape=[shp] * 4,
        compiler_params=_params("arbitrary"),
    )(parts, w, m, v)


def _small_reduce_adamw(loss_part, vec_grads, dps, dcw, dpw, vec_state, ps_state, cw_state, pw_state):
    D = vec_grads[0].shape[1]
    pw_rows = pw_state[0].shape[1]
    states = list(vec_state) + [ps_state, cw_state, pw_state]
    n_in = 1 + 4 + 3 + 3 * len(states)
    n_out = 1 + 4 * len(states)

    def body(*refs):
        loss_ref, dg = refs[0], refs[1:5]
        dps_ref, dcw_ref, dpw_ref = refs[5:8]
        st = refs[8:n_in]
        loss_out, outs = refs[n_in], refs[n_in + 1:n_in + n_out]
        pack, gat, cbuf, pbuf, send_sems, recv_sems, local_sems = refs[n_in + n_out:]
        x, y, c = _coords()
        me = _device_index((x, y, c))

        pack[...] = jnp.zeros_like(pack)
        for k in range(4):
            pack[k:k + 1, :] = dg[k][...]
        pack[4:5, 0:dps_ref.shape[1]] = dps_ref[...]
        pack[5:6, 0:loss_ref.shape[1]] = loss_ref[...]

        def pw_slice(i):
            return dpw_ref.at[:, pl.ds(i * pw_rows, pw_rows), :]

        mine = [pltpu.make_async_copy(pack, gat.at[me], local_sems.at[0]),
                pltpu.make_async_copy(dcw_ref.at[me], cbuf.at[me], local_sems.at[1]),
                pltpu.make_async_copy(pw_slice(me), pbuf.at[me], local_sems.at[2])]
        for cp in mine:
            cp.start()
        sends, recvs = [], []
        for mask in range(1, NDEV):
            peer = (1 - x if mask & 4 else x, 1 - y if mask & 2 else y, 1 - c if mask & 1 else c)
            p = _device_index(peer)
            for k, (src, buf) in enumerate(((pack, gat), (dcw_ref.at[p], cbuf), (pw_slice(p), pbuf))):
                kw = dict(send_sem=send_sems.at[mask, k], recv_sem=recv_sems.at[mask, k],
                          device_id=peer, device_id_type=MESH)
                sends.append(pltpu.make_async_remote_copy(src_ref=src, dst_ref=buf.at[me], **kw))
                recvs.append(pltpu.make_async_remote_copy(src_ref=src, dst_ref=buf.at[p], **kw))
                sends[-1].start()
        for cp in recvs:
            cp.wait_recv()
        for cp in sends:
            cp.wait_send()
        for cp in mine:
            cp.wait()

        def slot_sum(buf):
            s = buf[0]
            for k in range(1, NDEV):
                s = s + buf[k]
            return s

        vec = slot_sum(gat)
        loss_out[...] = vec[5:6, 0:loss_ref.shape[1]]
        grads = [vec[k:k + 1, :] for k in range(4)] + [vec[4:5, 0:dps_ref.shape[1]], slot_sum(cbuf), slot_sum(pbuf)]
        for k, g in enumerate(grads):
            w_ref, m_ref, v_ref = st[3 * k:3 * k + 3]
            outs[4 * k][...] = g
            outs[4 * k + 1][...], outs[4 * k + 2][...], outs[4 * k + 3][...] = _adamw(
                w_ref[...], g, m_ref[...], v_ref[...])

    flat_state = [a for s in states for a in s]
    out_shape = [jax.ShapeDtypeStruct(loss_part.shape, F32)]
    out_shape += [jax.ShapeDtypeStruct(s[0].shape, F32) for s in states for _ in range(4)]
    return pl.pallas_call(
        body, name="small_reduce_adamw",
        in_specs=[VMEM_SPEC] * n_in, out_specs=[VMEM_SPEC] * n_out, out_shape=out_shape,
        scratch_shapes=[pltpu.VMEM((NDEV, D), F32), pltpu.VMEM((NDEV, NDEV, D), F32),
                        pltpu.VMEM((NDEV,) + cw_state[0].shape, F32), pltpu.VMEM((NDEV,) + pw_state[0].shape, F32),
                        pltpu.SemaphoreType.DMA((NDEV, 3)), pltpu.SemaphoreType.DMA((NDEV, 3)),
                        pltpu.SemaphoreType.DMA((3,))],
        compiler_params=pltpu.CompilerParams(vmem_limit_bytes=VMEM_LIMIT),
    )(loss_part, *vec_grads, dps, dcw, dpw, *flat_state)


ROW_TILE = dict(w_in=512, w_gate=176, w_up=176, w_down=176, w_out=128)
FORWARD_STEP = dict(inproj=56, mixer_fwd=26, ffn_up=32)


def kernel(x, ln_mix_pre, w_in, conv_w, pool_w, pool_scale, w_out, ln_mix_post, ln_ffn_pre, w_gate, w_up, w_down, ln_ffn_post, loss_target, m_ln_mix_pre, m_w_in, m_conv_w, m_pool_w, m_pool_scale, m_w_out, m_ln_mix_post, m_ln_ffn_pre, m_w_gate, m_w_up, m_w_down, m_ln_ffn_post, v_ln_mix_pre, v_w_in, v_conv_w, v_pool_w, v_pool_scale, v_w_out, v_ln_mix_post, v_ln_ffn_pre, v_w_gate, v_w_up, v_w_down, v_ln_ffn_post):
    D = x.shape[2]
    xs, tgt = x[0], loss_target[0]
    flip = lambda arr: jnp.swapaxes(arr[0], 0, 1)
    win, wg, wu, wd, wout, pw, cw = _cast_gather_first(
        [w_in[0], flip(w_gate), flip(w_up), w_down[0], w_out[0], pool_w[0], conv_w[0]])

    (proj, h), (wout, wg) = _inproj(xs, ln_mix_pre, win, _gather_comm([wout, wg], FORWARD_STEP["inproj"]))
    wout2 = wout.reshape(D, D)
    (x1, hf, mixed, mo), (wu,) = _mixer_fwd(proj, xs, cw, pw, pool_scale, wout2, ln_mix_post, ln_ffn_pre,
                                            _gather_comm([wu], FORWARD_STEP["mixer_fwd"]))
    (g, u, a), (wd,) = _ffn_up(hf, wg, wu, _gather_comm([wd], FORWARD_STEP["ffn_up"]))
    dy, dff, loss, dg4 = _ffn_down_loss(a, wd, x1, tgt, ln_ffn_post)

    rows = lambda arr, lo, hi: (lo * arr.shape[1] // 4, (hi - lo) * arr.shape[1] // 4)
    dwd, _ = _wgrad_rows("wgrad_down", a, dff)
    (dg, du), (dwd_parts,) = _ffn_bwd_act(dff, wd, g, u, _scatter_comm([(dwd, None, *rows(dwd, 0, 2))]))
    dwg, (dwd_parts,) = _wgrad_rows("wgrad_gate", dg, hf, _scatter_comm([(dwd, dwd_parts, *rows(dwd, 2, 4))]))
    dwu, (dwg_parts,) = _wgrad_rows("wgrad_up", du, hf, _scatter_comm([(dwg, None, *rows(dwg, 0, 2))]))
    (dx1, dg3), (dwg_parts,), (dwu_parts,) = _ffn_bwd_in(
        dg, du, wg, wu, x1, dy, ln_ffn_pre, _scatter_comm([(dwg, dwg_parts, *rows(dwg, 2, 4))]),
        _scatter_comm([(dwu, None, *rows(dwu, 0, 2))]))
    (dmo, dproj, dg2, dcw, dps, dpw), (dwu_parts,) = _mixer_bwd(
        dx1, mo, proj, cw, pw, pool_scale, wout2, ln_mix_post, _scatter_comm([(dwu, dwu_parts, *rows(dwu, 2, 4))]))
    dwout = _wgrad_out(mixed, dmo)
    dwin, (dwout_parts,) = _wgrad_in(h, dproj, _scatter_comm([(dwout, None, *rows(dwout, 0, 4))]))
    (gx, dg1), (dwin_parts,) = _inproj_bwd(dproj, win, xs, dx1, ln_mix_pre,
                                           _scatter_comm([(dwin, None, *rows(dwin, 0, 4))]))

    res = {}
    for k, parts, w, m, v in (("w_down", dwd_parts, w_down, m_w_down, v_w_down),
                              ("w_out", dwout_parts, w_out, m_w_out, v_w_out),
                              ("w_in", dwin_parts, w_in, m_w_in, v_w_in)):
        res[k] = [o.reshape(w.shape) for o in _sum_adamw(parts, w[0], m[0], v[0], "sum_adamw_" + k, ROW_TILE[k])]
    for k, parts, w, m, v in (("w_gate", dwg_parts, w_gate, m_w_gate, v_w_gate),
                              ("w_up", dwu_parts, w_up, m_w_up, v_w_up)):
        outs = _sum_adamw(parts, flip(w), flip(m), flip(v), "sum_adamw_" + k, ROW_TILE[k])
        res[k] = [jnp.swapaxes(o, 0, 1)[None] for o in outs]

    loss_sum, *small = _small_reduce_adamw(
        loss, [dg1, dg2, dg3, dg4], dps, dcw, dpw,
        [(ln_mix_pre, m_ln_mix_pre, v_ln_mix_pre), (ln_mix_post, m_ln_mix_post, v_ln_mix_post),
         (ln_ffn_pre, m_ln_ffn_pre, v_ln_ffn_pre), (ln_ffn_post, m_ln_ffn_post, v_ln_ffn_post)],
        (pool_scale, m_pool_scale, v_pool_scale), (conv_w[0], m_conv_w[0], v_conv_w[0]),
        (pool_w[0], m_pool_w[0], v_pool_w[0]))
    small_names = ["ln_mix_pre", "ln_mix_post", "ln_ffn_pre", "ln_ffn_post", "pool_scale", "conv_w", "pool_w"]
    shapes = dict(conv_w=conv_w.shape, pool_w=pool_w.shape)
    for i, k in enumerate(small_names):
        res[k] = [o.reshape(shapes[k]) if k in shapes else o for o in small[4 * i:4 * i + 4]]

    order = ["ln_mix_pre", "w_in", "conv_w", "pool_w", "pool_scale", "w_out", "ln_mix_post", "ln_ffn_pre",
             "w_gate", "w_up", "w_down", "ln_ffn_post"]
    return (loss_sum[0, 0], gx[None], *[res[k][0] for k in order], *[res[k][1] for k in order],
            *[res[k][2] for k in order], *[res[k][3] for k in order])
```

```python
import functools
from typing import Any, NamedTuple

import jax
import jax.numpy as jnp
from jax import lax
from jax.experimental import pallas as pl
from jax.experimental.pallas import tpu as pltpu

EPS = 1e-6
NDEV = 8
CONV_HEADS = 8
HEAD_DIM = 128
CONV_WIDTH = CONV_HEADS * HEAD_DIM
POOL_WINDOWS = (2, 4, 8, 16)
POOL_GROUP_DIM = 256
HALO = 16

ADAM_LR = 0.001
ADAM_B1 = 0.9
ADAM_B2 = 0.999
ADAM_EPS = 1e-08
ADAM_WD = 0.01
ADAM_STEP = 10

F32 = jnp.float32
BF16 = jnp.bfloat16
VMEM_LIMIT = 58 * 1024 * 1024
MESH = pl.DeviceIdType.MESH
HBM_SPEC = pl.BlockSpec(memory_space=pl.ANY)
VMEM_SPEC = pl.BlockSpec(memory_space=pltpu.VMEM)

NT_DIMS = (((1,), (1,)), ((), ()))
TN_DIMS = (((0,), (0,)), ((), ()))


def _params(*sem):
    return pltpu.CompilerParams(dimension_semantics=sem, vmem_limit_bytes=VMEM_LIMIT)


def _rsq(v):
    return lax.rsqrt(jnp.mean(v * v, axis=-1, keepdims=True) + EPS)


def _norm_bwd(dn, n, r):
    return r * (dn - n * jnp.mean(dn * n, axis=-1, keepdims=True))


def _whole(shape):
    nd = len(shape)
    return pl.BlockSpec(shape, lambda *_: (0,) * nd, pipeline_mode=pl.Buffered(1))


def _inv_count(t0, tm, w):
    t = t0 + lax.broadcasted_iota(jnp.int32, (tm, 1), 0)
    return 1.0 / jnp.minimum(t + 1, w).astype(F32)


def _window_sum(ext, w, back):
    n = ext.shape[0]
    s, shift = ext, 1
    while shift < w:
        s = s + pltpu.roll(s, shift if back else n - shift, 0)
        shift *= 2
    return s


class _Comm(NamedTuple):
    arrays: Any
    out_shape: Any
    aliases: Any
    scratch: Any
    hooks: Any


def _coords():
    return lax.axis_index("x"), lax.axis_index("y"), lax.axis_index("c")


def _other_chips(x, y):
    return [(1 - x, y), (x, 1 - y), (1 - x, 1 - y)]


def _device_index(dev):
    return 4 * dev[0] + 2 * dev[1] + dev[2]


def _host_call(body, *, name, grid, in_specs, out_specs, out_shape, args, scratch_shapes=(), comm=None):
    sem = ("arbitrary",) * len(grid)
    in_specs, out_specs, out_shape, scratch_shapes = list(in_specs), list(out_specs), list(out_shape), list(scratch_shapes)
    if comm is None:
        res = pl.pallas_call(body, name=name, grid=grid, in_specs=in_specs, out_specs=out_specs, out_shape=out_shape,
                             scratch_shapes=scratch_shapes, compiler_params=_params(*sem))(*args)
        return res, []
    n_in, n_out, n_scr = len(in_specs), len(out_specs), len(scratch_shapes)
    n_cin, n_cout = len(comm.arrays), len(comm.out_shape)
    total = functools.reduce(lambda a, b: a * b, grid)

    def wrapped(*refs):
        ins, cin = refs[:n_in], refs[n_in:n_in + n_cin]
        o0 = n_in + n_cin
        outs, cout = refs[o0:o0 + n_out], refs[o0 + n_out:o0 + n_out + n_cout]
        s0 = o0 + n_out + n_cout
        scr, sems = refs[s0:s0 + n_scr], refs[s0 + n_scr:]
        step = pl.program_id(0)
        for d in range(1, len(grid)):
            step = step * grid[d] + pl.program_id(d)
        for when, before, fn in comm.hooks:
            if before:
                pl.when(step == when % total)(functools.partial(fn, cin, cout, sems))
        body(*ins, *outs, *scr)
        for when, before, fn in comm.hooks:
            if not before:
                pl.when(step == when % total)(functools.partial(fn, cin, cout, sems))

    res = pl.pallas_call(
        wrapped, name=name, grid=grid,
        in_specs=in_specs + [HBM_SPEC] * n_cin, out_specs=out_specs + [HBM_SPEC] * n_cout,
        out_shape=out_shape + list(comm.out_shape), scratch_shapes=scratch_shapes + list(comm.scratch),
        input_output_aliases={n_in + i: n_out + o for i, o in comm.aliases.items()},
        compiler_params=_params(*sem),
    )(*args, *comm.arrays)
    return res[:n_out], res[n_out:]


def _gather_steps(n, view, own_src, send_sems, recv_sems):
    x, y, c = _coords()
    me, sibling = (x, y, c), (x, y, 1 - c)
    chips = _other_chips(x, y)

    def copy(a, k, block, to, src=None):
        return pltpu.make_async_remote_copy(
            src_ref=view(a, block) if src is None else src, dst_ref=view(a, block),
            send_sem=send_sems.at[a, k], recv_sem=recv_sems.at[a, k], device_id=to, device_id_type=MESH)

    def first_copies():
        cps = []
        for a in range(n):
            cps.append(copy(a, 0, me, sibling, src=own_src(a)))
            cps += [copy(a, 1 + j, me, (*chip, c), src=own_src(a)) for j, chip in enumerate(chips)]
        return cps

    def passed_copies():
        return [copy(a, 4 + j, (*chip, c), sibling) for j, chip in enumerate(chips) for a in range(n)]

    def first():
        for cp in first_copies():
            cp.start()

    def forward():
        for j, chip in enumerate(chips):
            for a in range(n):
                copy(a, 1 + j, (*chip, c), me).wait_recv()
                copy(a, 4 + j, (*chip, c), sibling).start()

    def finish():
        for a in range(n):
            copy(a, 0, sibling, me).wait_recv()
            for j, chip in enumerate(chips):
                copy(a, 4 + j, (*chip, 1 - c), me).wait_recv()
        for cp in first_copies() + passed_copies():
            cp.wait_send()

    return first, forward, finish


def _gather_comm(arrays, forward_step):
    n = len(arrays)

    def steps(cout, sems):
        view = lambda a, dev: cout[a].at[_device_index(dev)]
        return _gather_steps(n, view, lambda a: view(a, _coords()), sems[0], sems[1])

    hooks = [(0, True, lambda cin, cout, sems: steps(cout, sems)[0]()),
             (forward_step, True, lambda cin, cout, sems: steps(cout, sems)[1]()),
             (-1, False, lambda cin, cout, sems: steps(cout, sems)[2]())]
    return _Comm(list(arrays), [jax.ShapeDtypeStruct(a.shape, a.dtype) for a in arrays], {i: i for i in range(n)},
                 [pltpu.SemaphoreType.DMA((n, 7)), pltpu.SemaphoreType.DMA((n, 7))], hooks)


def _scatter_comm(items):
    n = len(items)
    arrays, aliases, grad_at = [], {}, []
    for k, (grad, slots, _, _) in enumerate(items):
        grad_at.append(len(arrays))
        arrays.append(grad)
        if slots is not None:
            aliases[len(arrays)] = k
            arrays.append(slots)

    def copies(cin, cout, sems):
        send_sems, recv_sems, local_sems = sems
        x, y, c = _coords()
        me = _device_index((x, y, c))
        src = lambda a, p: cin[grad_at[a]].at[p, pl.ds(items[a][2], items[a][3]), :]
        dst = lambda a, p: cout[a].at[p, pl.ds(items[a][2], items[a][3]), :]
        mine = [pltpu.make_async_copy(src(a, me), dst(a, me), local_sems.at[a]) for a in range(n)]
        sends, recvs = [], []
        for a in range(n):
            for mask in range(1, NDEV):
                peer = (1 - x if mask & 4 else x, 1 - y if mask & 2 else y, 1 - c if mask & 1 else c)
                p = _device_index(peer)
                kw = dict(send_sem=send_sems.at[a, mask - 1], recv_sem=recv_sems.at[a, mask - 1],
                          device_id=peer, device_id_type=MESH)
                sends.append(pltpu.make_async_remote_copy(src_ref=src(a, p), dst_ref=dst(a, me), **kw))
                recvs.append(pltpu.make_async_remote_copy(src_ref=src(a, p), dst_ref=dst(a, p), **kw))
        return mine, sends, recvs

    def start(cin, cout, sems):
        mine, sends, _ = copies(cin, cout, sems)
        for cp in mine + sends:
            cp.start()

    def finish(cin, cout, sems):
        mine, sends, recvs = copies(cin, cout, sems)
        for cp in recvs:
            cp.wait_recv()
        for cp in sends:
            cp.wait_send()
        for cp in mine:
            cp.wait()

    return _Comm(arrays, [jax.ShapeDtypeStruct(it[0].shape, it[0].dtype) for it in items], aliases,
                 [pltpu.SemaphoreType.DMA((n, NDEV - 1)), pltpu.SemaphoreType.DMA((n, NDEV - 1)),
                  pltpu.SemaphoreType.DMA((n,))],
                 [(0, True, start), (-1, False, finish)])


NOW_ITEMS = (0, 5, 6)
FLIPPED_ITEMS = (1, 2)
POOL_ITEM = 5


def _cast_gather_first(shards):
    n = len(shards)
    dtypes = [BF16] * 6 + [F32]
    block_shapes = [s.shape[::-1] if a in FLIPPED_ITEMS else s.shape for a, s in enumerate(shards)]
    out_shapes = [(NDEV,) + s for s in block_shapes]
    g, rows, cols = shards[POOL_ITEM].shape
    out_shapes[POOL_ITEM] = (g, rows * NDEV, cols)

    later = [a for a in range(n) if a not in NOW_ITEMS]

    def body(*refs):
        ins, outs, raw, stage = refs[:n], refs[n:2 * n], refs[2 * n:3 * n], refs[3 * n:4 * n]
        send_sems, recv_sems, local_sems, load_sems = refs[4 * n:]

        def view(a, dev):
            i = _device_index(dev)
            if a == POOL_ITEM:
                return outs[a].at[:, pl.ds(i * rows, rows), :]
            return outs[a].at[i]

        loads = [pltpu.make_async_copy(ins[a], raw[a], load_sems.at[a]) for a in range(n)]
        mine = [pltpu.make_async_copy(stage[a], view(a, _coords()), local_sems.at[a]) for a in range(n)]
        for a in list(NOW_ITEMS) + later:
            loads[a].start()
        first, forward, finish = _gather_steps(
            len(NOW_ITEMS), lambda k, dev: view(NOW_ITEMS[k], dev), lambda k: stage[NOW_ITEMS[k]], send_sems, recv_sems)
        for a in list(NOW_ITEMS) + later:
            loads[a].wait()
            if a in FLIPPED_ITEMS:
                k = raw[a].shape[0]
                eye = (lax.broadcasted_iota(jnp.int32, (k, k), 0) == lax.broadcasted_iota(jnp.int32, (k, k), 1))
                stage[a][...] = lax.dot_general(raw[a][...].astype(BF16), eye.astype(BF16), TN_DIMS,
                                                preferred_element_type=F32).astype(BF16)
            else:
                stage[a][...] = raw[a][...].astype(dtypes[a])
            mine[a].start()
            if a == NOW_ITEMS[-1]:
                first()
        forward()
        finish()
        for cp in mine:
            cp.wait()

    return pl.pallas_call(
        body, name="cast_gather_first",
        in_specs=[HBM_SPEC] * n, out_specs=[HBM_SPEC] * n,
        out_shape=[jax.ShapeDtypeStruct(s, d) for s, d in zip(out_shapes, dtypes)],
        scratch_shapes=[pltpu.VMEM(s.shape, s.dtype) for s in shards]
        + [pltpu.VMEM(s, d) for s, d in zip(block_shapes, dtypes)]
        + [pltpu.SemaphoreType.DMA((len(NOW_ITEMS), 7)), pltpu.SemaphoreType.DMA((len(NOW_ITEMS), 7)),
           pltpu.SemaphoreType.DMA((n,)), pltpu.SemaphoreType.DMA((n,))],
        compiler_params=pltpu.CompilerParams(vmem_limit_bytes=VMEM_LIMIT),
    )(*shards)


def _inproj(x, g1, win, comm, tm=1024):
    T, D = x.shape
    nb, _, bn = win.shape

    def body(x_ref, g_ref, w_ref, proj_ref, h_ref):
        @pl.when(pl.program_id(1) == 0)
        def _():
            xv = x_ref[...]
            h_ref[...] = (xv * _rsq(xv) * g_ref[...]).astype(BF16)

        proj_ref[...] = jnp.dot(h_ref[...], w_ref[0], preferred_element_type=F32).astype(BF16)

    return _host_call(
        body, name="inproj", grid=(T // tm, nb), comm=comm, args=(x, g1, win),
        in_specs=[pl.BlockSpec((tm, D), lambda i, j: (i, 0)),
                  pl.BlockSpec((1, D), lambda i, j: (0, 0)),
                  pl.BlockSpec((1, D, bn), lambda i, j: (j, 0, 0))],
        out_specs=[pl.BlockSpec((tm, bn), lambda i, j: (i, j)),
                   pl.BlockSpec((tm, D), lambda i, j: (i, 0))],
        out_shape=[jax.ShapeDtypeStruct((T, nb * bn), BF16), jax.ShapeDtypeStruct((T, D), BF16)])


def _mixer_fwd(proj, x, cw, pw, ps, wout, g2, g3, comm, tm=256):
    T, D = x.shape
    P = proj.shape[1]

    def body(proj_ref, x_ref, cw_ref, pw_ref, ps_ref, wout_ref, g2_ref, g3_ref,
             x1_ref, hf_ref, mixed_ref, mo_ref, cu_carry, v_carry):
        i = pl.program_id(0)

        @pl.when(i == 0)
        def _():
            cu_carry[...] = jnp.zeros_like(cu_carry)
            v_carry[...] = jnp.zeros_like(v_carry)

        for h in range(CONV_HEADS):
            lo = h * HEAD_DIM
            gate_b = proj_ref[:, lo:lo + HEAD_DIM].astype(F32)
            cu = proj_ref[:, CONV_WIDTH + lo:CONV_WIDTH + lo + HEAD_DIM].astype(F32) * \
                proj_ref[:, 2 * CONV_WIDTH + lo:2 * CONV_WIDTH + lo + HEAD_DIM].astype(F32)
            ext = jnp.concatenate([cu_carry[:, lo:lo + HEAD_DIM], cu], axis=0)
            c1 = pltpu.roll(ext, 1, 0)[HALO:]
            c2 = pltpu.roll(ext, 2, 0)[HALO:]
            ya = gate_b * (cw_ref[h, 2:3, :] * cu + cw_ref[h, 1:2, :] * c1 + cw_ref[h, 0:1, :] * c2)
            mixed_ref[:, lo:lo + HEAD_DIM] = (ya * _rsq(ya)).astype(BF16)
            cu_carry[:, lo:lo + HEAD_DIM] = cu[tm - HALO:]

        for gi, w in enumerate(POOL_WINDOWS):
            lo = gi * POOL_GROUP_DIM
            v = proj_ref[:, 3 * CONV_WIDTH + lo:3 * CONV_WIDTH + lo + POOL_GROUP_DIM].astype(F32)
            ext = jnp.concatenate([v_carry[:, lo:lo + POOL_GROUP_DIM], v], axis=0)
            pooled = _window_sum(ext, w, True)[HALO:] * _inv_count(i * tm, tm, w) - v
            y = jnp.dot(pooled.astype(BF16), pw_ref[gi], preferred_element_type=F32)
            yb = y * _rsq(y) * ps_ref[:, lo:lo + POOL_GROUP_DIM]
            mixed_ref[:, CONV_WIDTH + lo:CONV_WIDTH + lo + POOL_GROUP_DIM] = yb.astype(BF16)
            v_carry[:, lo:lo + POOL_GROUP_DIM] = v[tm - HALO:]

        mo = jnp.dot(mixed_ref[...], wout_ref[...], preferred_element_type=F32)
        mo_ref[...] = mo
        x1 = x_ref[...] + mo * _rsq(mo) * g2_ref[...]
        x1_ref[...] = x1
        hf_ref[...] = (x1 * _rsq(x1) * g3_ref[...]).astype(BF16)

    row = lambda n: pl.BlockSpec((tm, n), lambda i: (i, 0))
    return _host_call(
        body, name="mixer_fwd", grid=(T // tm,), comm=comm, args=(proj, x, cw, pw, ps, wout, g2, g3),
        in_specs=[row(P), row(D), _whole(cw.shape), _whole(pw.shape), _whole(ps.shape),
                  _whole(wout.shape), _whole(g2.shape), _whole(g3.shape)],
        out_specs=[row(D), row(D), row(D), row(D)],
        out_shape=[jax.ShapeDtypeStruct((T, D), F32), jax.ShapeDtypeStruct((T, D), BF16),
                   jax.ShapeDtypeStruct((T, D), BF16), jax.ShapeDtypeStruct((T, D), F32)],
        scratch_shapes=[pltpu.VMEM((HALO, CONV_WIDTH), F32), pltpu.VMEM((HALO, CONV_WIDTH), F32)])


def _ffn_up(hf, wg, wu, comm, tm=1024):
    T, D = hf.shape
    nb, _, bf = wg.shape

    def body(hf_ref, wg_ref, wu_ref, g_ref, u_ref, a_ref):
        hv = hf_ref[...]
        g = jnp.dot(hv, wg_ref[0], preferred_element_type=F32)
        u = jnp.dot(hv, wu_ref[0], preferred_element_type=F32)
        g_ref[0] = g.astype(BF16)
        u_ref[0] = u.astype(BF16)
        a_ref[0] = (g * jax.nn.sigmoid(g) * u).astype(BF16)

    wspec = pl.BlockSpec((1, D, bf), lambda i, j: (j, 0, 0))
    ospec = pl.BlockSpec((1, tm, bf), lambda i, j: (j, i, 0))
    oshape = jax.ShapeDtypeStruct((nb, T, bf), BF16)
    return _host_call(
        body, name="ffn_up", grid=(T // tm, nb), comm=comm, args=(hf, wg, wu),
        in_specs=[pl.BlockSpec((tm, D), lambda i, j: (i, 0)), wspec, wspec],
        out_specs=[ospec, ospec, ospec], out_shape=[oshape, oshape, oshape])


def _ffn_down_loss(a, wd, x1, tgt, g4, tm=256):
    nblk, T, bf = a.shape
    D = x1.shape[1]
    nt = T // tm

    def body(a_ref, wd_ref, x1_ref, tgt_ref, g4_ref, dy_ref, dff_ref, loss_ref, dg4_ref, lacc_ref):
        i = pl.program_id(0)

        @pl.when(i == 0)
        def _():
            lacc_ref[...] = jnp.zeros_like(lacc_ref)
            dg4_ref[...] = jnp.zeros_like(dg4_ref)

        ff = jnp.dot(a_ref[0], wd_ref[0], preferred_element_type=F32)
        for k in range(1, nblk):
            ff = ff + jnp.dot(a_ref[k], wd_ref[k], preferred_element_type=F32)
        r = _rsq(ff)
        n = ff * r
        g4v = g4_ref[...]
        e = x1_ref[...] + n * g4v - tgt_ref[...]
        lacc_ref[...] += jnp.sum(e * e, axis=0, keepdims=True)
        dy = e * (1.0 / D)
        dy_ref[...] = dy
        dg4_ref[...] += jnp.sum(dy * n, axis=0, keepdims=True)
        dff_ref[...] = _norm_bwd(dy * g4v, n, r).astype(BF16)

        @pl.when(i == nt - 1)
        def _():
            loss_ref[...] = jnp.full(loss_ref.shape, (0.5 / D) * jnp.sum(lacc_ref[...]), F32)

    row = pl.BlockSpec((tm, D), lambda i: (i, 0))
    vec = pl.BlockSpec((1, D), lambda i: (0, 0))
    return _host_call(
        body, name="ffn_down_loss", grid=(nt,), args=(a, wd, x1, tgt, g4),
        in_specs=[pl.BlockSpec((nblk, tm, bf), lambda i: (0, i, 0)), _whole(wd.shape), row, row, vec],
        out_specs=[row, row, pl.BlockSpec((1, 128), lambda i: (0, 0)), vec],
        out_shape=[jax.ShapeDtypeStruct((T, D), F32), jax.ShapeDtypeStruct((T, D), BF16),
                   jax.ShapeDtypeStruct((1, 128), F32), jax.ShapeDtypeStruct((1, D), F32)],
        scratch_shapes=[pltpu.VMEM((1, D), F32)])[0]


def _ffn_bwd_act(dff, wd, g, u, comm, tm=1024):
    T, D = dff.shape
    nb, bf, _ = wd.shape

    def body(dff_ref, wd_ref, g_ref, u_ref, dg_ref, du_ref):
        da = lax.dot_general(dff_ref[...], wd_ref[0], NT_DIMS, preferred_element_type=F32)
        gv = g_ref[0].astype(F32)
        s = jax.nn.sigmoid(gv)
        du_ref[0] = (da * (gv * s)).astype(BF16)
        dg_ref[0] = (da * u_ref[0].astype(F32) * (s * (1.0 + gv * (1.0 - s)))).astype(BF16)

    blk = pl.BlockSpec((1, tm, bf), lambda i, j: (j, i, 0))
    oshape = jax.ShapeDtypeStruct((nb, T, bf), BF16)
    return _host_call(
        body, name="ffn_bwd_act", grid=(T // tm, nb), comm=comm, args=(dff, wd, g, u),
        in_specs=[pl.BlockSpec((tm, D), lambda i, j: (i, 0)),
                  pl.BlockSpec((1, bf, D), lambda i, j: (j, 0, 0)), blk, blk],
        out_specs=[blk, blk], out_shape=[oshape, oshape])


def _ffn_bwd_in(dg, du, wg, wu, x1, dy, g3, comm_a, comm_b, tm=256):
    nb, T, bf = dg.shape
    D = x1.shape[1]
    hb = nb // 2

    def partial_sum(dg_ref, du_ref, wg_ref, wu_ref):
        s = lax.dot_general(dg_ref[0], wg_ref[0], NT_DIMS, preferred_element_type=F32)
        s = s + lax.dot_general(du_ref[0], wu_ref[0], NT_DIMS, preferred_element_type=F32)
        for k in range(1, hb):
            s = s + lax.dot_general(dg_ref[k], wg_ref[k], NT_DIMS, preferred_element_type=F32)
            s = s + lax.dot_general(du_ref[k], wu_ref[k], NT_DIMS, preferred_element_type=F32)
        return s

    def first(dg_ref, du_ref, wg_ref, wu_ref, part_ref):
        part_ref[...] = partial_sum(dg_ref, du_ref, wg_ref, wu_ref)

    def second(dg_ref, du_ref, wg_ref, wu_ref, part_ref, x1_ref, dy_ref, g3_ref, dx1_ref, dg3_ref):
        @pl.when(pl.program_id(0) == 0)
        def _():
            dg3_ref[...] = jnp.zeros_like(dg3_ref)

        dhf = part_ref[...] + partial_sum(dg_ref, du_ref, wg_ref, wu_ref)
        x1v = x1_ref[...]
        r = _rsq(x1v)
        n = x1v * r
        dg3_ref[...] += jnp.sum(dhf * n, axis=0, keepdims=True)
        dx1_ref[...] = dy_ref[...] + _norm_bwd(dhf * g3_ref[...], n, r)

    row = pl.BlockSpec((tm, D), lambda i: (i, 0))
    vec = pl.BlockSpec((1, D), lambda i: (0, 0))
    rowshape = jax.ShapeDtypeStruct((T, D), F32)

    def specs(half):
        ablk = pl.BlockSpec((hb, tm, bf), lambda i: (half, i, 0))
        wblk = pl.BlockSpec((hb, D, bf), lambda i: (half, 0, 0), pipeline_mode=pl.Buffered(1))
        return [ablk, ablk, wblk, wblk]

    (part,), c0 = _host_call(first, name="ffn_bwd_in_a", grid=(T // tm,), comm=comm_a, args=(dg, du, wg, wu),
                             in_specs=specs(0), out_specs=[row], out_shape=[rowshape])
    res, c1 = _host_call(second, name="ffn_bwd_in_b", grid=(T // tm,), comm=comm_b,
                         args=(dg, du, wg, wu, part, x1, dy, g3),
                         in_specs=specs(1) + [row, row, row, vec], out_specs=[row, vec],
                         out_shape=[rowshape, jax.ShapeDtypeStruct((1, D), F32)])
    return res, c0, c1


def _wgrad(name, lhs, rhs, lhs_spec, rhs_spec, n_rhs, M, N, nb, nk, comm=None):
    def body(*refs):
        l_ref, r_refs = refs[0], refs[1:1 + n_rhs]
        o_refs, acc_refs = refs[1 + n_rhs:1 + 2 * n_rhs], refs[1 + 2 * n_rhs:]
        k = pl.program_id(1)
        tile = lambda ref: ref[0] if len(ref.shape) == 3 else ref[...]

        @pl.when(k == 0)
        def _():
            for acc_ref in acc_refs:
                acc_ref[...] = jnp.zeros_like(acc_ref)

        for r_ref, acc_ref in zip(r_refs, acc_refs):
            acc_ref[...] += lax.dot_general(tile(l_ref), tile(r_ref), TN_DIMS, preferred_element_type=F32)

        @pl.when(k == nk - 1)
        def _():
            for o_ref, acc_ref in zip(o_refs, acc_refs):
                o_ref[0] = acc_ref[...].astype(BF16)

    oblk = pl.BlockSpec((1, M, N), lambda j, k: (j, 0, 0))
    oshape = jax.ShapeDtypeStruct((nb, M, N), BF16)
    return _host_call(
        body, name=name, grid=(nb, nk), comm=comm, args=(lhs, *rhs),
        in_specs=[lhs_spec] + [rhs_spec] * n_rhs, out_specs=[oblk] * n_rhs, out_shape=[oshape] * n_rhs,
        scratch_shapes=[pltpu.VMEM((M, N), F32)] * n_rhs)


WGRAD_TOKENS = 2048


def _wgrad_rows(name, blocks, rhs, comm=None):
    nb, T, M = blocks.shape
    N = rhs.shape[1]
    tk = min(WGRAD_TOKENS, T)
    res, cres = _wgrad(name, blocks, [rhs], pl.BlockSpec((1, tk, M), lambda j, k: (j, k, 0)),
                       pl.BlockSpec((tk, N), lambda j, k: (k, 0)), 1, M, N, nb, T // tk, comm=comm)
    return res[0], cres


def _wgrad_wide(name, lhs, rhs, n_split, kb, comm=None):
    T, M = lhs.shape
    N = rhs.shape[1]
    slab = N // n_split
    tk = min(WGRAD_TOKENS, T)
    nk = T // tk

    def body(l_ref, r_ref, o_ref, acc_ref):
        k = pl.program_id(1)

        @pl.when(k == 0)
        def _():
            acc_ref[...] = jnp.zeros_like(acc_ref)

        acc_ref[...] += lax.dot_general(l_ref[...], r_ref[...], TN_DIMS, preferred_element_type=F32)

        @pl.when(k == nk - 1)
        def _():
            if kb == 0:
                o_ref[...] = acc_ref[...].astype(BF16)
            for b in range(kb):
                o_ref[b] = acc_ref[:, b * (slab // kb):(b + 1) * (slab // kb)].astype(BF16)

    if kb == 0:
        out_spec, out_shape = pl.BlockSpec((M, slab), lambda j, k: (0, j)), (M, N)
    else:
        out_spec, out_shape = pl.BlockSpec((kb, M, slab // kb), lambda j, k: (j, 0, 0)), (n_split * kb, M, slab // kb)
    return _host_call(
        body, name=name, grid=(n_split, nk), comm=comm, args=(lhs, rhs),
        in_specs=[pl.BlockSpec((tk, M), lambda j, k: (k, 0)), pl.BlockSpec((tk, slab), lambda j, k: (k, j))],
        out_specs=[out_spec], out_shape=[jax.ShapeDtypeStruct(out_shape, BF16)],
        scratch_shapes=[pltpu.VMEM((M, slab), F32)])


def _wgrad_out(mixed, dmo, comm, nb=NDEV):
    D = mixed.shape[1]
    res, cres = _wgrad_wide("wgrad_out", mixed, dmo, 2, 0, comm=comm)
    return res[0].reshape(nb, D // nb, D), cres


def _wgrad_in(h, dproj, comm, nb=NDEV):
    res, cres = _wgrad_wide("wgrad_in", h, dproj, nb // 2, 2, comm=comm)
    return res[0], cres


def _mixer_bwd(dx1, mo, proj, cw, pw, ps, wout, g2, comm, tm=256):
    T, D = dx1.shape
    P = proj.shape[1]
    nt = T // tm
    n_ext = tm + HALO
    hb = tm // HALO

    def body(dx1_ref, mo_ref, proj_ref, hc_ref, hu_ref, hv_ref, cw_ref, pw_ref, ps_ref, wout_ref, g2_ref,
             dmo_ref, dproj_ref, dg2_ref, dcw_ref, dps_ref, dpw_ref, dmix_ref, dconv_carry, q_carry):
        i = pl.program_id(0)
        tile = nt - 1 - i

        @pl.when(i == 0)
        def _():
            dconv_carry[...] = jnp.zeros_like(dconv_carry)
            q_carry[...] = jnp.zeros_like(q_carry)
            dg2_ref[...] = jnp.zeros_like(dg2_ref)
            dcw_ref[...] = jnp.zeros_like(dcw_ref)
            dps_ref[...] = jnp.zeros_like(dps_ref)
            dpw_ref[...] = jnp.zeros_like(dpw_ref)

        mov = mo_ref[...]
        r2 = _rsq(mov)
        n2 = mov * r2
        dx1v = dx1_ref[...]
        dg2_ref[...] += jnp.sum(dx1v * n2, axis=0, keepdims=True)
        dmo = _norm_bwd(dx1v * g2_ref[...], n2, r2).astype(BF16)
        dmo_ref[...] = dmo
        dmix_ref[...] = lax.dot_general(dmo, wout_ref[...], NT_DIMS, preferred_element_type=F32)

        has_prev = (tile > 0).astype(F32)

        for h in range(CONV_HEADS):
            lo = h * HEAD_DIM
            sl = slice(lo, lo + HEAD_DIM)
            gate_b = proj_ref[:, lo:lo + HEAD_DIM].astype(F32)
            gate_c = proj_ref[:, CONV_WIDTH + lo:CONV_WIDTH + lo + HEAD_DIM].astype(F32)
            uu = proj_ref[:, 2 * CONV_WIDTH + lo:2 * CONV_WIDTH + lo + HEAD_DIM].astype(F32)
            cu = gate_c * uu
            ext = jnp.concatenate([hc_ref[:, sl].astype(F32) * hu_ref[:, sl].astype(F32) * has_prev, cu], axis=0)
            c1 = pltpu.roll(ext, 1, 0)[HALO:]
            c2 = pltpu.roll(ext, 2, 0)[HALO:]
            w0, w1, w2 = cw_ref[h, 0:1, :], cw_ref[h, 1:2, :], cw_ref[h, 2:3, :]
            conv = w2 * cu + w1 * c1 + w0 * c2
            ya = gate_b * conv
            ra = _rsq(ya)
            dya = _norm_bwd(dmix_ref[:, sl], ya * ra, ra)
            dconv = dya * gate_b
            dcw_ref[h, 0:1, :] += jnp.sum(dconv * c2, axis=0, keepdims=True)
            dcw_ref[h, 1:2, :] += jnp.sum(dconv * c1, axis=0, keepdims=True)
            dcw_ref[h, 2:3, :] += jnp.sum(dconv * cu, axis=0, keepdims=True)
            extd = jnp.concatenate([dconv, dconv_carry[:, sl]], axis=0)
            d1 = pltpu.roll(extd, n_ext - 1, 0)[:tm]
            d2 = pltpu.roll(extd, n_ext - 2, 0)[:tm]
            dcu = w2 * dconv + w1 * d1 + w0 * d2
            dconv_carry[:, sl] = dconv[:HALO]
            dproj_ref[:, lo:lo + HEAD_DIM] = (dya * conv).astype(BF16)
            dproj_ref[:, CONV_WIDTH + lo:CONV_WIDTH + lo + HEAD_DIM] = (dcu * uu).astype(BF16)
            dproj_ref[:, 2 * CONV_WIDTH + lo:2 * CONV_WIDTH + lo + HEAD_DIM] = (dcu * gate_c).astype(BF16)

        for gi, w in enumerate(POOL_WINDOWS):
            lo = gi * POOL_GROUP_DIM
            sl = slice(lo, lo + POOL_GROUP_DIM)
            v = proj_ref[:, 3 * CONV_WIDTH + lo:3 * CONV_WIDTH + lo + POOL_GROUP_DIM].astype(F32)
            inv = _inv_count(tile * tm, tm, w)
            ext = jnp.concatenate([hv_ref[:, sl].astype(F32) * has_prev, v], axis=0)
            pooled = (_window_sum(ext, w, True)[HALO:] * inv - v).astype(BF16)
            y = jnp.dot(pooled, pw_ref[gi], preferred_element_type=F32)
            rp = _rsq(y)
            nb_ = y * rp
            dyb = dmix_ref[:, CONV_WIDTH + lo:CONV_WIDTH + lo + POOL_GROUP_DIM]
            dps_ref[:, sl] += jnp.sum(dyb * nb_, axis=0, keepdims=True)
            dy = _norm_bwd(dyb * ps_ref[:, sl], nb_, rp).astype(BF16)
            dpw_ref[gi] += lax.dot_general(pooled, dy, TN_DIMS, preferred_element_type=F32)
            dpooled = lax.dot_general(dy, pw_ref[gi], NT_DIMS, preferred_element_type=F32)
            q = dpooled * inv
            extq = jnp.concatenate([q, q_carry[:, sl]], axis=0)
            dv = _window_sum(extq, w, False)[:tm] - dpooled
            q_carry[:, sl] = q[:HALO]
            dproj_ref[:, 3 * CONV_WIDTH + lo:3 * CONV_WIDTH + lo + POOL_GROUP_DIM] = dv.astype(BF16)

    rev = lambda n: pl.BlockSpec((tm, n), lambda i: (nt - 1 - i, 0))

    def halo(col):
        return pl.BlockSpec((HALO, CONV_WIDTH), lambda i: (jnp.maximum((nt - 1 - i) * hb - 1, 0), col))

    return _host_call(
        body, name="mixer_bwd", grid=(nt,), comm=comm, args=(dx1, mo, proj, proj, proj, proj, cw, pw, ps, wout, g2),
        in_specs=[rev(D), rev(D), rev(P), halo(1), halo(2), halo(3), _whole(cw.shape), _whole(pw.shape),
                  _whole(ps.shape), _whole(wout.shape), _whole(g2.shape)],
        out_specs=[rev(D), rev(P), pl.BlockSpec((1, D), lambda i: (0, 0)),
                   pl.BlockSpec(cw.shape, lambda i: (0, 0, 0)), pl.BlockSpec(ps.shape, lambda i: (0, 0)),
                   pl.BlockSpec(pw.shape, lambda i: (0, 0, 0))],
        out_shape=[jax.ShapeDtypeStruct((T, D), BF16), jax.ShapeDtypeStruct((T, P), BF16),
                   jax.ShapeDtypeStruct((1, D), F32), jax.ShapeDtypeStruct(cw.shape, F32),
                   jax.ShapeDtypeStruct(ps.shape, F32), jax.ShapeDtypeStruct(pw.shape, F32)],
        scratch_shapes=[pltpu.VMEM((tm, D), F32), pltpu.VMEM((HALO, CONV_WIDTH), F32),
                        pltpu.VMEM((HALO, CONV_WIDTH), F32)])


def _inproj_bwd(dproj, win, x, dx1, g1, comm, tm=256):
    T, D = x.shape
    nblk, _, bn = win.shape

    def body(dp_ref, w_ref, x_ref, dx1_ref, g1_ref, gx_ref, dg1_ref):
        @pl.when(pl.program_id(0) == 0)
        def _():
            dg1_ref[...] = jnp.zeros_like(dg1_ref)

        dh = lax.dot_general(dp_ref[:, 0:bn], w_ref[0], NT_DIMS, preferred_element_type=F32)
        for k in range(1, nblk):
            dh = dh + lax.dot_general(dp_ref[:, k * bn:(k + 1) * bn], w_ref[k], NT_DIMS,
                                      preferred_element_type=F32)
        xv = x_ref[...]
        r = _rsq(xv)
        n = xv * r
        dg1_ref[...] += jnp.sum(dh * n, axis=0, keepdims=True)
        gx_ref[...] = dx1_ref[...] + _norm_bwd(dh * g1_ref[...], n, r)

    row = pl.BlockSpec((tm, D), lambda i: (i, 0))
    vec = pl.BlockSpec((1, D), lambda i: (0, 0))
    return _host_call(
        body, name="inproj_bwd", grid=(T // tm,), comm=comm, args=(dproj, win, x, dx1, g1),
        in_specs=[pl.BlockSpec((tm, nblk * bn), lambda i: (i, 0)), _whole(win.shape), row, row, vec],
        out_specs=[row, vec],
        out_shape=[jax.ShapeDtypeStruct((T, D), F32), jax.ShapeDtypeStruct((1, D), F32)])


def _adamw(w, g, m, v):
    m = ADAM_B1 * m + (1.0 - ADAM_B1) * g
    v = ADAM_B2 * v + (1.0 - ADAM_B2) * jnp.square(g)
    m_hat = m / (1.0 - ADAM_B1 ** ADAM_STEP)
    v_hat = v / (1.0 - ADAM_B2 ** ADAM_STEP)
    delta = -ADAM_LR * (m_hat / (jnp.sqrt(v_hat) + ADAM_EPS) + ADAM_WD * w)
    return delta, m, v


def _sum_adamw(parts, w, m, v, name, tr):
    r, cd = w.shape

    def body(p_ref, w_ref, m_ref, v_ref, g_ref, d_ref, mo_ref, vo_ref):
        g = p_ref[0].astype(F32)
        for k in range(1, NDEV):
            g = g + p_ref[k].astype(F32)
        g_ref[...] = g
        d_ref[...], mo_ref[...], vo_ref[...] = _adamw(w_ref[...], g, m_ref[...], v_ref[...])

    blk = pl.BlockSpec((tr, cd), lambda i: (i, 0))
    shp = jax.ShapeDtypeStruct((r, cd), F32)
    return pl.pallas_call(
        body, name=name, grid=(r // tr,),
        in_specs=[pl.BlockSpec((NDEV, tr, cd), lambda i: (0, i, 0)), blk, blk, blk],
        out_specs=[blk] * 4, out_shape=[shp] * 4,
        compiler_params=_params("arbitrary"),
    )(parts, w, m, v)


def _small_reduce_adamw(loss_part, vec_grads, dps, dcw, dpw, vec_state, ps_state, cw_state, pw_state):
    D = vec_grads[0].shape[1]
    pw_rows = pw_state[0].shape[1]
    states = list(vec_state) + [ps_state, cw_state, pw_state]
    n_in = 1 + 4 + 3 + 3 * len(states)
    n_out = 1 + 4 * len(states)

    def body(*refs):
        loss_ref, dg = refs[0], refs[1:5]
        dps_ref, dcw_ref, dpw_ref = refs[5:8]
        st = refs[8:n_in]
        loss_out, outs = refs[n_in], refs[n_in + 1:n_in + n_out]
        pack, gat, cbuf, pbuf, send_sems, recv_sems, local_sems = refs[n_in + n_out:]
        x, y, c = _coords()
        me = _device_index((x, y, c))

        pack[...] = jnp.zeros_like(pack)
        for k in range(4):
            pack[k:k + 1, :] = dg[k][...]
        pack[4:5, 0:dps_ref.shape[1]] = dps_ref[...]
        pack[5:6, 0:loss_ref.shape[1]] = loss_ref[...]

        def pw_slice(i):
            return dpw_ref.at[:, pl.ds(i * pw_rows, pw_rows), :]

        mine = [pltpu.make_async_copy(pack, gat.at[me], local_sems.at[0]),
                pltpu.make_async_copy(dcw_ref.at[me], cbuf.at[me], local_sems.at[1]),
                pltpu.make_async_copy(pw_slice(me), pbuf.at[me], local_sems.at[2])]
        for cp in mine:
            cp.start()
        sends, recvs = [], []
        for mask in range(1, NDEV):
            peer = (1 - x if mask & 4 else x, 1 - y if mask & 2 else y, 1 - c if mask & 1 else c)
            p = _device_index(peer)
            for k, (src, buf) in enumerate(((pack, gat), (dcw_ref.at[p], cbuf), (pw_slice(p), pbuf))):
                kw = dict(send_sem=send_sems.at[mask, k], recv_sem=recv_sems.at[mask, k],
                          device_id=peer, device_id_type=MESH)
                sends.append(pltpu.make_async_remote_copy(src_ref=src, dst_ref=buf.at[me], **kw))
                recvs.append(pltpu.make_async_remote_copy(src_ref=src, dst_ref=buf.at[p], **kw))
                sends[-1].start()
        for cp in recvs:
            cp.wait_recv()
        for cp in sends:
            cp.wait_send()
        for cp in mine:
            cp.wait()

        def slot_sum(buf):
            s = buf[0]
            for k in range(1, NDEV):
                s = s + buf[k]
            return s

        vec = slot_sum(gat)
        loss_out[...] = vec[5:6, 0:loss_ref.shape[1]]
        grads = [vec[k:k + 1, :] for k in range(4)] + [vec[4:5, 0:dps_ref.shape[1]], slot_sum(cbuf), slot_sum(pbuf)]
        for k, g in enumerate(grads):
            w_ref, m_ref, v_ref = st[3 * k:3 * k + 3]
            outs[4 * k][...] = g
            outs[4 * k + 1][...], outs[4 * k + 2][...], outs[4 * k + 3][...] = _adamw(
                w_ref[...], g, m_ref[...], v_ref[...])

    flat_state = [a for s in states for a in s]
    out_shape = [jax.ShapeDtypeStruct(loss_part.shape, F32)]
    out_shape += [jax.ShapeDtypeStruct(s[0].shape, F32) for s in states for _ in range(4)]
    return pl.pallas_call(
        body, name="small_reduce_adamw",
        in_specs=[VMEM_SPEC] * n_in, out_specs=[VMEM_SPEC] * n_out, out_shape=out_shape,
        scratch_shapes=[pltpu.VMEM((NDEV, D), F32), pltpu.VMEM((NDEV, NDEV, D), F32),
                        pltpu.VMEM((NDEV,) + cw_state[0].shape, F32), pltpu.VMEM((NDEV,) + pw_state[0].shape, F32),
                        pltpu.SemaphoreType.DMA((NDEV, 3)), pltpu.SemaphoreType.DMA((NDEV, 3)),
                        pltpu.SemaphoreType.DMA((3,))],
        compiler_params=pltpu.CompilerParams(vmem_limit_bytes=VMEM_LIMIT),
    )(loss_part, *vec_grads, dps, dcw, dpw, *flat_state)


ROW_TILE = dict(w_in=512, w_gate=176, w_up=176, w_down=176, w_out=128)
FORWARD_STEP = dict(inproj=56, mixer_fwd=26, ffn_up=32)


def kernel(x, ln_mix_pre, w_in, conv_w, pool_w, pool_scale, w_out, ln_mix_post, ln_ffn_pre, w_gate, w_up, w_down, ln_ffn_post, loss_target, m_ln_mix_pre, m_w_in, m_conv_w, m_pool_w, m_pool_scale, m_w_out, m_ln_mix_post, m_ln_ffn_pre, m_w_gate, m_w_up, m_w_down, m_ln_ffn_post, v_ln_mix_pre, v_w_in, v_conv_w, v_pool_w, v_pool_scale, v_w_out, v_ln_mix_post, v_ln_ffn_pre, v_w_gate, v_w_up, v_w_down, v_ln_ffn_post):
    D = x.shape[2]
    xs, tgt = x[0], loss_target[0]
    flip = lambda arr: jnp.swapaxes(arr[0], 0, 1)
    win, wg, wu, wd, wout, pw, cw = _cast_gather_first(
        [w_in[0], flip(w_gate), flip(w_up), w_down[0], w_out[0], pool_w[0], conv_w[0]])

    (proj, h), (wout, wg) = _inproj(xs, ln_mix_pre, win, _gather_comm([wout, wg], FORWARD_STEP["inproj"]))
    wout2 = wout.reshape(D, D)
    (x1, hf, mixed, mo), (wu,) = _mixer_fwd(proj, xs, cw, pw, pool_scale, wout2, ln_mix_post, ln_ffn_pre,
                                            _gather_comm([wu], FORWARD_STEP["mixer_fwd"]))
    (g, u, a), (wd,) = _ffn_up(hf, wg, wu, _gather_comm([wd], FORWARD_STEP["ffn_up"]))
    dy, dff, loss, dg4 = _ffn_down_loss(a, wd, x1, tgt, ln_ffn_post)

    rows = lambda arr, lo, hi: (lo * arr.shape[1] // 4, (hi - lo) * arr.shape[1] // 4)
    dwd, _ = _wgrad_rows("wgrad_down", a, dff)
    (dg, du), (dwd_parts,) = _ffn_bwd_act(dff, wd, g, u, _scatter_comm([(dwd, None, *rows(dwd, 0, 2))]))
    dwg, (dwd_parts,) = _wgrad_rows("wgrad_gate", dg, hf, _scatter_comm([(dwd, dwd_parts, *rows(dwd, 2, 4))]))
    dwu, (dwg_parts,) = _wgrad_rows("wgrad_up", du, hf, _scatter_comm([(dwg, None, *rows(dwg, 0, 2))]))
    (dx1, dg3), (dwg_parts,), (dwu_parts,) = _ffn_bwd_in(
        dg, du, wg, wu, x1, dy, ln_ffn_pre, _scatter_comm([(dwg, dwg_parts, *rows(dwg, 2, 4))]),
        _scatter_comm([(dwu, None, *rows(dwu, 0, 2))]))
    (dmo, dproj, dg2, dcw, dps, dpw), (dwu_parts,) = _mixer_bwd(
        dx1, mo, proj, cw, pw, pool_scale, wout2, ln_mix_post, _scatter_comm([(dwu, dwu_parts, *rows(dwu, 2, 4))]))
    dwout, _ = _wgrad_out(mixed, dmo, None)
    dwin, (dwout_parts,) = _wgrad_in(h, dproj, _scatter_comm([(dwout, None, *rows(dwout, 0, 4))]))
    (gx, dg1), (dwin_parts,) = _inproj_bwd(dproj, win, xs, dx1, ln_mix_pre,
                                           _scatter_comm([(dwin, None, *rows(dwin, 0, 4))]))

    res = {}
    for k, parts, w, m, v in (("w_down", dwd_parts, w_down, m_w_down, v_w_down),
                              ("w_out", dwout_parts, w_out, m_w_out, v_w_out),
                              ("w_in", dwin_parts, w_in, m_w_in, v_w_in)):
        res[k] = [o.reshape(w.shape) for o in _sum_adamw(parts, w[0], m[0], v[0], "sum_adamw_" + k, ROW_TILE[k])]
    for k, parts, w, m, v in (("w_gate", dwg_parts, w_gate, m_w_gate, v_w_gate),
                              ("w_up", dwu_parts, w_up, m_w_up, v_w_up)):
        outs = _sum_adamw(parts, flip(w), flip(m), flip(v), "sum_adamw_" + k, ROW_TILE[k])
        res[k] = [jnp.swapaxes(o, 0, 1)[None] for o in outs]

    loss_sum, *small = _small_reduce_adamw(
        loss, [dg1, dg2, dg3, dg4], dps, dcw, dpw,
        [(ln_mix_pre, m_ln_mix_pre, v_ln_mix_pre), (ln_mix_post, m_ln_mix_post, v_ln_mix_post),
         (ln_ffn_pre, m_ln_ffn_pre, v_ln_ffn_pre), (ln_ffn_post, m_ln_ffn_post, v_ln_ffn_post)],
        (pool_scale, m_pool_scale, v_pool_scale), (conv_w[0], m_conv_w[0], v_conv_w[0]),
        (pool_w[0], m_pool_w[0], v_pool_w[0]))
    small_names = ["ln_mix_pre", "ln_mix_post", "ln_ffn_pre", "ln_ffn_post", "pool_scale", "conv_w", "pool_w"]
    shapes = dict(conv_w=conv_w.shape, pool_w=pool_w.shape)
    for i, k in enumerate(small_names):
        res[k] = [o.reshape(shapes[k]) if k in shapes else o for o in small[4 * i:4 * i + 4]]

    order = ["ln_mix_pre", "w_in", "conv_w", "pool_w", "pool_scale", "w_out", "ln_mix_post", "ln_ffn_pre",
             "w_gate", "w_up", "w_down", "ln_ffn_post"]
    return (loss_sum[0, 0], gx[None], *[res[k][0] for k in order], *[res[k][1] for k in order],
            *[res[k][2] for k in order], *[res[k][3] for k in order])
```

```python
import functools
from typing import Any, NamedTuple

import jax
import jax.numpy as jnp
from jax import lax
from jax.experimental import pallas as pl
from jax.experimental.pallas import tpu as pltpu

EPS = 1e-6
NDEV = 8
CONV_HEADS = 8
HEAD_DIM = 128
CONV_WIDTH = CONV_HEADS * HEAD_DIM
POOL_WINDOWS = (2, 4, 8, 16)
POOL_GROUP_DIM = 256
HALO = 16

ADAM_LR = 0.001
ADAM_B1 = 0.9
ADAM_B2 = 0.999
ADAM_EPS = 1e-08
ADAM_WD = 0.01
ADAM_STEP = 10

F32 = jnp.float32
BF16 = jnp.bfloat16
VMEM_LIMIT = 58 * 1024 * 1024
MESH = pl.DeviceIdType.MESH
HBM_SPEC = pl.BlockSpec(memory_space=pl.ANY)
VMEM_SPEC = pl.BlockSpec(memory_space=pltpu.VMEM)

NT_DIMS = (((1,), (1,)), ((), ()))
TN_DIMS = (((0,), (0,)), ((), ()))


def _params(*sem):
    return pltpu.CompilerParams(dimension_semantics=sem, vmem_limit_bytes=VMEM_LIMIT)


def _rsq(v):
    return lax.rsqrt(jnp.mean(v * v, axis=-1, keepdims=True) + EPS)


def _norm_bwd(dn, n, r):
    return r * (dn - n * jnp.mean(dn * n, axis=-1, keepdims=True))


def _whole(shape):
    nd = len(shape)
    return pl.BlockSpec(shape, lambda *_: (0,) * nd, pipeline_mode=pl.Buffered(1))


def _inv_count(t0, tm, w):
    t = t0 + lax.broadcasted_iota(jnp.int32, (tm, 1), 0)
    return 1.0 / jnp.minimum(t + 1, w).astype(F32)


def _window_sum(ext, w, back):
    n = ext.shape[0]
    s, shift = ext, 1
    while shift < w:
        s = s + pltpu.roll(s, shift if back else n - shift, 0)
        shift *= 2
    return s


class _Comm(NamedTuple):
    arrays: Any
    out_shape: Any
    aliases: Any
    scratch: Any
    hooks: Any


def _merge_comm(a, b):
    n_in, n_out, n_sem = len(a.arrays), len(a.out_shape), len(a.scratch)
    head = lambda fn: (lambda cin, cout, sems: fn(cin[:n_in], cout[:n_out], sems[:n_sem]))
    tail = lambda fn: (lambda cin, cout, sems: fn(cin[n_in:], cout[n_out:], sems[n_sem:]))
    aliases = dict(a.aliases)
    aliases.update({n_in + i: n_out + o for i, o in b.aliases.items()})
    hooks = [(w, before, head(fn)) for w, before, fn in a.hooks] + [(w, before, tail(fn)) for w, before, fn in b.hooks]
    return _Comm(list(a.arrays) + list(b.arrays), list(a.out_shape) + list(b.out_shape), aliases,
                 list(a.scratch) + list(b.scratch), hooks)


def _coords():
    return lax.axis_index("x"), lax.axis_index("y"), lax.axis_index("c")


def _other_chips(x, y):
    return [(1 - x, y), (x, 1 - y), (1 - x, 1 - y)]


def _device_index(dev):
    return 4 * dev[0] + 2 * dev[1] + dev[2]


def _host_call(body, *, name, grid, in_specs, out_specs, out_shape, args, scratch_shapes=(), comm=None, fill=()):
    sem = ("arbitrary",) * len(grid)
    in_specs, out_specs, out_shape, scratch_shapes = list(in_specs), list(out_specs), list(out_shape), list(scratch_shapes)
    if comm is None and not fill:
        res = pl.pallas_call(body, name=name, grid=grid, in_specs=in_specs, out_specs=out_specs, out_shape=out_shape,
                             scratch_shapes=scratch_shapes, compiler_params=_params(*sem))(*args)
        return res, []
    comm = comm or _Comm([], [], {}, [], [])
    n_in, n_out, n_scr, n_fill = len(in_specs), len(out_specs), len(scratch_shapes), len(fill)
    n_cin, n_cout = len(comm.arrays), len(comm.out_shape)
    total = functools.reduce(lambda a, b: a * b, grid)

    def wrapped(*refs):
        ins, cin = refs[:n_in], refs[n_in + n_fill:n_in + n_fill + n_cin]
        o0 = n_in + n_fill + n_cin
        outs, cout = refs[o0:o0 + n_out], refs[o0 + n_out:o0 + n_out + n_cout]
        s0 = o0 + n_out + n_cout
        scr, sems = refs[s0:s0 + n_scr], refs[s0 + n_scr:]
        step = pl.program_id(0)
        for d in range(1, len(grid)):
            step = step * grid[d] + pl.program_id(d)
        for when, before, fn in comm.hooks:
            if before:
                pl.when(step == when % total)(functools.partial(fn, cin, cout, sems))
        body(*ins, *outs, *scr)
        for when, before, fn in comm.hooks:
            if not before:
                pl.when(step == when % total)(functools.partial(fn, cin, cout, sems))

    aliases = {n_in + k: k for k in range(n_fill)}
    aliases.update({n_in + n_fill + i: n_out + o for i, o in comm.aliases.items()})
    res = pl.pallas_call(
        wrapped, name=name, grid=grid,
        in_specs=in_specs + [HBM_SPEC] * (n_fill + n_cin), out_specs=out_specs + [HBM_SPEC] * n_cout,
        out_shape=out_shape + list(comm.out_shape), scratch_shapes=scratch_shapes + list(comm.scratch),
        input_output_aliases=aliases, compiler_params=_params(*sem),
    )(*args, *fill, *comm.arrays)
    return res[:n_out], res[n_out:]


def _gather_steps(n, view, own_src, send_sems, recv_sems):
    x, y, c = _coords()
    me, sibling = (x, y, c), (x, y, 1 - c)
    chips = _other_chips(x, y)

    def copy(a, k, block, to, src=None):
        return pltpu.make_async_remote_copy(
            src_ref=view(a, block) if src is None else src, dst_ref=view(a, block),
            send_sem=send_sems.at[a, k], recv_sem=recv_sems.at[a, k], device_id=to, device_id_type=MESH)

    def first_copies():
        cps = []
        for a in range(n):
            cps.append(copy(a, 0, me, sibling, src=own_src(a)))
            cps += [copy(a, 1 + j, me, (*chip, c), src=own_src(a)) for j, chip in enumerate(chips)]
        return cps

    def passed_copies():
        return [copy(a, 4 + j, (*chip, c), sibling) for j, chip in enumerate(chips) for a in range(n)]

    def first():
        for cp in first_copies():
            cp.start()

    def forward():
        for j, chip in enumerate(chips):
            for a in range(n):
                copy(a, 1 + j, (*chip, c), me).wait_recv()
                copy(a, 4 + j, (*chip, c), sibling).start()

    def finish():
        for a in range(n):
            copy(a, 0, sibling, me).wait_recv()
            for j, chip in enumerate(chips):
                copy(a, 4 + j, (*chip, 1 - c), me).wait_recv()
        for cp in first_copies() + passed_copies():
            cp.wait_send()

    return first, forward, finish


def _half_gather_steps(n, view, send_sems, recv_sems, core):
    x, y, c = _coords()
    me, sibling = (x, y, c), (x, y, 1 - c)
    chips = _other_chips(x, y)
    sends = c == core

    def copy(a, k, block, to):
        return pltpu.make_async_remote_copy(
            src_ref=view(a, block), dst_ref=view(a, block),
            send_sem=send_sems.at[a, k], recv_sem=recv_sems.at[a, k], device_id=to, device_id_type=MESH)

    def first():
        @pl.when(sends)
        def _():
            for a in range(n):
                copy(a, 0, me, sibling).start()
                for j, chip in enumerate(chips):
                    copy(a, 1 + j, me, (*chip, c)).start()

    def forward():
        @pl.when(sends)
        def _():
            for j, chip in enumerate(chips):
                for a in range(n):
                    copy(a, 1 + j, (*chip, c), me).wait_recv()
                    copy(a, 4 + j, (*chip, c), sibling).start()

    def finish():
        @pl.when(sends)
        def _():
            for a in range(n):
                copy(a, 0, me, sibling).wait_send()
                for j, chip in enumerate(chips):
                    copy(a, 1 + j, me, (*chip, c)).wait_send()
                    copy(a, 4 + j, (*chip, c), sibling).wait_send()

        @pl.when(jnp.logical_not(sends))
        def _():
            for a in range(n):
                copy(a, 0, sibling, me).wait_recv()
                for j, chip in enumerate(chips):
                    copy(a, 4 + j, (*chip, 1 - c), me).wait_recv()

    return first, forward, finish


def _gather_comm(arrays, forward_step, core=None):
    n = len(arrays)

    def steps(cout, sems):
        view = lambda a, dev: cout[a].at[_device_index(dev)]
        if core is not None:
            return _half_gather_steps(n, view, sems[0], sems[1], core)
        return _gather_steps(n, view, lambda a: view(a, _coords()), sems[0], sems[1])

    hooks = [(0, True, lambda cin, cout, sems: steps(cout, sems)[0]()),
             (forward_step, True, lambda cin, cout, sems: steps(cout, sems)[1]()),
             (-1, False, lambda cin, cout, sems: steps(cout, sems)[2]())]
    return _Comm(list(arrays), [jax.ShapeDtypeStruct(a.shape, a.dtype) for a in arrays], {i: i for i in range(n)},
                 [pltpu.SemaphoreType.DMA((n, 7)), pltpu.SemaphoreType.DMA((n, 7))], hooks)


def _scatter_comm(items):
    n = len(items)
    arrays, aliases, grad_at = [], {}, []
    for k, (grad, slots, _, _) in enumerate(items):
        grad_at.append(len(arrays))
        arrays.append(grad)
        if slots is not None:
            aliases[len(arrays)] = k
            arrays.append(slots)

    def copies(cin, cout, sems):
        send_sems, recv_sems, local_sems = sems
        x, y, c = _coords()
        me = _device_index((x, y, c))
        src = lambda a, p: cin[grad_at[a]].at[p, pl.ds(items[a][2], items[a][3]), :]
        dst = lambda a, p: cout[a].at[p, pl.ds(items[a][2], items[a][3]), :]
        mine = [pltpu.make_async_copy(src(a, me), dst(a, me), local_sems.at[a]) for a in range(n)]
        sends, recvs = [], []
        for a in range(n):
            for mask in range(1, NDEV):
                peer = (1 - x if mask & 4 else x, 1 - y if mask & 2 else y, 1 - c if mask & 1 else c)
                p = _device_index(peer)
                kw = dict(send_sem=send_sems.at[a, mask - 1], recv_sem=recv_sems.at[a, mask - 1],
                          device_id=peer, device_id_type=MESH)
                sends.append(pltpu.make_async_remote_copy(src_ref=src(a, p), dst_ref=dst(a, me), **kw))
                recvs.append(pltpu.make_async_remote_copy(src_ref=src(a, p), dst_ref=dst(a, p), **kw))
        return mine, sends, recvs

    def start(cin, cout, sems):
        mine, sends, _ = copies(cin, cout, sems)
        for cp in mine + sends:
            cp.start()

    def finish(cin, cout, sems):
        mine, sends, recvs = copies(cin, cout, sems)
        for cp in recvs:
            cp.wait_recv()
        for cp in sends:
            cp.wait_send()
        for cp in mine:
            cp.wait()

    return _Comm(arrays, [jax.ShapeDtypeStruct(it[0].shape, it[0].dtype) for it in items], aliases,
                 [pltpu.SemaphoreType.DMA((n, NDEV - 1)), pltpu.SemaphoreType.DMA((n, NDEV - 1)),
                  pltpu.SemaphoreType.DMA((n,))],
                 [(0, True, start), (-1, False, finish)])


NOW_ITEMS = (0, 5, 6)
FLIPPED_ITEMS = (1, 2)
POOL_ITEM = 5


def _cast_gather_first(shards):
    n = len(shards)
    dtypes = [BF16] * 6 + [F32]
    block_shapes = [s.shape[::-1] if a in FLIPPED_ITEMS else s.shape for a, s in enumerate(shards)]
    out_shapes = [(NDEV,) + s for s in block_shapes]
    g, rows, cols = shards[POOL_ITEM].shape
    out_shapes[POOL_ITEM] = (g, rows * NDEV, cols)

    later = [a for a in range(n) if a not in NOW_ITEMS]

    def body(*refs):
        ins, outs, raw, stage = refs[:n], refs[n:2 * n], refs[2 * n:3 * n], refs[3 * n:4 * n]
        send_sems, recv_sems, local_sems, load_sems = refs[4 * n:]

        def view(a, dev):
            i = _device_index(dev)
            if a == POOL_ITEM:
                return outs[a].at[:, pl.ds(i * rows, rows), :]
            return outs[a].at[i]

        loads = [pltpu.make_async_copy(ins[a], raw[a], load_sems.at[a]) for a in range(n)]
        mine = [pltpu.make_async_copy(stage[a], view(a, _coords()), local_sems.at[a]) for a in range(n)]
        for a in list(NOW_ITEMS) + later:
            loads[a].start()
        first, forward, finish = _gather_steps(
            len(NOW_ITEMS), lambda k, dev: view(NOW_ITEMS[k], dev), lambda k: stage[NOW_ITEMS[k]], send_sems, recv_sems)
        for a in list(NOW_ITEMS) + later:
            loads[a].wait()
            if a in FLIPPED_ITEMS:
                k = raw[a].shape[0]
                eye = (lax.broadcasted_iota(jnp.int32, (k, k), 0) == lax.broadcasted_iota(jnp.int32, (k, k), 1))
                stage[a][...] = lax.dot_general(raw[a][...].astype(BF16), eye.astype(BF16), TN_DIMS,
                                                preferred_element_type=F32).astype(BF16)
            else:
                stage[a][...] = raw[a][...].astype(dtypes[a])
            mine[a].start()
            if a == NOW_ITEMS[-1]:
                first()
        forward()
        finish()
        for cp in mine:
            cp.wait()

    return pl.pallas_call(
        body, name="cast_gather_first",
        in_specs=[HBM_SPEC] * n, out_specs=[HBM_SPEC] * n,
        out_shape=[jax.ShapeDtypeStruct(s, d) for s, d in zip(out_shapes, dtypes)],
        scratch_shapes=[pltpu.VMEM(s.shape, s.dtype) for s in shards]
        + [pltpu.VMEM(s, d) for s, d in zip(block_shapes, dtypes)]
        + [pltpu.SemaphoreType.DMA((len(NOW_ITEMS), 7)), pltpu.SemaphoreType.DMA((len(NOW_ITEMS), 7)),
           pltpu.SemaphoreType.DMA((n,)), pltpu.SemaphoreType.DMA((n,))],
        compiler_params=pltpu.CompilerParams(vmem_limit_bytes=VMEM_LIMIT),
    )(*shards)


def _inproj(x, g1, win, comm, tm=1024):
    T, D = x.shape
    nb, _, bn = win.shape

    def body(x_ref, g_ref, w_ref, proj_ref, h_ref):
        @pl.when(pl.program_id(1) == 0)
        def _():
            xv = x_ref[...]
            h_ref[...] = (xv * _rsq(xv) * g_ref[...]).astype(BF16)

        proj_ref[...] = jnp.dot(h_ref[...], w_ref[0], preferred_element_type=F32).astype(BF16)

    return _host_call(
        body, name="inproj", grid=(T // tm, nb), comm=comm, args=(x, g1, win),
        in_specs=[pl.BlockSpec((tm, D), lambda i, j: (i, 0)),
                  pl.BlockSpec((1, D), lambda i, j: (0, 0)),
                  pl.BlockSpec((1, D, bn), lambda i, j: (j, 0, 0))],
        out_specs=[pl.BlockSpec((tm, bn), lambda i, j: (i, j)),
                   pl.BlockSpec((tm, D), lambda i, j: (i, 0))],
        out_shape=[jax.ShapeDtypeStruct((T, nb * bn), BF16), jax.ShapeDtypeStruct((T, D), BF16)])


def _mixer_fwd(proj, x, cw, pw, ps, wout, g2, g3, comm, tm=256):
    T, D = x.shape
    P = proj.shape[1]

    def body(proj_ref, x_ref, cw_ref, pw_ref, ps_ref, wout_ref, g2_ref, g3_ref,
             x1_ref, hf_ref, mixed_ref, mo_ref, cu_carry, v_carry):
        i = pl.program_id(0)

        @pl.when(i == 0)
        def _():
            cu_carry[...] = jnp.zeros_like(cu_carry)
            v_carry[...] = jnp.zeros_like(v_carry)

        for h in range(CONV_HEADS):
            lo = h * HEAD_DIM
            gate_b = proj_ref[:, lo:lo + HEAD_DIM].astype(F32)
            cu = proj_ref[:, CONV_WIDTH + lo:CONV_WIDTH + lo + HEAD_DIM].astype(F32) * \
                proj_ref[:, 2 * CONV_WIDTH + lo:2 * CONV_WIDTH + lo + HEAD_DIM].astype(F32)
            ext = jnp.concatenate([cu_carry[:, lo:lo + HEAD_DIM], cu], axis=0)
            c1 = pltpu.roll(ext, 1, 0)[HALO:]
            c2 = pltpu.roll(ext, 2, 0)[HALO:]
            ya = gate_b * (cw_ref[h, 2:3, :] * cu + cw_ref[h, 1:2, :] * c1 + cw_ref[h, 0:1, :] * c2)
            mixed_ref[:, lo:lo + HEAD_DIM] = (ya * _rsq(ya)).astype(BF16)
            cu_carry[:, lo:lo + HEAD_DIM] = cu[tm - HALO:]

        for gi, w in enumerate(POOL_WINDOWS):
            lo = gi * POOL_GROUP_DIM
            v = proj_ref[:, 3 * CONV_WIDTH + lo:3 * CONV_WIDTH + lo + POOL_GROUP_DIM].astype(F32)
            ext = jnp.concatenate([v_carry[:, lo:lo + POOL_GROUP_DIM], v], axis=0)
            pooled = _window_sum(ext, w, True)[HALO:] * _inv_count(i * tm, tm, w) - v
            y = jnp.dot(pooled.astype(BF16), pw_ref[gi], preferred_element_type=F32)
            yb = y * _rsq(y) * ps_ref[:, lo:lo + POOL_GROUP_DIM]
            mixed_ref[:, CONV_WIDTH + lo:CONV_WIDTH + lo + POOL_GROUP_DIM] = yb.astype(BF16)
            v_carry[:, lo:lo + POOL_GROUP_DIM] = v[tm - HALO:]

        mo = jnp.dot(mixed_ref[...], wout_ref[...], preferred_element_type=F32)
        mo_ref[...] = mo
        x1 = x_ref[...] + mo * _rsq(mo) * g2_ref[...]
        x1_ref[...] = x1
        hf_ref[...] = (x1 * _rsq(x1) * g3_ref[...]).astype(BF16)

    row = lambda n: pl.BlockSpec((tm, n), lambda i: (i, 0))
    return _host_call(
        body, name="mixer_fwd", grid=(T // tm,), comm=comm, args=(proj, x, cw, pw, ps, wout, g2, g3),
        in_specs=[row(P), row(D), _whole(cw.shape), _whole(pw.shape), _whole(ps.shape),
                  _whole(wout.shape), _whole(g2.shape), _whole(g3.shape)],
        out_specs=[row(D), row(D), row(D), row(D)],
        out_shape=[jax.ShapeDtypeStruct((T, D), F32), jax.ShapeDtypeStruct((T, D), BF16),
                   jax.ShapeDtypeStruct((T, D), BF16), jax.ShapeDtypeStruct((T, D), F32)],
        scratch_shapes=[pltpu.VMEM((HALO, CONV_WIDTH), F32), pltpu.VMEM((HALO, CONV_WIDTH), F32)])


def _ffn_up(hf, wg, wu, comm, core, fill=(), tm=1024):
    T, D = hf.shape
    nb, _, bf = wg.shape

    def body(hf_ref, wg_ref, wu_ref, g_ref, u_ref, a_ref):
        hv = hf_ref[...]
        g = jnp.dot(hv, wg_ref[0], preferred_element_type=F32)
        u = jnp.dot(hv, wu_ref[0], preferred_element_type=F32)
        g_ref[0] = g.astype(BF16)
        u_ref[0] = u.astype(BF16)
        a_ref[0] = (g * jax.nn.sigmoid(g) * u).astype(BF16)

    wspec = pl.BlockSpec((1, D, bf), lambda i, j: (2 * j + core, 0, 0))
    ospec = pl.BlockSpec((1, tm, bf), lambda i, j: (2 * j + core, i, 0))
    oshape = jax.ShapeDtypeStruct((nb, T, bf), BF16)
    return _host_call(
        body, name=f"ffn_up_{core}", grid=(T // tm, nb // 2), comm=comm, fill=fill, args=(hf, wg, wu),
        in_specs=[pl.BlockSpec((tm, D), lambda i, j: (i, 0)), wspec, wspec],
        out_specs=[ospec, ospec, ospec], out_shape=[oshape, oshape, oshape])


def _ffn_down_loss(a, wd, x1, tgt, g4, tm=256):
    nblk, T, bf = a.shape
    D = x1.shape[1]
    nt = T // tm

    def body(a_ref, wd_ref, x1_ref, tgt_ref, g4_ref, dy_ref, dff_ref, loss_ref, dg4_ref, lacc_ref):
        i = pl.program_id(0)

        @pl.when(i == 0)
        def _():
            lacc_ref[...] = jnp.zeros_like(lacc_ref)
            dg4_ref[...] = jnp.zeros_like(dg4_ref)

        ff = jnp.dot(a_ref[0], wd_ref[0], preferred_element_type=F32)
        for k in range(1, nblk):
            ff = ff + jnp.dot(a_ref[k], wd_ref[k], preferred_element_type=F32)
        r = _rsq(ff)
        n = ff * r
        g4v = g4_ref[...]
        e = x1_ref[...] + n * g4v - tgt_ref[...]
        lacc_ref[...] += jnp.sum(e * e, axis=0, keepdims=True)
        dy = e * (1.0 / D)
        dy_ref[...] = dy
        dg4_ref[...] += jnp.sum(dy * n, axis=0, keepdims=True)
        dff_ref[...] = _norm_bwd(dy * g4v, n, r).astype(BF16)

        @pl.when(i == nt - 1)
        def _():
            loss_ref[...] = jnp.full(loss_ref.shape, (0.5 / D) * jnp.sum(lacc_ref[...]), F32)

    row = pl.BlockSpec((tm, D), lambda i: (i, 0))
    vec = pl.BlockSpec((1, D), lambda i: (0, 0))
    return _host_call(
        body, name="ffn_down_loss", grid=(nt,), args=(a, wd, x1, tgt, g4),
        in_specs=[pl.BlockSpec((nblk, tm, bf), lambda i: (0, i, 0)), _whole(wd.shape), row, row, vec],
        out_specs=[row, row, pl.BlockSpec((1, 128), lambda i: (0, 0)), vec],
        out_shape=[jax.ShapeDtypeStruct((T, D), F32), jax.ShapeDtypeStruct((T, D), BF16),
                   jax.ShapeDtypeStruct((1, 128), F32), jax.ShapeDtypeStruct((1, D), F32)],
        scratch_shapes=[pltpu.VMEM((1, D), F32)])[0]


def _ffn_bwd_act(dff, wd, g, u, comm, tm=1024):
    T, D = dff.shape
    nb, bf, _ = wd.shape

    def body(dff_ref, wd_ref, g_ref, u_ref, dg_ref, du_ref):
        da = lax.dot_general(dff_ref[...], wd_ref[0], NT_DIMS, preferred_element_type=F32)
        gv = g_ref[0].astype(F32)
        s = jax.nn.sigmoid(gv)
        du_ref[0] = (da * (gv * s)).astype(BF16)
        dg_ref[0] = (da * u_ref[0].astype(F32) * (s * (1.0 + gv * (1.0 - s)))).astype(BF16)

    blk = pl.BlockSpec((1, tm, bf), lambda i, j: (j, i, 0))
    oshape = jax.ShapeDtypeStruct((nb, T, bf), BF16)
    return _host_call(
        body, name="ffn_bwd_act", grid=(T // tm, nb), comm=comm, args=(dff, wd, g, u),
        in_specs=[pl.BlockSpec((tm, D), lambda i, j: (i, 0)),
                  pl.BlockSpec((1, bf, D), lambda i, j: (j, 0, 0)), blk, blk],
        out_specs=[blk, blk], out_shape=[oshape, oshape])


def _ffn_bwd_in(dg, du, wg, wu, x1, dy, g3, comm_a, comm_b, tm=256):
    nb, T, bf = dg.shape
    D = x1.shape[1]
    hb = nb // 2

    def partial_sum(dg_ref, du_ref, wg_ref, wu_ref):
        s = lax.dot_general(dg_ref[0], wg_ref[0], NT_DIMS, preferred_element_type=F32)
        s = s + lax.dot_general(du_ref[0], wu_ref[0], NT_DIMS, preferred_element_type=F32)
        for k in range(1, hb):
            s = s + lax.dot_general(dg_ref[k], wg_ref[k], NT_DIMS, preferred_element_type=F32)
            s = s + lax.dot_general(du_ref[k], wu_ref[k], NT_DIMS, preferred_element_type=F32)
        return s

    def first(dg_ref, du_ref, wg_ref, wu_ref, part_ref):
        part_ref[...] = partial_sum(dg_ref, du_ref, wg_ref, wu_ref)

    def second(dg_ref, du_ref, wg_ref, wu_ref, part_ref, x1_ref, dy_ref, g3_ref, dx1_ref, dg3_ref):
        @pl.when(pl.program_id(0) == 0)
        def _():
            dg3_ref[...] = jnp.zeros_like(dg3_ref)

        dhf = part_ref[...] + partial_sum(dg_ref, du_ref, wg_ref, wu_ref)
        x1v = x1_ref[...]
        r = _rsq(x1v)
        n = x1v * r
        dg3_ref[...] += jnp.sum(dhf * n, axis=0, keepdims=True)
        dx1_ref[...] = dy_ref[...] + _norm_bwd(dhf * g3_ref[...], n, r)

    row = pl.BlockSpec((tm, D), lambda i: (i, 0))
    vec = pl.BlockSpec((1, D), lambda i: (0, 0))
    rowshape = jax.ShapeDtypeStruct((T, D), F32)

    def specs(half):
        ablk = pl.BlockSpec((hb, tm, bf), lambda i: (half, i, 0))
        wblk = pl.BlockSpec((hb, D, bf), lambda i: (half, 0, 0), pipeline_mode=pl.Buffered(1))
        return [ablk, ablk, wblk, wblk]

    (part,), c0 = _host_call(first, name="ffn_bwd_in_a", grid=(T // tm,), comm=comm_a, args=(dg, du, wg, wu),
                             in_specs=specs(0), out_specs=[row], out_shape=[rowshape])
    res, c1 = _host_call(second, name="ffn_bwd_in_b", grid=(T // tm,), comm=comm_b,
                         args=(dg, du, wg, wu, part, x1, dy, g3),
                         in_specs=specs(1) + [row, row, row, vec], out_specs=[row, vec],
                         out_shape=[rowshape, jax.ShapeDtypeStruct((1, D), F32)])
    return res, c0, c1


def _wgrad(name, lhs, rhs, lhs_spec, rhs_spec, n_rhs, M, N, nb, nk, comm=None):
    def body(*refs):
        l_ref, r_refs = refs[0], refs[1:1 + n_rhs]
        o_refs, acc_refs = refs[1 + n_rhs:1 + 2 * n_rhs], refs[1 + 2 * n_rhs:]
        k = pl.program_id(1)
        tile = lambda ref: ref[0] if len(ref.shape) == 3 else ref[...]

        @pl.when(k == 0)
        def _():
            for acc_ref in acc_refs:
                acc_ref[...] = jnp.zeros_like(acc_ref)

        for r_ref, acc_ref in zip(r_refs, acc_refs):
            acc_ref[...] += lax.dot_general(tile(l_ref), tile(r_ref), TN_DIMS, preferred_element_type=F32)

        @pl.when(k == nk - 1)
        def _():
            for o_ref, acc_ref in zip(o_refs, acc_refs):
                o_ref[0] = acc_ref[...].astype(BF16)

    oblk = pl.BlockSpec((1, M, N), lambda j, k: (j, 0, 0))
    oshape = jax.ShapeDtypeStruct((nb, M, N), BF16)
    return _host_call(
        body, name=name, grid=(nb, nk), comm=comm, args=(lhs, *rhs),
        in_specs=[lhs_spec] + [rhs_spec] * n_rhs, out_specs=[oblk] * n_rhs, out_shape=[oshape] * n_rhs,
        scratch_shapes=[pltpu.VMEM((M, N), F32)] * n_rhs)


WGRAD_TOKENS = 2048


def _wgrad_rows(name, blocks, rhs, comm=None):
    nb, T, M = blocks.shape
    N = rhs.shape[1]
    tk = min(WGRAD_TOKENS, T)
    res, cres = _wgrad(name, blocks, [rhs], pl.BlockSpec((1, tk, M), lambda j, k: (j, k, 0)),
                       pl.BlockSpec((tk, N), lambda j, k: (k, 0)), 1, M, N, nb, T // tk, comm=comm)
    return res[0], cres


def _wgrad_wide(name, lhs, rhs, n_split, kb, comm=None):
    T, M = lhs.shape
    N = rhs.shape[1]
    slab = N // n_split
    tk = min(WGRAD_TOKENS, T)
    nk = T // tk

    def body(l_ref, r_ref, o_ref, acc_ref):
        k = pl.program_id(1)

        @pl.when(k == 0)
        def _():
            acc_ref[...] = jnp.zeros_like(acc_ref)

        acc_ref[...] += lax.dot_general(l_ref[...], r_ref[...], TN_DIMS, preferred_element_type=F32)

        @pl.when(k == nk - 1)
        def _():
            if kb == 0:
                o_ref[...] = acc_ref[...].astype(BF16)
            for b in range(kb):
                o_ref[b] = acc_ref[:, b * (slab // kb):(b + 1) * (slab // kb)].astype(BF16)

    if kb == 0:
        out_spec, out_shape = pl.BlockSpec((M, slab), lambda j, k: (0, j)), (M, N)
    else:
        out_spec, out_shape = pl.BlockSpec((kb, M, slab // kb), lambda j, k: (j, 0, 0)), (n_split * kb, M, slab // kb)
    return _host_call(
        body, name=name, grid=(n_split, nk), comm=comm, args=(lhs, rhs),
        in_specs=[pl.BlockSpec((tk, M), lambda j, k: (k, 0)), pl.BlockSpec((tk, slab), lambda j, k: (k, j))],
        out_specs=[out_spec], out_shape=[jax.ShapeDtypeStruct(out_shape, BF16)],
        scratch_shapes=[pltpu.VMEM((M, slab), F32)])


def _wgrad_out(mixed, dmo, comm, nb=NDEV):
    D = mixed.shape[1]
    res, cres = _wgrad_wide("wgrad_out", mixed, dmo, 2, 0, comm=comm)
    return res[0].reshape(nb, D // nb, D), cres


def _wgrad_in(h, dproj, comm, nb=NDEV):
    res, cres = _wgrad_wide("wgrad_in", h, dproj, nb // 2, 2, comm=comm)
    return res[0], cres


def _mixer_bwd(dx1, mo, proj, cw, pw, ps, wout, g2, comm, tm=256):
    T, D = dx1.shape
    P = proj.shape[1]
    nt = T // tm
    n_ext = tm + HALO
    hb = tm // HALO

    def body(dx1_ref, mo_ref, proj_ref, hc_ref, hu_ref, hv_ref, cw_ref, pw_ref, ps_ref, wout_ref, g2_ref,
             dmo_ref, dproj_ref, dg2_ref, dcw_ref, dps_ref, dpw_ref, dmix_ref, dconv_carry, q_carry):
        i = pl.program_id(0)
        tile = nt - 1 - i

        @pl.when(i == 0)
        def _():
            dconv_carry[...] = jnp.zeros_like(dconv_carry)
            q_carry[...] = jnp.zeros_like(q_carry)
            dg2_ref[...] = jnp.zeros_like(dg2_ref)
            dcw_ref[...] = jnp.zeros_like(dcw_ref)
            dps_ref[...] = jnp.zeros_like(dps_ref)
            dpw_ref[...] = jnp.zeros_like(dpw_ref)

        mov = mo_ref[...]
        r2 = _rsq(mov)
        n2 = mov * r2
        dx1v = dx1_ref[...]
        dg2_ref[...] += jnp.sum(dx1v * n2, axis=0, keepdims=True)
        dmo = _norm_bwd(dx1v * g2_ref[...], n2, r2).astype(BF16)
        dmo_ref[...] = dmo
        dmix_ref[...] = lax.dot_general(dmo, wout_ref[...], NT_DIMS, preferred_element_type=F32)

        has_prev = (tile > 0).astype(F32)

        for h in range(CONV_HEADS):
            lo = h * HEAD_DIM
            sl = slice(lo, lo + HEAD_DIM)
            gate_b = proj_ref[:, lo:lo + HEAD_DIM].astype(F32)
            gate_c = proj_ref[:, CONV_WIDTH + lo:CONV_WIDTH + lo + HEAD_DIM].astype(F32)
            uu = proj_ref[:, 2 * CONV_WIDTH + lo:2 * CONV_WIDTH + lo + HEAD_DIM].astype(F32)
            cu = gate_c * uu
            ext = jnp.concatenate([hc_ref[:, sl].astype(F32) * hu_ref[:, sl].astype(F32) * has_prev, cu], axis=0)
            c1 = pltpu.roll(ext, 1, 0)[HALO:]
            c2 = pltpu.roll(ext, 2, 0)[HALO:]
            w0, w1, w2 = cw_ref[h, 0:1, :], cw_ref[h, 1:2, :], cw_ref[h, 2:3, :]
            conv = w2 * cu + w1 * c1 + w0 * c2
            ya = gate_b * conv
            ra = _rsq(ya)
            dya = _norm_bwd(dmix_ref[:, sl], ya * ra, ra)
            dconv = dya * gate_b
            dcw_ref[h, 0:1, :] += jnp.sum(dconv * c2, axis=0, keepdims=True)
            dcw_ref[h, 1:2, :] += jnp.sum(dconv * c1, axis=0, keepdims=True)
            dcw_ref[h, 2:3, :] += jnp.sum(dconv * cu, axis=0, keepdims=True)
            extd = jnp.concatenate([dconv, dconv_carry[:, sl]], axis=0)
            d1 = pltpu.roll(extd, n_ext - 1, 0)[:tm]
            d2 = pltpu.roll(extd, n_ext - 2, 0)[:tm]
            dcu = w2 * dconv + w1 * d1 + w0 * d2
            dconv_carry[:, sl] = dconv[:HALO]
            dproj_ref[:, lo:lo + HEAD_DIM] = (dya * conv).astype(BF16)
            dproj_ref[:, CONV_WIDTH + lo:CONV_WIDTH + lo + HEAD_DIM] = (dcu * uu).astype(BF16)
            dproj_ref[:, 2 * CONV_WIDTH + lo:2 * CONV_WIDTH + lo + HEAD_DIM] = (dcu * gate_c).astype(BF16)

        for gi, w in enumerate(POOL_WINDOWS):
            lo = gi * POOL_GROUP_DIM
            sl = slice(lo, lo + POOL_GROUP_DIM)
            v = proj_ref[:, 3 * CONV_WIDTH + lo:3 * CONV_WIDTH + lo + POOL_GROUP_DIM].astype(F32)
            inv = _inv_count(tile * tm, tm, w)
            ext = jnp.concatenate([hv_ref[:, sl].astype(F32) * has_prev, v], axis=0)
            pooled = (_window_sum(ext, w, True)[HALO:] * inv - v).astype(BF16)
            y = jnp.dot(pooled, pw_ref[gi], preferred_element_type=F32)
            rp = _rsq(y)
            nb_ = y * rp
            dyb = dmix_ref[:, CONV_WIDTH + lo:CONV_WIDTH + lo + POOL_GROUP_DIM]
            dps_ref[:, sl] += jnp.sum(dyb * nb_, axis=0, keepdims=True)
            dy = _norm_bwd(dyb * ps_ref[:, sl], nb_, rp).astype(BF16)
            dpw_ref[gi] += lax.dot_general(pooled, dy, TN_DIMS, preferred_element_type=F32)
            dpooled = lax.dot_general(dy, pw_ref[gi], NT_DIMS, preferred_element_type=F32)
            q = dpooled * inv
            extq = jnp.concatenate([q, q_carry[:, sl]], axis=0)
            dv = _window_sum(extq, w, False)[:tm] - dpooled
            q_carry[:, sl] = q[:HALO]
            dproj_ref[:, 3 * CONV_WIDTH + lo:3 * CONV_WIDTH + lo + POOL_GROUP_DIM] = dv.astype(BF16)

    rev = lambda n: pl.BlockSpec((tm, n), lambda i: (nt - 1 - i, 0))

    def halo(col):
        return pl.BlockSpec((HALO, CONV_WIDTH), lambda i: (jnp.maximum((nt - 1 - i) * hb - 1, 0), col))

    return _host_call(
        body, name="mixer_bwd", grid=(nt,), comm=comm, args=(dx1, mo, proj, proj, proj, proj, cw, pw, ps, wout, g2),
        in_specs=[rev(D), rev(D), rev(P), halo(1), halo(2), halo(3), _whole(cw.shape), _whole(pw.shape),
                  _whole(ps.shape), _whole(wout.shape), _whole(g2.shape)],
        out_specs=[rev(D), rev(P), pl.BlockSpec((1, D), lambda i: (0, 0)),
                   pl.BlockSpec(cw.shape, lambda i: (0, 0, 0)), pl.BlockSpec(ps.shape, lambda i: (0, 0)),
                   pl.BlockSpec(pw.shape, lambda i: (0, 0, 0))],
        out_shape=[jax.ShapeDtypeStruct((T, D), BF16), jax.ShapeDtypeStruct((T, P), BF16),
                   jax.ShapeDtypeStruct((1, D), F32), jax.ShapeDtypeStruct(cw.shape, F32),
                   jax.ShapeDtypeStruct(ps.shape, F32), jax.ShapeDtypeStruct(pw.shape, F32)],
        scratch_shapes=[pltpu.VMEM((tm, D), F32), pltpu.VMEM((HALO, CONV_WIDTH), F32),
                        pltpu.VMEM((HALO, CONV_WIDTH), F32)])


def _inproj_bwd(dproj, win, x, dx1, g1, comm, tm=256):
    T, D = x.shape
    nblk, _, bn = win.shape

    def body(dp_ref, w_ref, x_ref, dx1_ref, g1_ref, gx_ref, dg1_ref):
        @pl.when(pl.program_id(0) == 0)
        def _():
            dg1_ref[...] = jnp.zeros_like(dg1_ref)

        dh = lax.dot_general(dp_ref[:, 0:bn], w_ref[0], NT_DIMS, preferred_element_type=F32)
        for k in range(1, nblk):
            dh = dh + lax.dot_general(dp_ref[:, k * bn:(k + 1) * bn], w_ref[k], NT_DIMS,
                                      preferred_element_type=F32)
        xv = x_ref[...]
        r = _rsq(xv)
        n = xv * r
        dg1_ref[...] += jnp.sum(dh * n, axis=0, keepdims=True)
        gx_ref[...] = dx1_ref[...] + _norm_bwd(dh * g1_ref[...], n, r)

    row = pl.BlockSpec((tm, D), lambda i: (i, 0))
    vec = pl.BlockSpec((1, D), lambda i: (0, 0))
    return _host_call(
        body, name="inproj_bwd", grid=(T // tm,), comm=comm, args=(dproj, win, x, dx1, g1),
        in_specs=[pl.BlockSpec((tm, nblk * bn), lambda i: (i, 0)), _whole(win.shape), row, row, vec],
        out_specs=[row, vec],
        out_shape=[jax.ShapeDtypeStruct((T, D), F32), jax.ShapeDtypeStruct((1, D), F32)])


def _adamw(w, g, m, v):
    m = ADAM_B1 * m + (1.0 - ADAM_B1) * g
    v = ADAM_B2 * v + (1.0 - ADAM_B2) * jnp.square(g)
    m_hat = m / (1.0 - ADAM_B1 ** ADAM_STEP)
    v_hat = v / (1.0 - ADAM_B2 ** ADAM_STEP)
    delta = -ADAM_LR * (m_hat / (jnp.sqrt(v_hat) + ADAM_EPS) + ADAM_WD * w)
    return delta, m, v


def _sum_adamw(parts, w, m, v, name, tr):
    r, cd = w.shape

    def body(p_ref, w_ref, m_ref, v_ref, g_ref, d_ref, mo_ref, vo_ref):
        g = p_ref[0].astype(F32)
        for k in range(1, NDEV):
            g = g + p_ref[k].astype(F32)
        g_ref[...] = g
        d_ref[...], mo_ref[...], vo_ref[...] = _adamw(w_ref[...], g, m_ref[...], v_ref[...])

    blk = pl.BlockSpec((tr, cd), lambda i: (i, 0))
    shp = jax.ShapeDtypeStruct((r, cd), F32)
    return pl.pallas_call(
        body, name=name, grid=(r // tr,),
        in_specs=[pl.BlockSpec((NDEV, tr, cd), lambda i: (0, i, 0)), blk, blk, blk],
        out_specs=[blk] * 4, out_shape=[shp] * 4,
        compiler_params=_params("arbitrary"),
    )(parts, w, m, v)


def _small_reduce_adamw(loss_part, vec_grads, dps, dcw, dpw, vec_state, ps_state, cw_state, pw_state):
    D = vec_grads[0].shape[1]
    pw_rows = pw_state[0].shape[1]
    states = list(vec_state) + [ps_state, cw_state, pw_state]
    n_in = 1 + 4 + 3 + 3 * len(states)
    n_out = 1 + 4 * len(states)

    def body(*refs):
        loss_ref, dg = refs[0], refs[1:5]
        dps_ref, dcw_ref, dpw_ref = refs[5:8]
        st = refs[8:n_in]
        loss_out, outs = refs[n_in], refs[n_in + 1:n_in + n_out]
        pack, gat, cbuf, pbuf, send_sems, recv_sems, local_sems = refs[n_in + n_out:]
        x, y, c = _coords()
        me = _device_index((x, y, c))

        pack[...] = jnp.zeros_like(pack)
        for k in range(4):
            pack[k:k + 1, :] = dg[k][...]
        pack[4:5, 0:dps_ref.shape[1]] = dps_ref[...]
        pack[5:6, 0:loss_ref.shape[1]] = loss_ref[...]

        def pw_slice(i):
            return dpw_ref.at[:, pl.ds(i * pw_rows, pw_rows), :]

        mine = [pltpu.make_async_copy(pack, gat.at[me], local_sems.at[0]),
                pltpu.make_async_copy(dcw_ref.at[me], cbuf.at[me], local_sems.at[1]),
                pltpu.make_async_copy(pw_slice(me), pbuf.at[me], local_sems.at[2])]
        for cp in mine:
            cp.start()
        sends, recvs = [], []
        for mask in range(1, NDEV):
            peer = (1 - x if mask & 4 else x, 1 - y if mask & 2 else y, 1 - c if mask & 1 else c)
            p = _device_index(peer)
            for k, (src, buf) in enumerate(((pack, gat), (dcw_ref.at[p], cbuf), (pw_slice(p), pbuf))):
                kw = dict(send_sem=send_sems.at[mask, k], recv_sem=recv_sems.at[mask, k],
                          device_id=peer, device_id_type=MESH)
                sends.append(pltpu.make_async_remote_copy(src_ref=src, dst_ref=buf.at[me], **kw))
                recvs.append(pltpu.make_async_remote_copy(src_ref=src, dst_ref=buf.at[p], **kw))
                sends[-1].start()
        for cp in recvs:
            cp.wait_recv()
        for cp in sends:
            cp.wait_send()
        for cp in mine:
            cp.wait()

        def slot_sum(buf):
            s = buf[0]
            for k in range(1, NDEV):
                s = s + buf[k]
            return s

        vec = slot_sum(gat)
        loss_out[...] = vec[5:6, 0:loss_ref.shape[1]]
        grads = [vec[k:k + 1, :] for k in range(4)] + [vec[4:5, 0:dps_ref.shape[1]], slot_sum(cbuf), slot_sum(pbuf)]
        for k, g in enumerate(grads):
            w_ref, m_ref, v_ref = st[3 * k:3 * k + 3]
            outs[4 * k][...] = g
            outs[4 * k + 1][...], outs[4 * k + 2][...], outs[4 * k + 3][...] = _adamw(
                w_ref[...], g, m_ref[...], v_ref[...])

    flat_state = [a for s in states for a in s]
    out_shape = [jax.ShapeDtypeStruct(loss_part.shape, F32)]
    out_shape += [jax.ShapeDtypeStruct(s[0].shape, F32) for s in states for _ in range(4)]
    return pl.pallas_call(
        body, name="small_reduce_adamw",
        in_specs=[VMEM_SPEC] * n_in, out_specs=[VMEM_SPEC] * n_out, out_shape=out_shape,
        scratch_shapes=[pltpu.VMEM((NDEV, D), F32), pltpu.VMEM((NDEV, NDEV, D), F32),
                        pltpu.VMEM((NDEV,) + cw_state[0].shape, F32), pltpu.VMEM((NDEV,) + pw_state[0].shape, F32),
                        pltpu.SemaphoreType.DMA((NDEV, 3)), pltpu.SemaphoreType.DMA((NDEV, 3)),
                        pltpu.SemaphoreType.DMA((3,))],
        compiler_params=pltpu.CompilerParams(vmem_limit_bytes=VMEM_LIMIT),
    )(loss_part, *vec_grads, dps, dcw, dpw, *flat_state)


ROW_TILE = dict(w_in=512, w_gate=176, w_up=176, w_down=176, w_out=128)
FORWARD_STEP = dict(inproj=56, mixer_fwd=26, ffn_up=24)


def kernel(x, ln_mix_pre, w_in, conv_w, pool_w, pool_scale, w_out, ln_mix_post, ln_ffn_pre, w_gate, w_up, w_down, ln_ffn_post, loss_target, m_ln_mix_pre, m_w_in, m_conv_w, m_pool_w, m_pool_scale, m_w_out, m_ln_mix_post, m_ln_ffn_pre, m_w_gate, m_w_up, m_w_down, m_ln_ffn_post, v_ln_mix_pre, v_w_in, v_conv_w, v_pool_w, v_pool_scale, v_w_out, v_ln_mix_post, v_ln_ffn_pre, v_w_gate, v_w_up, v_w_down, v_ln_ffn_post):
    D = x.shape[2]
    xs, tgt = x[0], loss_target[0]
    flip = lambda arr: jnp.swapaxes(arr[0], 0, 1)
    win, wg, wu, wd, wout, pw, cw = _cast_gather_first(
        [w_in[0], flip(w_gate), flip(w_up), w_down[0], w_out[0], pool_w[0], conv_w[0]])

    (proj, h), (wout, wg) = _inproj(xs, ln_mix_pre, win, _merge_comm(
        _gather_comm([wout], FORWARD_STEP["inproj"]), _gather_comm([wg], FORWARD_STEP["inproj"], core=1)))
    wout2 = wout.reshape(D, D)
    (x1, hf, mixed, mo), (wu,) = _mixer_fwd(proj, xs, cw, pw, pool_scale, wout2, ln_mix_post, ln_ffn_pre,
                                            _gather_comm([wu], FORWARD_STEP["mixer_fwd"], core=1))
    gua, (wg, wu) = _ffn_up(hf, wg, wu, _gather_comm([wg, wu], FORWARD_STEP["ffn_up"], core=0), 1)
    (g, u, a), (wd,) = _ffn_up(hf, wg, wu, _gather_comm([wd], FORWARD_STEP["ffn_up"]), 0, fill=gua)
    dy, dff, loss, dg4 = _ffn_down_loss(a, wd, x1, tgt, ln_ffn_post)

    rows = lambda arr, lo, hi: (lo * arr.shape[1] // 4, (hi - lo) * arr.shape[1] // 4)
    dwd, _ = _wgrad_rows("wgrad_down", a, dff)
    (dg, du), (dwd_parts,) = _ffn_bwd_act(dff, wd, g, u, _scatter_comm([(dwd, None, *rows(dwd, 0, 2))]))
    dwg, (dwd_parts,) = _wgrad_rows("wgrad_gate", dg, hf, _scatter_comm([(dwd, dwd_parts, *rows(dwd, 2, 4))]))
    dwu, (dwg_parts,) = _wgrad_rows("wgrad_up", du, hf, _scatter_comm([(dwg, None, *rows(dwg, 0, 2))]))
    (dx1, dg3), (dwg_parts,), (dwu_parts,) = _ffn_bwd_in(
        dg, du, wg, wu, x1, dy, ln_ffn_pre, _scatter_comm([(dwg, dwg_parts, *rows(dwg, 2, 4))]),
        _scatter_comm([(dwu, None, *rows(dwu, 0, 2))]))
    (dmo, dproj, dg2, dcw, dps, dpw), (dwu_parts,) = _mixer_bwd(
        dx1, mo, proj, cw, pw, pool_scale, wout2, ln_mix_post, _scatter_comm([(dwu, dwu_parts, *rows(dwu, 2, 4))]))
    dwout, _ = _wgrad_out(mixed, dmo, None)
    dwin, (dwout_parts,) = _wgrad_in(h, dproj, _scatter_comm([(dwout, None, *rows(dwout, 0, 4))]))
    (gx, dg1), (dwin_parts,) = _inproj_bwd(dproj, win, xs, dx1, ln_mix_pre,
                                           _scatter_comm([(dwin, None, *rows(dwin, 0, 4))]))

    res = {}
    for k, parts, w, m, v in (("w_down", dwd_parts, w_down, m_w_down, v_w_down),
                              ("w_out", dwout_parts, w_out, m_w_out, v_w_out),
                              ("w_in", dwin_parts, w_in, m_w_in, v_w_in)):
        res[k] = [o.reshape(w.shape) for o in _sum_adamw(parts, w[0], m[0], v[0], "sum_adamw_" + k, ROW_TILE[k])]
    for k, parts, w, m, v in (("w_gate", dwg_parts, w_gate, m_w_gate, v_w_gate),
                              ("w_up", dwu_parts, w_up, m_w_up, v_w_up)):
        outs = _sum_adamw(parts, flip(w), flip(m), flip(v), "sum_adamw_" + k, ROW_TILE[k])
        res[k] = [jnp.swapaxes(o, 0, 1)[None] for o in outs]

    loss_sum, *small = _small_reduce_adamw(
        loss, [dg1, dg2, dg3, dg4], dps, dcw, dpw,
        [(ln_mix_pre, m_ln_mix_pre, v_ln_mix_pre), (ln_mix_post, m_ln_mix_post, v_ln_mix_post),
         (ln_ffn_pre, m_ln_ffn_pre, v_ln_ffn_pre), (ln_ffn_post, m_ln_ffn_post, v_ln_ffn_post)],
        (pool_scale, m_pool_scale, v_pool_scale), (conv_w[0], m_conv_w[0], v_conv_w[0]),
        (pool_w[0], m_pool_w[0], v_pool_w[0]))
    small_names = ["ln_mix_pre", "ln_mix_post", "ln_ffn_pre", "ln_ffn_post", "pool_scale", "conv_w", "pool_w"]
    shapes = dict(conv_w=conv_w.shape, pool_w=pool_w.shape)
    for i, k in enumerate(small_names):
        res[k] = [o.reshape(shapes[k]) if k in shapes else o for o in small[4 * i:4 * i + 4]]

    order = ["ln_mix_pre", "w_in", "conv_w", "pool_w", "pool_scale", "w_out", "ln_mix_post", "ln_ffn_pre",
             "w_gate", "w_up", "w_down", "ln_ffn_post"]
    return (loss_sum[0, 0], gx[None], *[res[k][0] for k in order], *[res[k][1] for k in order],
            *[res[k][2] for k in order], *[res[k][3] for k in order])
```

```python
import functools
from typing import Any, NamedTuple

import jax
import jax.numpy as jnp
from jax import lax
from jax.experimental import pallas as pl
from jax.experimental.pallas import tpu as pltpu

EPS = 1e-6
NDEV = 8
CONV_HEADS = 8
HEAD_DIM = 128
CONV_WIDTH = CONV_HEADS * HEAD_DIM
POOL_WINDOWS = (2, 4, 8, 16)
POOL_GROUP_DIM = 256
HALO = 16

ADAM_LR = 0.001
ADAM_B1 = 0.9
ADAM_B2 = 0.999
ADAM_EPS = 1e-08
ADAM_WD = 0.01
ADAM_STEP = 10

F32 = jnp.float32
BF16 = jnp.bfloat16
VMEM_LIMIT = 58 * 1024 * 1024
MESH = pl.DeviceIdType.MESH
HBM_SPEC = pl.BlockSpec(memory_space=pl.ANY)
VMEM_SPEC = pl.BlockSpec(memory_space=pltpu.VMEM)

NT_DIMS = (((1,), (1,)), ((), ()))
TN_DIMS = (((0,), (0,)), ((), ()))


def _params(*sem):
    return pltpu.CompilerParams(dimension_semantics=sem, vmem_limit_bytes=VMEM_LIMIT)


def _rsq(v):
    return lax.rsqrt(jnp.mean(v * v, axis=-1, keepdims=True) + EPS)


def _norm_bwd(dn, n, r):
    return r * (dn - n * jnp.mean(dn * n, axis=-1, keepdims=True))


def _whole(shape):
    nd = len(shape)
    return pl.BlockSpec(shape, lambda *_: (0,) * nd, pipeline_mode=pl.Buffered(1))


def _inv_count(t0, tm, w):
    t = t0 + lax.broadcasted_iota(jnp.int32, (tm, 1), 0)
    return 1.0 / jnp.minimum(t + 1, w).astype(F32)


def _window_sum(ext, w, back):
    n = ext.shape[0]
    s, shift = ext, 1
    while shift < w:
        s = s + pltpu.roll(s, shift if back else n - shift, 0)
        shift *= 2
    return s


class _Comm(NamedTuple):
    arrays: Any
    out_shape: Any
    aliases: Any
    scratch: Any
    hooks: Any


def _coords():
    return lax.axis_index("x"), lax.axis_index("y"), lax.axis_index("c")


def _other_chips(x, y):
    return [(1 - x, y), (x, 1 - y), (1 - x, 1 - y)]


def _device_index(dev):
    return 4 * dev[0] + 2 * dev[1] + dev[2]


def _host_call(body, *, name, grid, in_specs, out_specs, out_shape, args, scratch_shapes=(), comm=None):
    sem = ("arbitrary",) * len(grid)
    in_specs, out_specs, out_shape, scratch_shapes = list(in_specs), list(out_specs), list(out_shape), list(scratch_shapes)
    if comm is None:
        res = pl.pallas_call(body, name=name, grid=grid, in_specs=in_specs, out_specs=out_specs, out_shape=out_shape,
                             scratch_shapes=scratch_shapes, compiler_params=_params(*sem))(*args)
        return res, []
    n_in, n_out, n_scr = len(in_specs), len(out_specs), len(scratch_shapes)
    n_cin, n_cout = len(comm.arrays), len(comm.out_shape)
    total = functools.reduce(lambda a, b: a * b, grid)

    def wrapped(*refs):
        ins, cin = refs[:n_in], refs[n_in:n_in + n_cin]
        o0 = n_in + n_cin
        outs, cout = refs[o0:o0 + n_out], refs[o0 + n_out:o0 + n_out + n_cout]
        s0 = o0 + n_out + n_cout
        scr, sems = refs[s0:s0 + n_scr], refs[s0 + n_scr:]
        step = pl.program_id(0)
        for d in range(1, len(grid)):
            step = step * grid[d] + pl.program_id(d)
        for when, before, fn in comm.hooks:
            if before:
                pl.when(step == when % total)(functools.partial(fn, cin, cout, sems))
        body(*ins, *outs, *scr)
        for when, before, fn in comm.hooks:
            if not before:
                pl.when(step == when % total)(functools.partial(fn, cin, cout, sems))

    res = pl.pallas_call(
        wrapped, name=name, grid=grid,
        in_specs=in_specs + [HBM_SPEC] * n_cin, out_specs=out_specs + [HBM_SPEC] * n_cout,
        out_shape=out_shape + list(comm.out_shape), scratch_shapes=scratch_shapes + list(comm.scratch),
        input_output_aliases={n_in + i: n_out + o for i, o in comm.aliases.items()},
        compiler_params=_params(*sem),
    )(*args, *comm.arrays)
    return res[:n_out], res[n_out:]


def _gather_steps(n, view, own_src, send_sems, recv_sems):
    x, y, c = _coords()
    me, sibling = (x, y, c), (x, y, 1 - c)
    chips = _other_chips(x, y)

    def copy(a, k, block, to, src=None):
        return pltpu.make_async_remote_copy(
            src_ref=view(a, block) if src is None else src, dst_ref=view(a, block),
            send_sem=send_sems.at[a, k], recv_sem=recv_sems.at[a, k], device_id=to, device_id_type=MESH)

    def first_copies():
        cps = []
        for a in range(n):
            cps.append(copy(a, 0, me, sibling, src=own_src(a)))
            cps += [copy(a, 1 + j, me, (*chip, c), src=own_src(a)) for j, chip in enumerate(chips)]
        return cps

    def passed_copies():
        return [copy(a, 4 + j, (*chip, c), sibling) for j, chip in enumerate(chips) for a in range(n)]

    def first():
        for cp in first_copies():
            cp.start()

    def forward():
        for j, chip in enumerate(chips):
            for a in range(n):
                copy(a, 1 + j, (*chip, c), me).wait_recv()
                copy(a, 4 + j, (*chip, c), sibling).start()

    def finish():
        for a in range(n):
            copy(a, 0, sibling, me).wait_recv()
            for j, chip in enumerate(chips):
                copy(a, 4 + j, (*chip, 1 - c), me).wait_recv()
        for cp in first_copies() + passed_copies():
            cp.wait_send()

    return first, forward, finish


def _gather_comm(arrays, forward_step):
    n = len(arrays)

    def steps(cout, sems):
        view = lambda a, dev: cout[a].at[_device_index(dev)]
        return _gather_steps(n, view, lambda a: view(a, _coords()), sems[0], sems[1])

    hooks = [(0, True, lambda cin, cout, sems: steps(cout, sems)[0]()),
             (forward_step, True, lambda cin, cout, sems: steps(cout, sems)[1]()),
             (-1, False, lambda cin, cout, sems: steps(cout, sems)[2]())]
    return _Comm(list(arrays), [jax.ShapeDtypeStruct(a.shape, a.dtype) for a in arrays], {i: i for i in range(n)},
                 [pltpu.SemaphoreType.DMA((n, 7)), pltpu.SemaphoreType.DMA((n, 7))], hooks)


def _scatter_comm(items):
    n = len(items)
    arrays, aliases, grad_at = [], {}, []
    for k, (grad, slots, _, _) in enumerate(items):
        grad_at.append(len(arrays))
        arrays.append(grad)
        if slots is not None:
            aliases[len(arrays)] = k
            arrays.append(slots)

    def copies(cin, cout, sems):
        send_sems, recv_sems, local_sems = sems
        x, y, c = _coords()
        me = _device_index((x, y, c))
        src = lambda a, p: cin[grad_at[a]].at[p, pl.ds(items[a][2], items[a][3]), :]
        dst = lambda a, p: cout[a].at[p, pl.ds(items[a][2], items[a][3]), :]
        mine = [pltpu.make_async_copy(src(a, me), dst(a, me), local_sems.at[a]) for a in range(n)]
        sends, recvs = [], []
        for a in range(n):
            for mask in range(1, NDEV):
                peer = (1 - x if mask & 4 else x, 1 - y if mask & 2 else y, 1 - c if mask & 1 else c)
                p = _device_index(peer)
                kw = dict(send_sem=send_sems.at[a, mask - 1], recv_sem=recv_sems.at[a, mask - 1],
                          device_id=peer, device_id_type=MESH)
                sends.append(pltpu.make_async_remote_copy(src_ref=src(a, p), dst_ref=dst(a, me), **kw))
                recvs.append(pltpu.make_async_remote_copy(src_ref=src(a, p), dst_ref=dst(a, p), **kw))
        return mine, sends, recvs

    def start(cin, cout, sems):
        mine, sends, _ = copies(cin, cout, sems)
        for cp in mine + sends:
            cp.start()

    def finish(cin, cout, sems):
        mine, sends, recvs = copies(cin, cout, sems)
        for cp in recvs:
            cp.wait_recv()
        for cp in sends:
            cp.wait_send()
        for cp in mine:
            cp.wait()

    return _Comm(arrays, [jax.ShapeDtypeStruct(it[0].shape, it[0].dtype) for it in items], aliases,
                 [pltpu.SemaphoreType.DMA((n, NDEV - 1)), pltpu.SemaphoreType.DMA((n, NDEV - 1)),
                  pltpu.SemaphoreType.DMA((n,))],
                 [(0, True, start), (-1, False, finish)])


NOW_ITEMS = (0, 5, 6)
FLIPPED_ITEMS = (1, 2)
POOL_ITEM = 5


def _cast_gather_first(shards):
    n = len(shards)
    dtypes = [BF16] * 6 + [F32]
    block_shapes = [s.shape[::-1] if a in FLIPPED_ITEMS else s.shape for a, s in enumerate(shards)]
    out_shapes = [(NDEV,) + s for s in block_shapes]
    g, rows, cols = shards[POOL_ITEM].shape
    out_shapes[POOL_ITEM] = (g, rows * NDEV, cols)

    later = [a for a in range(n) if a not in NOW_ITEMS]

    def body(*refs):
        ins, outs, raw, stage = refs[:n], refs[n:2 * n], refs[2 * n:3 * n], refs[3 * n:4 * n]
        send_sems, recv_sems, local_sems, load_sems = refs[4 * n:]

        def view(a, dev):
            i = _device_index(dev)
            if a == POOL_ITEM:
                return outs[a].at[:, pl.ds(i * rows, rows), :]
            return outs[a].at[i]

        loads = [pltpu.make_async_copy(ins[a], raw[a], load_sems.at[a]) for a in range(n)]
        mine = [pltpu.make_async_copy(stage[a], view(a, _coords()), local_sems.at[a]) for a in range(n)]
        for a in list(NOW_ITEMS) + later:
            loads[a].start()
        first, forward, finish = _gather_steps(
            len(NOW_ITEMS), lambda k, dev: view(NOW_ITEMS[k], dev), lambda k: stage[NOW_ITEMS[k]], send_sems, recv_sems)
        for a in list(NOW_ITEMS) + later:
            loads[a].wait()
            if a in FLIPPED_ITEMS:
                k = raw[a].shape[0]
                eye = (lax.broadcasted_iota(jnp.int32, (k, k), 0) == lax.broadcasted_iota(jnp.int32, (k, k), 1))
                stage[a][...] = lax.dot_general(raw[a][...].astype(BF16), eye.astype(BF16), TN_DIMS,
                                                preferred_element_type=F32).astype(BF16)
            else:
                stage[a][...] = raw[a][...].astype(dtypes[a])
            mine[a].start()
            if a == NOW_ITEMS[-1]:
                first()
        forward()
        finish()
        for cp in mine:
            cp.wait()

    return pl.pallas_call(
        body, name="cast_gather_first",
        in_specs=[HBM_SPEC] * n, out_specs=[HBM_SPEC] * n,
        out_shape=[jax.ShapeDtypeStruct(s, d) for s, d in zip(out_shapes, dtypes)],
        scratch_shapes=[pltpu.VMEM(s.shape, s.dtype) for s in shards]
        + [pltpu.VMEM(s, d) for s, d in zip(block_shapes, dtypes)]
        + [pltpu.SemaphoreType.DMA((len(NOW_ITEMS), 7)), pltpu.SemaphoreType.DMA((len(NOW_ITEMS), 7)),
           pltpu.SemaphoreType.DMA((n,)), pltpu.SemaphoreType.DMA((n,))],
        compiler_params=pltpu.CompilerParams(vmem_limit_bytes=VMEM_LIMIT),
    )(*shards)


def _inproj(x, g1, win, comm, tm=1024):
    T, D = x.shape
    nb, _, bn = win.shape

    def body(x_ref, g_ref, w_ref, proj_ref, h_ref):
        @pl.when(pl.program_id(1) == 0)
        def _():
            xv = x_ref[...]
            h_ref[...] = (xv * _rsq(xv) * g_ref[...]).astype(BF16)

        proj_ref[...] = jnp.dot(h_ref[...], w_ref[0], preferred_element_type=F32).astype(BF16)

    return _host_call(
        body, name="inproj", grid=(T // tm, nb), comm=comm, args=(x, g1, win),
        in_specs=[pl.BlockSpec((tm, D), lambda i, j: (i, 0)),
                  pl.BlockSpec((1, D), lambda i, j: (0, 0)),
                  pl.BlockSpec((1, D, bn), lambda i, j: (j, 0, 0))],
        out_specs=[pl.BlockSpec((tm, bn), lambda i, j: (i, j)),
                   pl.BlockSpec((tm, D), lambda i, j: (i, 0))],
        out_shape=[jax.ShapeDtypeStruct((T, nb * bn), BF16), jax.ShapeDtypeStruct((T, D), BF16)])


def _mixer_fwd(proj, x, cw, pw, ps, wout, g2, g3, comm, tm=256):
    T, D = x.shape
    P = proj.shape[1]

    def body(proj_ref, x_ref, cw_ref, pw_ref, ps_ref, wout_ref, g2_ref, g3_ref,
             x1_ref, hf_ref, mixed_ref, mo_ref, cu_carry, v_carry):
        i = pl.program_id(0)

        @pl.when(i == 0)
        def _():
            cu_carry[...] = jnp.zeros_like(cu_carry)
            v_carry[...] = jnp.zeros_like(v_carry)

        for h in range(CONV_HEADS):
            lo = h * HEAD_DIM
            gate_b = proj_ref[:, lo:lo + HEAD_DIM].astype(F32)
            cu = proj_ref[:, CONV_WIDTH + lo:CONV_WIDTH + lo + HEAD_DIM].astype(F32) * \
                proj_ref[:, 2 * CONV_WIDTH + lo:2 * CONV_WIDTH + lo + HEAD_DIM].astype(F32)
            ext = jnp.concatenate([cu_carry[:, lo:lo + HEAD_DIM], cu], axis=0)
            c1 = pltpu.roll(ext, 1, 0)[HALO:]
            c2 = pltpu.roll(ext, 2, 0)[HALO:]
            ya = gate_b * (cw_ref[h, 2:3, :] * cu + cw_ref[h, 1:2, :] * c1 + cw_ref[h, 0:1, :] * c2)
            mixed_ref[:, lo:lo + HEAD_DIM] = (ya * _rsq(ya)).astype(BF16)
            cu_carry[:, lo:lo + HEAD_DIM] = cu[tm - HALO:]

        for gi, w in enumerate(POOL_WINDOWS):
            lo = gi * POOL_GROUP_DIM
            v = proj_ref[:, 3 * CONV_WIDTH + lo:3 * CONV_WIDTH + lo + POOL_GROUP_DIM].astype(F32)
            ext = jnp.concatenate([v_carry[:, lo:lo + POOL_GROUP_DIM], v], axis=0)
            pooled = _window_sum(ext, w, True)[HALO:] * _inv_count(i * tm, tm, w) - v
            y = jnp.dot(pooled.astype(BF16), pw_ref[gi], preferred_element_type=F32)
            yb = y * _rsq(y) * ps_ref[:, lo:lo + POOL_GROUP_DIM]
            mixed_ref[:, CONV_WIDTH + lo:CONV_WIDTH + lo + POOL_GROUP_DIM] = yb.astype(BF16)
            v_carry[:, lo:lo + POOL_GROUP_DIM] = v[tm - HALO:]

        mo = jnp.dot(mixed_ref[...], wout_ref[...], preferred_element_type=F32)
        mo_ref[...] = mo
        x1 = x_ref[...] + mo * _rsq(mo) * g2_ref[...]
        x1_ref[...] = x1
        hf_ref[...] = (x1 * _rsq(x1) * g3_ref[...]).astype(BF16)

    row = lambda n: pl.BlockSpec((tm, n), lambda i: (i, 0))
    return _host_call(
        body, name="mixer_fwd", grid=(T // tm,), comm=comm, args=(proj, x, cw, pw, ps, wout, g2, g3),
        in_specs=[row(P), row(D), _whole(cw.shape), _whole(pw.shape), _whole(ps.shape),
                  _whole(wout.shape), _whole(g2.shape), _whole(g3.shape)],
        out_specs=[row(D), row(D), row(D), row(D)],
        out_shape=[jax.ShapeDtypeStruct((T, D), F32), jax.ShapeDtypeStruct((T, D), BF16),
                   jax.ShapeDtypeStruct((T, D), BF16), jax.ShapeDtypeStruct((T, D), F32)],
        scratch_shapes=[pltpu.VMEM((HALO, CONV_WIDTH), F32), pltpu.VMEM((HALO, CONV_WIDTH), F32)])


def _ffn_up(hf, wg, wu, comm, tm=1024):
    T, D = hf.shape
    nb, _, bf = wg.shape

    def body(hf_ref, wg_ref, wu_ref, g_ref, u_ref, a_ref):
        hv = hf_ref[...]
        g = jnp.dot(hv, wg_ref[0], preferred_element_type=F32)
        u = jnp.dot(hv, wu_ref[0], preferred_element_type=F32)
        g_ref[0] = g.astype(BF16)
        u_ref[0] = u.astype(BF16)
        a_ref[0] = (g * jax.nn.sigmoid(g) * u).astype(BF16)

    wspec = pl.BlockSpec((1, D, bf), lambda i, j: (j, 0, 0))
    ospec = pl.BlockSpec((1, tm, bf), lambda i, j: (j, i, 0))
    oshape = jax.ShapeDtypeStruct((nb, T, bf), BF16)
    return _host_call(
        body, name="ffn_up", grid=(T // tm, nb), comm=comm, args=(hf, wg, wu),
        in_specs=[pl.BlockSpec((tm, D), lambda i, j: (i, 0)), wspec, wspec],
        out_specs=[ospec, ospec, ospec], out_shape=[oshape, oshape, oshape])


def _ffn_down_loss(a, wd, x1, tgt, g4, tm=256):
    nblk, T, bf = a.shape
    D = x1.shape[1]
    nt = T // tm

    def body(a_ref, wd_ref, x1_ref, tgt_ref, g4_ref, dy_ref, dff_ref, loss_ref, dg4_ref, lacc_ref):
        i = pl.program_id(0)

        @pl.when(i == 0)
        def _():
            lacc_ref[...] = jnp.zeros_like(lacc_ref)
            dg4_ref[...] = jnp.zeros_like(dg4_ref)

        ff = jnp.dot(a_ref[0], wd_ref[0], preferred_element_type=F32)
        for k in range(1, nblk):
            ff = ff + jnp.dot(a_ref[k], wd_ref[k], preferred_element_type=F32)
        r = _rsq(ff)
        n = ff * r
        g4v = g4_ref[...]
        e = x1_ref[...] + n * g4v - tgt_ref[...]
        lacc_ref[...] += jnp.sum(e * e, axis=0, keepdims=True)
        dy = e * (1.0 / D)
        dy_ref[...] = dy
        dg4_ref[...] += jnp.sum(dy * n, axis=0, keepdims=True)
        dff_ref[...] = _norm_bwd(dy * g4v, n, r).astype(BF16)

        @pl.when(i == nt - 1)
        def _():
            loss_ref[...] = jnp.full(loss_ref.shape, (0.5 / D) * jnp.sum(lacc_ref[...]), F32)

    row = pl.BlockSpec((tm, D), lambda i: (i, 0))
    vec = pl.BlockSpec((1, D), lambda i: (0, 0))
    return _host_call(
        body, name="ffn_down_loss", grid=(nt,), args=(a, wd, x1, tgt, g4),
        in_specs=[pl.BlockSpec((nblk, tm, bf), lambda i: (0, i, 0)), _whole(wd.shape), row, row, vec],
        out_specs=[row, row, pl.BlockSpec((1, 128), lambda i: (0, 0)), vec],
        out_shape=[jax.ShapeDtypeStruct((T, D), F32), jax.ShapeDtypeStruct((T, D), BF16),
                   jax.ShapeDtypeStruct((1, 128), F32), jax.ShapeDtypeStruct((1, D), F32)],
        scratch_shapes=[pltpu.VMEM((1, D), F32)])[0]


def _ffn_bwd_act(dff, wd, g, u, comm, tm=1024):
    T, D = dff.shape
    nb, bf, _ = wd.shape

    def body(dff_ref, wd_ref, g_ref, u_ref, dg_ref, du_ref):
        da = lax.dot_general(dff_ref[...], wd_ref[0], NT_DIMS, preferred_element_type=F32)
        gv = g_ref[0].astype(F32)
        s = jax.nn.sigmoid(gv)
        du_ref[0] = (da * (gv * s)).astype(BF16)
        dg_ref[0] = (da * u_ref[0].astype(F32) * (s * (1.0 + gv * (1.0 - s)))).astype(BF16)

    blk = pl.BlockSpec((1, tm, bf), lambda i, j: (j, i, 0))
    oshape = jax.ShapeDtypeStruct((nb, T, bf), BF16)
    return _host_call(
        body, name="ffn_bwd_act", grid=(T // tm, nb), comm=comm, args=(dff, wd, g, u),
        in_specs=[pl.BlockSpec((tm, D), lambda i, j: (i, 0)),
                  pl.BlockSpec((1, bf, D), lambda i, j: (j, 0, 0)), blk, blk],
        out_specs=[blk, blk], out_shape=[oshape, oshape])


def _ffn_bwd_in(dg, du, wg, wu, x1, dy, g3, comm_a, comm_b, tm=256):
    nb, T, bf = dg.shape
    D = x1.shape[1]
    hb = nb // 2

    def partial_sum(dg_ref, du_ref, wg_ref, wu_ref):
        s = lax.dot_general(dg_ref[0], wg_ref[0], NT_DIMS, preferred_element_type=F32)
        s = s + lax.dot_general(du_ref[0], wu_ref[0], NT_DIMS, preferred_element_type=F32)
        for k in range(1, hb):
            s = s + lax.dot_general(dg_ref[k], wg_ref[k], NT_DIMS, preferred_element_type=F32)
            s = s + lax.dot_general(du_ref[k], wu_ref[k], NT_DIMS, preferred_element_type=F32)
        return s

    def first(dg_ref, du_ref, wg_ref, wu_ref, part_ref):
        part_ref[...] = partial_sum(dg_ref, du_ref, wg_ref, wu_ref)

    def second(dg_ref, du_ref, wg_ref, wu_ref, part_ref, x1_ref, dy_ref, g3_ref, dx1_ref, dg3_ref):
        @pl.when(pl.program_id(0) == 0)
        def _():
            dg3_ref[...] = jnp.zeros_like(dg3_ref)

        dhf = part_ref[...] + partial_sum(dg_ref, du_ref, wg_ref, wu_ref)
        x1v = x1_ref[...]
        r = _rsq(x1v)
        n = x1v * r
        dg3_ref[...] += jnp.sum(dhf * n, axis=0, keepdims=True)
        dx1_ref[...] = dy_ref[...] + _norm_bwd(dhf * g3_ref[...], n, r)

    row = pl.BlockSpec((tm, D), lambda i: (i, 0))
    vec = pl.BlockSpec((1, D), lambda i: (0, 0))
    rowshape = jax.ShapeDtypeStruct((T, D), F32)

    def specs(half):
        ablk = pl.BlockSpec((hb, tm, bf), lambda i: (half, i, 0))
        wblk = pl.BlockSpec((hb, D, bf), lambda i: (half, 0, 0), pipeline_mode=pl.Buffered(1))
        return [ablk, ablk, wblk, wblk]

    (part,), c0 = _host_call(first, name="ffn_bwd_in_a", grid=(T // tm,), comm=comm_a, args=(dg, du, wg, wu),
                             in_specs=specs(0), out_specs=[row], out_shape=[rowshape])
    res, c1 = _host_call(second, name="ffn_bwd_in_b", grid=(T // tm,), comm=comm_b,
                         args=(dg, du, wg, wu, part, x1, dy, g3),
                         in_specs=specs(1) + [row, row, row, vec], out_specs=[row, vec],
                         out_shape=[rowshape, jax.ShapeDtypeStruct((1, D), F32)])
    return res, c0, c1


def _wgrad(name, lhs, rhs, lhs_spec, rhs_spec, n_rhs, M, N, nb, nk, comm=None):
    def body(*refs):
        l_ref, r_refs = refs[0], refs[1:1 + n_rhs]
        o_refs, acc_refs = refs[1 + n_rhs:1 + 2 * n_rhs], refs[1 + 2 * n_rhs:]
        k = pl.program_id(1)
        tile = lambda ref: ref[0] if len(ref.shape) == 3 else ref[...]

        @pl.when(k == 0)
        def _():
            for acc_ref in acc_refs:
                acc_ref[...] = jnp.zeros_like(acc_ref)

        for r_ref, acc_ref in zip(r_refs, acc_refs):
            acc_ref[...] += lax.dot_general(tile(l_ref), tile(r_ref), TN_DIMS, preferred_element_type=F32)

        @pl.when(k == nk - 1)
        def _():
            for o_ref, acc_ref in zip(o_refs, acc_refs):
                o_ref[0] = acc_ref[...].astype(BF16)

    oblk = pl.BlockSpec((1, M, N), lambda j, k: (j, 0, 0))
    oshape = jax.ShapeDtypeStruct((nb, M, N), BF16)
    return _host_call(
        body, name=name, grid=(nb, nk), comm=comm, args=(lhs, *rhs),
        in_specs=[lhs_spec] + [rhs_spec] * n_rhs, out_specs=[oblk] * n_rhs, out_shape=[oshape] * n_rhs,
        scratch_shapes=[pltpu.VMEM((M, N), F32)] * n_rhs)


WGRAD_TOKENS = 2048


def _wgrad_rows(name, blocks, rhs, comm=None):
    nb, T, M = blocks.shape
    N = rhs.shape[1]
    tk = min(WGRAD_TOKENS, T)
    res, cres = _wgrad(name, blocks, [rhs], pl.BlockSpec((1, tk, M), lambda j, k: (j, k, 0)),
                       pl.BlockSpec((tk, N), lambda j, k: (k, 0)), 1, M, N, nb, T // tk, comm=comm)
    return res[0], cres


def _wgrad_wide(name, lhs, rhs, n_split, kb, comm=None):
    T, M = lhs.shape
    N = rhs.shape[1]
    slab = N // n_split
    tk = min(WGRAD_TOKENS, T)
    nk = T // tk

    def body(l_ref, r_ref, o_ref, acc_ref):
        k = pl.program_id(1)

        @pl.when(k == 0)
        def _():
            acc_ref[...] = jnp.zeros_like(acc_ref)

        acc_ref[...] += lax.dot_general(l_ref[...], r_ref[...], TN_DIMS, preferred_element_type=F32)

        @pl.when(k == nk - 1)
        def _():
            if kb == 0:
                o_ref[...] = acc_ref[...].astype(BF16)
            for b in range(kb):
                o_ref[b] = acc_ref[:, b * (slab // kb):(b + 1) * (slab // kb)].astype(BF16)

    if kb == 0:
        out_spec, out_shape = pl.BlockSpec((M, slab), lambda j, k: (0, j)), (M, N)
    else:
        out_spec, out_shape = pl.BlockSpec((kb, M, slab // kb), lambda j, k: (j, 0, 0)), (n_split * kb, M, slab // kb)
    return _host_call(
        body, name=name, grid=(n_split, nk), comm=comm, args=(lhs, rhs),
        in_specs=[pl.BlockSpec((tk, M), lambda j, k: (k, 0)), pl.BlockSpec((tk, slab), lambda j, k: (k, j))],
        out_specs=[out_spec], out_shape=[jax.ShapeDtypeStruct(out_shape, BF16)],
        scratch_shapes=[pltpu.VMEM((M, slab), F32)])


def _wgrad_out(mixed, dmo, comm, nb=NDEV):
    D = mixed.shape[1]
    res, cres = _wgrad_wide("wgrad_out", mixed, dmo, 2, 0, comm=comm)
    return res[0].reshape(nb, D // nb, D), cres


def _wgrad_in(h, dproj, comm, nb=NDEV):
    res, cres = _wgrad_wide("wgrad_in", h, dproj, nb // 2, 2, comm=comm)
    return res[0], cres


def _mixer_bwd(dx1, mo, proj, cw, pw, ps, wout, g2, comm, tm=256):
    T, D = dx1.shape
    P = proj.shape[1]
    nt = T // tm
    n_ext = tm + HALO
    hb = tm // HALO

    def body(dx1_ref, mo_ref, proj_ref, hc_ref, hu_ref, hv_ref, cw_ref, pw_ref, ps_ref, wout_ref, g2_ref,
             dmo_ref, dproj_ref, dg2_ref, dcw_ref, dps_ref, dpw_ref, dmix_ref, dconv_carry, q_carry):
        i = pl.program_id(0)
        tile = nt - 1 - i

        @pl.when(i == 0)
        def _():
            dconv_carry[...] = jnp.zeros_like(dconv_carry)
            q_carry[...] = jnp.zeros_like(q_carry)
            dg2_ref[...] = jnp.zeros_like(dg2_ref)
            dcw_ref[...] = jnp.zeros_like(dcw_ref)
            dps_ref[...] = jnp.zeros_like(dps_ref)
            dpw_ref[...] = jnp.zeros_like(dpw_ref)

        mov = mo_ref[...]
        r2 = _rsq(mov)
        n2 = mov * r2
        dx1v = dx1_ref[...]
        dg2_ref[...] += jnp.sum(dx1v * n2, axis=0, keepdims=True)
        dmo = _norm_bwd(dx1v * g2_ref[...], n2, r2).astype(BF16)
        dmo_ref[...] = dmo
        dmix_ref[...] = lax.dot_general(dmo, wout_ref[...], NT_DIMS, preferred_element_type=F32)

        has_prev = (tile > 0).astype(F32)

        for h in range(CONV_HEADS):
            lo = h * HEAD_DIM
            sl = slice(lo, lo + HEAD_DIM)
            gate_b = proj_ref[:, lo:lo + HEAD_DIM].astype(F32)
            gate_c = proj_ref[:, CONV_WIDTH + lo:CONV_WIDTH + lo + HEAD_DIM].astype(F32)
            uu = proj_ref[:, 2 * CONV_WIDTH + lo:2 * CONV_WIDTH + lo + HEAD_DIM].astype(F32)
            cu = gate_c * uu
            ext = jnp.concatenate([hc_ref[:, sl].astype(F32) * hu_ref[:, sl].astype(F32) * has_prev, cu], axis=0)
            c1 = pltpu.roll(ext, 1, 0)[HALO:]
            c2 = pltpu.roll(ext, 2, 0)[HALO:]
            w0, w1, w2 = cw_ref[h, 0:1, :], cw_ref[h, 1:2, :], cw_ref[h, 2:3, :]
            conv = w2 * cu + w1 * c1 + w0 * c2
            ya = gate_b * conv
            ra = _rsq(ya)
            dya = _norm_bwd(dmix_ref[:, sl], ya * ra, ra)
            dconv = dya * gate_b
            dcw_ref[h, 0:1, :] += jnp.sum(dconv * c2, axis=0, keepdims=True)
            dcw_ref[h, 1:2, :] += jnp.sum(dconv * c1, axis=0, keepdims=True)
            dcw_ref[h, 2:3, :] += jnp.sum(dconv * cu, axis=0, keepdims=True)
            extd = jnp.concatenate([dconv, dconv_carry[:, sl]], axis=0)
            d1 = pltpu.roll(extd, n_ext - 1, 0)[:tm]
            d2 = pltpu.roll(extd, n_ext - 2, 0)[:tm]
            dcu = w2 * dconv + w1 * d1 + w0 * d2
            dconv_carry[:, sl] = dconv[:HALO]
            dproj_ref[:, lo:lo + HEAD_DIM] = (dya * conv).astype(BF16)
            dproj_ref[:, CONV_WIDTH + lo:CONV_WIDTH + lo + HEAD_DIM] = (dcu * uu).astype(BF16)
            dproj_ref[:, 2 * CONV_WIDTH + lo:2 * CONV_WIDTH + lo + HEAD_DIM] = (dcu * gate_c).astype(BF16)

        for gi, w in enumerate(POOL_WINDOWS):
            lo = gi * POOL_GROUP_DIM
            sl = slice(lo, lo + POOL_GROUP_DIM)
            v = proj_ref[:, 3 * CONV_WIDTH + lo:3 * CONV_WIDTH + lo + POOL_GROUP_DIM].astype(F32)
            inv = _inv_count(tile * tm, tm, w)
            ext = jnp.concatenate([hv_ref[:, sl].astype(F32) * has_prev, v], axis=0)
            pooled = (_window_sum(ext, w, True)[HALO:] * inv - v).astype(BF16)
            y = jnp.dot(pooled, pw_ref[gi], preferred_element_type=F32)
            rp = _rsq(y)
            nb_ = y * rp
            dyb = dmix_ref[:, CONV_WIDTH + lo:CONV_WIDTH + lo + POOL_GROUP_DIM]
            dps_ref[:, sl] += jnp.sum(dyb * nb_, axis=0, keepdims=True)
            dy = _norm_bwd(dyb * ps_ref[:, sl], nb_, rp).astype(BF16)
            dpw_ref[gi] += lax.dot_general(pooled, dy, TN_DIMS, preferred_element_type=F32)
            dpooled = lax.dot_general(dy, pw_ref[gi], NT_DIMS, preferred_element_type=F32)
            q = dpooled * inv
            extq = jnp.concatenate([q, q_carry[:, sl]], axis=0)
            dv = _window_sum(extq, w, False)[:tm] - dpooled
            q_carry[:, sl] = q[:HALO]
            dproj_ref[:, 3 * CONV_WIDTH + lo:3 * CONV_WIDTH + lo + POOL_GROUP_DIM] = dv.astype(BF16)

    rev = lambda n: pl.BlockSpec((tm, n), lambda i: (nt - 1 - i, 0))

    def halo(col):
        return pl.BlockSpec((HALO, CONV_WIDTH), lambda i: (jnp.maximum((nt - 1 - i) * hb - 1, 0), col))

    return _host_call(
        body, name="mixer_bwd", grid=(nt,), comm=comm, args=(dx1, mo, proj, proj, proj, proj, cw, pw, ps, wout, g2),
        in_specs=[rev(D), rev(D), rev(P), halo(1), halo(2), halo(3), _whole(cw.shape), _whole(pw.shape),
                  _whole(ps.shape), _whole(wout.shape), _whole(g2.shape)],
        out_specs=[rev(D), rev(P), pl.BlockSpec((1, D), lambda i: (0, 0)),
                   pl.BlockSpec(cw.shape, lambda i: (0, 0, 0)), pl.BlockSpec(ps.shape, lambda i: (0, 0)),
                   pl.BlockSpec(pw.shape, lambda i: (0, 0, 0))],
        out_shape=[jax.ShapeDtypeStruct((T, D), BF16), jax.ShapeDtypeStruct((T, P), BF16),
                   jax.ShapeDtypeStruct((1, D), F32), jax.ShapeDtypeStruct(cw.shape, F32),
                   jax.ShapeDtypeStruct(ps.shape, F32), jax.ShapeDtypeStruct(pw.shape, F32)],
        scratch_shapes=[pltpu.VMEM((tm, D), F32), pltpu.VMEM((HALO, CONV_WIDTH), F32),
                        pltpu.VMEM((HALO, CONV_WIDTH), F32)])


def _inproj_bwd(dproj, win, x, dx1, g1, comm, tm=256):
    T, D = x.shape
    nblk, _, bn = win.shape

    def body(dp_ref, w_ref, x_ref, dx1_ref, g1_ref, gx_ref, dg1_ref):
        @pl.when(pl.program_id(0) == 0)
        def _():
            dg1_ref[...] = jnp.zeros_like(dg1_ref)

        dh = lax.dot_general(dp_ref[:, 0:bn], w_ref[0], NT_DIMS, preferred_element_type=F32)
        for k in range(1, nblk):
            dh = dh + lax.dot_general(dp_ref[:, k * bn:(k + 1) * bn], w_ref[k], NT_DIMS,
                                      preferred_element_type=F32)
        xv = x_ref[...]
        r = _rsq(xv)
        n = xv * r
        dg1_ref[...] += jnp.sum(dh * n, axis=0, keepdims=True)
        gx_ref[...] = dx1_ref[...] + _norm_bwd(dh * g1_ref[...], n, r)

    row = pl.BlockSpec((tm, D), lambda i: (i, 0))
    vec = pl.BlockSpec((1, D), lambda i: (0, 0))
    return _host_call(
        body, name="inproj_bwd", grid=(T // tm,), comm=comm, args=(dproj, win, x, dx1, g1),
        in_specs=[pl.BlockSpec((tm, nblk * bn), lambda i: (i, 0)), _whole(win.shape), row, row, vec],
        out_specs=[row, vec],
        out_shape=[jax.ShapeDtypeStruct((T, D), F32), jax.ShapeDtypeStruct((1, D), F32)])


def _adamw(w, g, m, v):
    m = ADAM_B1 * m + (1.0 - ADAM_B1) * g
    v = ADAM_B2 * v + (1.0 - ADAM_B2) * jnp.square(g)
    m_hat = m / (1.0 - ADAM_B1 ** ADAM_STEP)
    v_hat = v / (1.0 - ADAM_B2 ** ADAM_STEP)
    delta = -ADAM_LR * (m_hat / (jnp.sqrt(v_hat) + ADAM_EPS) + ADAM_WD * w)
    return delta, m, v


def _sum_adamw(parts, w, m, v, name, tr):
    r, cd = w.shape

    def body(p_ref, w_ref, m_ref, v_ref, g_ref, d_ref, mo_ref, vo_ref):
        g = p_ref[0].astype(F32)
        for k in range(1, NDEV):
            g = g + p_ref[k].astype(F32)
        g_ref[...] = g
        d_ref[...], mo_ref[...], vo_ref[...] = _adamw(w_ref[...], g, m_ref[...], v_ref[...])

    blk = pl.BlockSpec((tr, cd), lambda i: (i, 0))
    shp = jax.ShapeDtypeStruct((r, cd), F32)
    return pl.pallas_call(
        body, name=name, grid=(r // tr,),
        in_specs=[pl.BlockSpec((NDEV, tr, cd), lambda i: (0, i, 0)), blk, blk, blk],
        out_specs=[blk] * 4, out_shape=[shp] * 4,
        compiler_params=_params("arbitrary"),
    )(parts, w, m, v)


def _small_reduce_adamw(loss_part, vec_grads, dps, dcw, dpw, vec_state, ps_state, cw_state, pw_state):
    D = vec_grads[0].shape[1]
    pw_rows = pw_state[0].shape[1]
    states = list(vec_state) + [ps_state, cw_state, pw_state]
    n_in = 1 + 4 + 3 + 3 * len(states)
    n_out = 1 + 4 * len(states)

    def body(*refs):
        loss_ref, dg = refs[0], refs[1:5]
        dps_ref, dcw_ref, dpw_ref = refs[5:8]
        st = refs[8:n_in]
        loss_out, outs = refs[n_in], refs[n_in + 1:n_in + n_out]
        pack, gat, cbuf, pbuf, send_sems, recv_sems, local_sems = refs[n_in + n_out:]
        x, y, c = _coords()
        me = _device_index((x, y, c))

        pack[...] = jnp.zeros_like(pack)
        for k in range(4):
            pack[k:k + 1, :] = dg[k][...]
        pack[4:5, 0:dps_ref.shape[1]] = dps_ref[...]
        pack[5:6, 0:loss_ref.shape[1]] = loss_ref[...]

        def pw_slice(i):
            return dpw_ref.at[:, pl.ds(i * pw_rows, pw_rows), :]

        mine = [pltpu.make_async_copy(pack, gat.at[me], local_sems.at[0]),
                pltpu.make_async_copy(dcw_ref.at[me], cbuf.at[me], local_sems.at[1]),
                pltpu.make_async_copy(pw_slice(me), pbuf.at[me], local_sems.at[2])]
        for cp in mine:
            cp.start()
        sends, recvs = [], []
        for mask in range(1, NDEV):
            peer = (1 - x if mask & 4 else x, 1 - y if mask & 2 else y, 1 - c if mask & 1 else c)
            p = _device_index(peer)
            for k, (src, buf) in enumerate(((pack, gat), (dcw_ref.at[p], cbuf), (pw_slice(p), pbuf))):
                kw = dict(send_sem=send_sems.at[mask, k], recv_sem=recv_sems.at[mask, k],
                          device_id=peer, device_id_type=MESH)
                sends.append(pltpu.make_async_remote_copy(src_ref=src, dst_ref=buf.at[me], **kw))
                recvs.append(pltpu.make_async_remote_copy(src_ref=src, dst_ref=buf.at[p], **kw))
                sends[-1].start()
        for cp in recvs:
            cp.wait_recv()
        for cp in sends:
            cp.wait_send()
        for cp in mine:
            cp.wait()

        def slot_sum(buf):
            s = buf[0]
            for k in range(1, NDEV):
                s = s + buf[k]
            return s

        vec = slot_sum(gat)
        loss_out[...] = vec[5:6, 0:loss_ref.shape[1]]
        grads = [vec[k:k + 1, :] for k in range(4)] + [vec[4:5, 0:dps_ref.shape[1]], slot_sum(cbuf), slot_sum(pbuf)]
        for k, g in enumerate(grads):
            w_ref, m_ref, v_ref = st[3 * k:3 * k + 3]
            outs[4 * k][...] = g
            outs[4 * k + 1][...], outs[4 * k + 2][...], outs[4 * k + 3][...] = _adamw(
                w_ref[...], g, m_ref[...], v_ref[...])

    flat_state = [a for s in states for a in s]
    out_shape = [jax.ShapeDtypeStruct(loss_part.shape, F32)]
    out_shape += [jax.ShapeDtypeStruct(s[0].shape, F32) for s in states for _ in range(4)]
    return pl.pallas_call(
        body, name="small_reduce_adamw",
        in_specs=[VMEM_SPEC] * n_in, out_specs=[VMEM_SPEC] * n_out, out_shape=out_shape,
        scratch_shapes=[pltpu.VMEM((NDEV, D), F32), pltpu.VMEM((NDEV, NDEV, D), F32),
                        pltpu.VMEM((NDEV,) + cw_state[0].shape, F32), pltpu.VMEM((NDEV,) + pw_state[0].shape, F32),
                        pltpu.SemaphoreType.DMA((NDEV, 3)), pltpu.SemaphoreType.DMA((NDEV, 3)),
                        pltpu.SemaphoreType.DMA((3,))],
        compiler_params=pltpu.CompilerParams(vmem_limit_bytes=VMEM_LIMIT),
    )(loss_part, *vec_grads, dps, dcw, dpw, *flat_state)


ROW_TILE = dict(w_in=512, w_gate=176, w_up=176, w_down=176, w_out=128)
FORWARD_STEP = dict(inproj=56, mixer_fwd=26, ffn_up=32)


def kernel(x, ln_mix_pre, w_in, conv_w, pool_w, pool_scale, w_out, ln_mix_post, ln_ffn_pre, w_gate, w_up, w_down, ln_ffn_post, loss_target, m_ln_mix_pre, m_w_in, m_conv_w, m_pool_w, m_pool_scale, m_w_out, m_ln_mix_post, m_ln_ffn_pre, m_w_gate, m_w_up, m_w_down, m_ln_ffn_post, v_ln_mix_pre, v_w_in, v_conv_w, v_pool_w, v_pool_scale, v_w_out, v_ln_mix_post, v_ln_ffn_pre, v_w_gate, v_w_up, v_w_down, v_ln_ffn_post):
    D = x.shape[2]
    xs, tgt = x[0], loss_target[0]
    flip = lambda arr: jnp.swapaxes(arr[0], 0, 1)
    win, wg, wu, wd, wout, pw, cw = _cast_gather_first(
        [w_in[0], flip(w_gate), flip(w_up), w_down[0], w_out[0], pool_w[0], conv_w[0]])

    (proj, h), (wout, wg) = _inproj(xs, ln_mix_pre, win, _gather_comm([wout, wg], FORWARD_STEP["inproj"]))
    wout2 = wout.reshape(D, D)
    (x1, hf, mixed, mo), (wu,) = _mixer_fwd(proj, xs, cw, pw, pool_scale, wout2, ln_mix_post, ln_ffn_pre,
                                            _gather_comm([wu], FORWARD_STEP["mixer_fwd"]))
    (g, u, a), (wd,) = _ffn_up(hf, wg, wu, _gather_comm([wd], FORWARD_STEP["ffn_up"]))
    dy, dff, loss, dg4 = _ffn_down_loss(a, wd, x1, tgt, ln_ffn_post)

    rows = lambda arr, lo, hi: (lo * arr.shape[1] // 4, (hi - lo) * arr.shape[1] // 4)
    dwd, _ = _wgrad_rows("wgrad_down", a, dff)
    (dg, du), (dwd_parts,) = _ffn_bwd_act(dff, wd, g, u, _scatter_comm([(dwd, None, *rows(dwd, 0, 2))]))
    dwg, (dwd_parts,) = _wgrad_rows("wgrad_gate", dg, hf, _scatter_comm([(dwd, dwd_parts, *rows(dwd, 2, 4))]))
    dwu, (dwg_parts,) = _wgrad_rows("wgrad_up", du, hf, _scatter_comm([(dwg, None, *rows(dwg, 0, 2))]))
    (dx1, dg3), (dwg_parts,), (dwu_parts,) = _ffn_bwd_in(
        dg, du, wg, wu, x1, dy, ln_ffn_pre, _scatter_comm([(dwg, dwg_parts, *rows(dwg, 2, 4))]),
        _scatter_comm([(dwu, None, *rows(dwu, 0, 2))]))
    (dmo, dproj, dg2, dcw, dps, dpw), (dwu_parts,) = _mixer_bwd(
        dx1, mo, proj, cw, pw, pool_scale, wout2, ln_mix_post, _scatter_comm([(dwu, dwu_parts, *rows(dwu, 2, 4))]))
    dwin, _ = _wgrad_in(h, dproj, None)
    dwout, (dwin_parts,) = _wgrad_out(mixed, dmo, _scatter_comm([(dwin, None, *rows(dwin, 0, 2))]))
    (gx, dg1), (dwin_parts, dwout_parts) = _inproj_bwd(
        dproj, win, xs, dx1, ln_mix_pre,
        _scatter_comm([(dwin, dwin_parts, *rows(dwin, 2, 4)), (dwout, None, *rows(dwout, 0, 4))]))

    res = {}
    for k, parts, w, m, v in (("w_down", dwd_parts, w_down, m_w_down, v_w_down),
                              ("w_out", dwout_parts, w_out, m_w_out, v_w_out),
                              ("w_in", dwin_parts, w_in, m_w_in, v_w_in)):
        res[k] = [o.reshape(w.shape) for o in _sum_adamw(parts, w[0], m[0], v[0], "sum_adamw_" + k, ROW_TILE[k])]
    for k, parts, w, m, v in (("w_gate", dwg_parts, w_gate, m_w_gate, v_w_gate),
                              ("w_up", dwu_parts, w_up, m_w_up, v_w_up)):
        outs = _sum_adamw(parts, flip(w), flip(m), flip(v), "sum_adamw_" + k, ROW_TILE[k])
        res[k] = [jnp.swapaxes(o, 0, 1)[None] for o in outs]

    loss_sum, *small = _small_reduce_adamw(
        loss, [dg1, dg2, dg3, dg4], dps, dcw, dpw,
        [(ln_mix_pre, m_ln_mix_pre, v_ln_mix_pre), (ln_mix_post, m_ln_mix_post, v_ln_mix_post),
         (ln_ffn_pre, m_ln_ffn_pre, v_ln_ffn_pre), (ln_ffn_post, m_ln_ffn_post, v_ln_ffn_post)],
        (pool_scale, m_pool_scale, v_pool_scale), (conv_w[0], m_conv_w[0], v_conv_w[0]),
        (pool_w[0], m_pool_w[0], v_pool_w[0]))
    small_names = ["ln_mix_pre", "ln_mix_post", "ln_ffn_pre", "ln_ffn_post", "pool_scale", "conv_w", "pool_w"]
    shapes = dict(conv_w=conv_w.shape, pool_w=pool_w.shape)
    for i, k in enumerate(small_names):
        res[k] = [o.reshape(shapes[k]) if k in shapes else o for o in small[4 * i:4 * i + 4]]

    order = ["ln_mix_pre", "w_in", "conv_w", "pool_w", "pool_scale", "w_out", "ln_mix_post", "ln_ffn_pre",
             "w_gate", "w_up", "w_down", "ln_ffn_post"]
    return (loss_sum[0, 0], gx[None], *[res[k][0] for k in order], *[res[k][1] for k in order],
            *[res[k][2] for k in order], *[res[k][3] for k in order])
```

```python
import functools
from typing import Any, NamedTuple

import jax
import jax.numpy as jnp
from jax import lax
from jax.experimental import pallas as pl
from jax.experimental.pallas import tpu as pltpu

EPS = 1e-6
NDEV = 8
CONV_HEADS = 8
HEAD_DIM = 128
CONV_WIDTH = CONV_HEADS * HEAD_DIM
POOL_WINDOWS = (2, 4, 8, 16)
POOL_GROUP_DIM = 256
HALO = 16

ADAM_LR = 0.001
ADAM_B1 = 0.9
ADAM_B2 = 0.999
ADAM_EPS = 1e-08
ADAM_WD = 0.01
ADAM_STEP = 10

F32 = jnp.float32
BF16 = jnp.bfloat16
VMEM_LIMIT = 58 * 1024 * 1024
MESH = pl.DeviceIdType.MESH
HBM_SPEC = pl.BlockSpec(memory_space=pl.ANY)
VMEM_SPEC = pl.BlockSpec(memory_space=pltpu.VMEM)

NT_DIMS = (((1,), (1,)), ((), ()))
TN_DIMS = (((0,), (0,)), ((), ()))


def _params(*sem):
    return pltpu.CompilerParams(dimension_semantics=sem, vmem_limit_bytes=VMEM_LIMIT)


def _rsq(v):
    return lax.rsqrt(jnp.mean(v * v, axis=-1, keepdims=True) + EPS)


def _norm_bwd(dn, n, r):
    return r * (dn - n * jnp.mean(dn * n, axis=-1, keepdims=True))


def _whole(shape):
    nd = len(shape)
    return pl.BlockSpec(shape, lambda *_: (0,) * nd, pipeline_mode=pl.Buffered(1))


def _inv_count(t0, tm, w):
    t = t0 + lax.broadcasted_iota(jnp.int32, (tm, 1), 0)
    return 1.0 / jnp.minimum(t + 1, w).astype(F32)


def _window_sum(ext, w, back):
    n = ext.shape[0]
    s, shift = ext, 1
    while shift < w:
        s = s + pltpu.roll(s, shift if back else n - shift, 0)
        shift *= 2
    return s


class _Comm(NamedTuple):
    arrays: Any
    out_shape: Any
    aliases: Any
    scratch: Any
    hooks: Any


def _coords():
    return lax.axis_index("x"), lax.axis_index("y"), lax.axis_index("c")


def _other_chips(x, y):
    return [(1 - x, y), (x, 1 - y), (1 - x, 1 - y)]


def _device_index(dev):
    return 4 * dev[0] + 2 * dev[1] + dev[2]


def _host_call(body, *, name, grid, in_specs, out_specs, out_shape, args, scratch_shapes=(), comm=None):
    sem = ("arbitrary",) * len(grid)
    in_specs, out_specs, out_shape, scratch_shapes = list(in_specs), list(out_specs), list(out_shape), list(scratch_shapes)
    if comm is None:
        res = pl.pallas_call(body, name=name, grid=grid, in_specs=in_specs, out_specs=out_specs, out_shape=out_shape,
                             scratch_shapes=scratch_shapes, compiler_params=_params(*sem))(*args)
        return res, []
    n_in, n_out, n_scr = len(in_specs), len(out_specs), len(scratch_shapes)
    n_cin, n_cout = len(comm.arrays), len(comm.out_shape)
    total = functools.reduce(lambda a, b: a * b, grid)

    def wrapped(*refs):
        ins, cin = refs[:n_in], refs[n_in:n_in + n_cin]
        o0 = n_in + n_cin
        outs, cout = refs[o0:o0 + n_out], refs[o0 + n_out:o0 + n_out + n_cout]
        s0 = o0 + n_out + n_cout
        scr, sems = refs[s0:s0 + n_scr], refs[s0 + n_scr:]
        step = pl.program_id(0)
        for d in range(1, len(grid)):
            step = step * grid[d] + pl.program_id(d)
        for when, before, fn in comm.hooks:
            if before:
                pl.when(step == when % total)(functools.partial(fn, cin, cout, sems))
        body(*ins, *outs, *scr)
        for when, before, fn in comm.hooks:
            if not before:
                pl.when(step == when % total)(functools.partial(fn, cin, cout, sems))

    res = pl.pallas_call(
        wrapped, name=name, grid=grid,
        in_specs=in_specs + [HBM_SPEC] * n_cin, out_specs=out_specs + [HBM_SPEC] * n_cout,
        out_shape=out_shape + list(comm.out_shape), scratch_shapes=scratch_shapes + list(comm.scratch),
        input_output_aliases={n_in + i: n_out + o for i, o in comm.aliases.items()},
        compiler_params=_params(*sem),
    )(*args, *comm.arrays)
    return res[:n_out], res[n_out:]


def _gather_steps(n, view, own_src, send_sems, recv_sems):
    x, y, c = _coords()
    me, sibling = (x, y, c), (x, y, 1 - c)
    chips = _other_chips(x, y)

    def copy(a, k, block, to, src=None):
        return pltpu.make_async_remote_copy(
            src_ref=view(a, block) if src is None else src, dst_ref=view(a, block),
            send_sem=send_sems.at[a, k], recv_sem=recv_sems.at[a, k], device_id=to, device_id_type=MESH)

    def first_copies():
        cps = []
        for a in range(n):
            cps.append(copy(a, 0, me, sibling, src=own_src(a)))
            cps += [copy(a, 1 + j, me, (*chip, c), src=own_src(a)) for j, chip in enumerate(chips)]
        return cps

    def passed_copies():
        return [copy(a, 4 + j, (*chip, c), sibling) for j, chip in enumerate(chips) for a in range(n)]

    def first():
        for cp in first_copies():
            cp.start()

    def forward():
        for j, chip in enumerate(chips):
            for a in range(n):
                copy(a, 1 + j, (*chip, c), me).wait_recv()
                copy(a, 4 + j, (*chip, c), sibling).start()

    def finish():
        for a in range(n):
            copy(a, 0, sibling, me).wait_recv()
            for j, chip in enumerate(chips):
                copy(a, 4 + j, (*chip, 1 - c), me).wait_recv()
        for cp in first_copies() + passed_copies():
            cp.wait_send()

    return first, forward, finish


def _gather_comm(arrays, forward_step):
    n = len(arrays)

    def steps(cout, sems):
        view = lambda a, dev: cout[a].at[_device_index(dev)]
        return _gather_steps(n, view, lambda a: view(a, _coords()), sems[0], sems[1])

    hooks = [(0, True, lambda cin, cout, sems: steps(cout, sems)[0]()),
             (forward_step, True, lambda cin, cout, sems: steps(cout, sems)[1]()),
             (-1, False, lambda cin, cout, sems: steps(cout, sems)[2]())]
    return _Comm(list(arrays), [jax.ShapeDtypeStruct(a.shape, a.dtype) for a in arrays], {i: i for i in range(n)},
                 [pltpu.SemaphoreType.DMA((n, 7)), pltpu.SemaphoreType.DMA((n, 7))], hooks)


def _scatter_comm(items):
    n = len(items)
    arrays, aliases, grad_at = [], {}, []
    for k, (grad, slots, _, _) in enumerate(items):
        grad_at.append(len(arrays))
        arrays.append(grad)
        if slots is not None:
            aliases[len(arrays)] = k
            arrays.append(slots)

    def copies(cin, cout, sems):
        send_sems, recv_sems, local_sems = sems
        x, y, c = _coords()
        me = _device_index((x, y, c))
        src = lambda a, p: cin[grad_at[a]].at[p, pl.ds(items[a][2], items[a][3]), :]
        dst = lambda a, p: cout[a].at[p, pl.ds(items[a][2], items[a][3]), :]
        mine = [pltpu.make_async_copy(src(a, me), dst(a, me), local_sems.at[a]) for a in range(n)]
        sends, recvs = [], []
        for a in range(n):
            for mask in range(1, NDEV):
                peer = (1 - x if mask & 4 else x, 1 - y if mask & 2 else y, 1 - c if mask & 1 else c)
                p = _device_index(peer)
                kw = dict(send_sem=send_sems.at[a, mask - 1], recv_sem=recv_sems.at[a, mask - 1],
                          device_id=peer, device_id_type=MESH)
                sends.append(pltpu.make_async_remote_copy(src_ref=src(a, p), dst_ref=dst(a, me), **kw))
                recvs.append(pltpu.make_async_remote_copy(src_ref=src(a, p), dst_ref=dst(a, p), **kw))
        return mine, sends, recvs

    def start(cin, cout, sems):
        mine, sends, _ = copies(cin, cout, sems)
        for cp in mine + sends:
            cp.start()

    def finish(cin, cout, sems):
        mine, sends, recvs = copies(cin, cout, sems)
        for cp in recvs:
            cp.wait_recv()
        for cp in sends:
            cp.wait_send()
        for cp in mine:
            cp.wait()

    return _Comm(arrays, [jax.ShapeDtypeStruct(it[0].shape, it[0].dtype) for it in items], aliases,
                 [pltpu.SemaphoreType.DMA((n, NDEV - 1)), pltpu.SemaphoreType.DMA((n, NDEV - 1)),
                  pltpu.SemaphoreType.DMA((n,))],
                 [(0, True, start), (-1, False, finish)])


NOW_ITEMS = (0, 5, 6)
FLIPPED_ITEMS = (1, 2)
POOL_ITEM = 5


def _cast_gather_first(shards):
    n = len(shards)
    dtypes = [BF16] * 6 + [F32]
    block_shapes = [s.shape[::-1] if a in FLIPPED_ITEMS else s.shape for a, s in enumerate(shards)]
    out_shapes = [(NDEV,) + s for s in block_shapes]
    g, rows, cols = shards[POOL_ITEM].shape
    out_shapes[POOL_ITEM] = (g, rows * NDEV, cols)

    later = [a for a in range(n) if a not in NOW_ITEMS]

    def body(*refs):
        ins, outs, raw, stage = refs[:n], refs[n:2 * n], refs[2 * n:3 * n], refs[3 * n:4 * n]
        send_sems, recv_sems, local_sems, load_sems = refs[4 * n:]

        def view(a, dev):
            i = _device_index(dev)
            if a == POOL_ITEM:
                return outs[a].at[:, pl.ds(i * rows, rows), :]
            return outs[a].at[i]

        loads = [pltpu.make_async_copy(ins[a], raw[a], load_sems.at[a]) for a in range(n)]
        mine = [pltpu.make_async_copy(stage[a], view(a, _coords()), local_sems.at[a]) for a in range(n)]
        for a in list(NOW_ITEMS) + later:
            loads[a].start()
        first, forward, finish = _gather_steps(
            len(NOW_ITEMS), lambda k, dev: view(NOW_ITEMS[k], dev), lambda k: stage[NOW_ITEMS[k]], send_sems, recv_sems)
        for a in list(NOW_ITEMS) + later:
            loads[a].wait()
            if a in FLIPPED_ITEMS:
                k = raw[a].shape[0]
                eye = (lax.broadcasted_iota(jnp.int32, (k, k), 0) == lax.broadcasted_iota(jnp.int32, (k, k), 1))
                stage[a][...] = lax.dot_general(raw[a][...].astype(BF16), eye.astype(BF16), TN_DIMS,
                                                preferred_element_type=F32).astype(BF16)
            else:
                stage[a][...] = raw[a][...].astype(dtypes[a])
            mine[a].start()
            if a == NOW_ITEMS[-1]:
                first()
        forward()
        finish()
        for cp in mine:
            cp.wait()

    return pl.pallas_call(
        body, name="cast_gather_first",
        in_specs=[HBM_SPEC] * n, out_specs=[HBM_SPEC] * n,
        out_shape=[jax.ShapeDtypeStruct(s, d) for s, d in zip(out_shapes, dtypes)],
        scratch_shapes=[pltpu.VMEM(s.shape, s.dtype) for s in shards]
        + [pltpu.VMEM(s, d) for s, d in zip(block_shapes, dtypes)]
        + [pltpu.SemaphoreType.DMA((len(NOW_ITEMS), 7)), pltpu.SemaphoreType.DMA((len(NOW_ITEMS), 7)),
           pltpu.SemaphoreType.DMA((n,)), pltpu.SemaphoreType.DMA((n,))],
        compiler_params=pltpu.CompilerParams(vmem_limit_bytes=VMEM_LIMIT),
    )(*shards)


def _inproj(x, g1, win, comm, tm=1024):
    T, D = x.shape
    nb, _, bn = win.shape

    def body(x_ref, g_ref, w_ref, proj_ref, h_ref):
        @pl.when(pl.program_id(1) == 0)
        def _():
            xv = x_ref[...]
            h_ref[...] = (xv * _rsq(xv) * g_ref[...]).astype(BF16)

        proj_ref[...] = jnp.dot(h_ref[...], w_ref[0], preferred_element_type=F32).astype(BF16)

    return _host_call(
        body, name="inproj", grid=(T // tm, nb), comm=comm, args=(x, g1, win),
        in_specs=[pl.BlockSpec((tm, D), lambda i, j: (i, 0)),
                  pl.BlockSpec((1, D), lambda i, j: (0, 0)),
                  pl.BlockSpec((1, D, bn), lambda i, j: (j, 0, 0))],
        out_specs=[pl.BlockSpec((tm, bn), lambda i, j: (i, j)),
                   pl.BlockSpec((tm, D), lambda i, j: (i, 0))],
        out_shape=[jax.ShapeDtypeStruct((T, nb * bn), BF16), jax.ShapeDtypeStruct((T, D), BF16)])


def _mixer_fwd(proj, x, cw, pw, ps, wout, g2, g3, comm, tm=256):
    T, D = x.shape
    P = proj.shape[1]

    def body(proj_ref, x_ref, cw_ref, pw_ref, ps_ref, wout_ref, g2_ref, g3_ref,
             x1_ref, hf_ref, mixed_ref, mo_ref, cu_carry, v_carry):
        i = pl.program_id(0)

        @pl.when(i == 0)
        def _():
            cu_carry[...] = jnp.zeros_like(cu_carry)
            v_carry[...] = jnp.zeros_like(v_carry)

        for h in range(CONV_HEADS):
            lo = h * HEAD_DIM
            gate_b = proj_ref[:, lo:lo + HEAD_DIM].astype(F32)
            cu = proj_ref[:, CONV_WIDTH + lo:CONV_WIDTH + lo + HEAD_DIM].astype(F32) * \
                proj_ref[:, 2 * CONV_WIDTH + lo:2 * CONV_WIDTH + lo + HEAD_DIM].astype(F32)
            ext = jnp.concatenate([cu_carry[:, lo:lo + HEAD_DIM], cu], axis=0)
            c1 = pltpu.roll(ext, 1, 0)[HALO:]
            c2 = pltpu.roll(ext, 2, 0)[HALO:]
            ya = gate_b * (cw_ref[h, 2:3, :] * cu + cw_ref[h, 1:2, :] * c1 + cw_ref[h, 0:1, :] * c2)
            mixed_ref[:, lo:lo + HEAD_DIM] = (ya * _rsq(ya)).astype(BF16)
            cu_carry[:, lo:lo + HEAD_DIM] = cu[tm - HALO:]

        for gi, w in enumerate(POOL_WINDOWS):
            lo = gi * POOL_GROUP_DIM
            v = proj_ref[:, 3 * CONV_WIDTH + lo:3 * CONV_WIDTH + lo + POOL_GROUP_DIM].astype(F32)
            ext = jnp.concatenate([v_carry[:, lo:lo + POOL_GROUP_DIM], v], axis=0)
            pooled = _window_sum(ext, w, True)[HALO:] * _inv_count(i * tm, tm, w) - v
            y = jnp.dot(pooled.astype(BF16), pw_ref[gi], preferred_element_type=F32)
            yb = y * _rsq(y) * ps_ref[:, lo:lo + POOL_GROUP_DIM]
            mixed_ref[:, CONV_WIDTH + lo:CONV_WIDTH + lo + POOL_GROUP_DIM] = yb.astype(BF16)
            v_carry[:, lo:lo + POOL_GROUP_DIM] = v[tm - HALO:]

        mo = jnp.dot(mixed_ref[...], wout_ref[...], preferred_element_type=F32)
        mo_ref[...] = mo
        x1 = x_ref[...] + mo * _rsq(mo) * g2_ref[...]
        x1_ref[...] = x1
        hf_ref[...] = (x1 * _rsq(x1) * g3_ref[...]).astype(BF16)

    row = lambda n: pl.BlockSpec((tm, n), lambda i: (i, 0))
    return _host_call(
        body, name="mixer_fwd", grid=(T // tm,), comm=comm, args=(proj, x, cw, pw, ps, wout, g2, g3),
        in_specs=[row(P), row(D), _whole(cw.shape), _whole(pw.shape), _whole(ps.shape),
                  _whole(wout.shape), _whole(g2.shape), _whole(g3.shape)],
        out_specs=[row(D), row(D), row(D), row(D)],
        out_shape=[jax.ShapeDtypeStruct((T, D), F32), jax.ShapeDtypeStruct((T, D), BF16),
                   jax.ShapeDtypeStruct((T, D), BF16), jax.ShapeDtypeStruct((T, D), F32)],
        scratch_shapes=[pltpu.VMEM((HALO, CONV_WIDTH), F32), pltpu.VMEM((HALO, CONV_WIDTH), F32)])


def _ffn_up(hf, wg, wu, comm, tm=1024):
    T, D = hf.shape
    nb, _, bf = wg.shape

    def body(hf_ref, wg_ref, wu_ref, g_ref, u_ref, a_ref):
        hv = hf_ref[...]
        g = jnp.dot(hv, wg_ref[0], preferred_element_type=F32)
        u = jnp.dot(hv, wu_ref[0], preferred_element_type=F32)
        g_ref[0] = g.astype(BF16)
        u_ref[0] = u.astype(BF16)
        a_ref[0] = (g * jax.nn.sigmoid(g) * u).astype(BF16)

    wspec = pl.BlockSpec((1, D, bf), lambda i, j: (j, 0, 0))
    ospec = pl.BlockSpec((1, tm, bf), lambda i, j: (j, i, 0))
    oshape = jax.ShapeDtypeStruct((nb, T, bf), BF16)
    return _host_call(
        body, name="ffn_up", grid=(T // tm, nb), comm=comm, args=(hf, wg, wu),
        in_specs=[pl.BlockSpec((tm, D), lambda i, j: (i, 0)), wspec, wspec],
        out_specs=[ospec, ospec, ospec], out_shape=[oshape, oshape, oshape])


def _ffn_down_loss(a, wd, x1, tgt, g4, tm=256):
    nblk, T, bf = a.shape
    D = x1.shape[1]
    nt = T // tm

    def body(a_ref, wd_ref, x1_ref, tgt_ref, g4_ref, dy_ref, dff_ref, loss_ref, dg4_ref, lacc_ref):
        i = pl.program_id(0)

        @pl.when(i == 0)
        def _():
            lacc_ref[...] = jnp.zeros_like(lacc_ref)
            dg4_ref[...] = jnp.zeros_like(dg4_ref)

        ff = jnp.dot(a_ref[0], wd_ref[0], preferred_element_type=F32)
        for k in range(1, nblk):
            ff = ff + jnp.dot(a_ref[k], wd_ref[k], preferred_element_type=F32)
        r = _rsq(ff)
        n = ff * r
        g4v = g4_ref[...]
        e = x1_ref[...] + n * g4v - tgt_ref[...]
        lacc_ref[...] += jnp.sum(e * e, axis=0, keepdims=True)
        dy = e * (1.0 / D)
        dy_ref[...] = dy
        dg4_ref[...] += jnp.sum(dy * n, axis=0, keepdims=True)
        dff_ref[...] = _norm_bwd(dy * g4v, n, r).astype(BF16)

        @pl.when(i == nt - 1)
        def _():
            loss_ref[...] = jnp.full(loss_ref.shape, (0.5 / D) * jnp.sum(lacc_ref[...]), F32)

    row = pl.BlockSpec((tm, D), lambda i: (i, 0))
    vec = pl.BlockSpec((1, D), lambda i: (0, 0))
    return _host_call(
        body, name="ffn_down_loss", grid=(nt,), args=(a, wd, x1, tgt, g4),
        in_specs=[pl.BlockSpec((nblk, tm, bf), lambda i: (0, i, 0)), _whole(wd.shape), row, row, vec],
        out_specs=[row, row, pl.BlockSpec((1, 128), lambda i: (0, 0)), vec],
        out_shape=[jax.ShapeDtypeStruct((T, D), F32), jax.ShapeDtypeStruct((T, D), BF16),
                   jax.ShapeDtypeStruct((1, 128), F32), jax.ShapeDtypeStruct((1, D), F32)],
        scratch_shapes=[pltpu.VMEM((1, D), F32)])[0]


def _ffn_bwd_act(dff, wd, g, u, comm, tm=1024):
    T, D = dff.shape
    nb, bf, _ = wd.shape

    def body(dff_ref, wd_ref, g_ref, u_ref, dg_ref, du_ref):
        da = lax.dot_general(dff_ref[...], wd_ref[0], NT_DIMS, preferred_element_type=F32)
        gv = g_ref[0].astype(F32)
        s = jax.nn.sigmoid(gv)
        du_ref[0] = (da * (gv * s)).astype(BF16)
        dg_ref[0] = (da * u_ref[0].astype(F32) * (s * (1.0 + gv * (1.0 - s)))).astype(BF16)

    blk = pl.BlockSpec((1, tm, bf), lambda i, j: (j, i, 0))
    oshape = jax.ShapeDtypeStruct((nb, T, bf), BF16)
    return _host_call(
        body, name="ffn_bwd_act", grid=(T // tm, nb), comm=comm, args=(dff, wd, g, u),
        in_specs=[pl.BlockSpec((tm, D), lambda i, j: (i, 0)),
                  pl.BlockSpec((1, bf, D), lambda i, j: (j, 0, 0)), blk, blk],
        out_specs=[blk, blk], out_shape=[oshape, oshape])


def _ffn_bwd_in(dg, du, wg, wu, x1, dy, g3, comm_a, comm_b, tm=256):
    nb, T, bf = dg.shape
    D = x1.shape[1]
    hb = nb // 2

    def partial_sum(dg_ref, du_ref, wg_ref, wu_ref):
        s = lax.dot_general(dg_ref[0], wg_ref[0], NT_DIMS, preferred_element_type=F32)
        s = s + lax.dot_general(du_ref[0], wu_ref[0], NT_DIMS, preferred_element_type=F32)
        for k in range(1, hb):
            s = s + lax.dot_general(dg_ref[k], wg_ref[k], NT_DIMS, preferred_element_type=F32)
            s = s + lax.dot_general(du_ref[k], wu_ref[k], NT_DIMS, preferred_element_type=F32)
        return s

    def first(dg_ref, du_ref, wg_ref, wu_ref, part_ref):
        part_ref[...] = partial_sum(dg_ref, du_ref, wg_ref, wu_ref)

    def second(dg_ref, du_ref, wg_ref, wu_ref, part_ref, x1_ref, dy_ref, g3_ref, dx1_ref, dg3_ref):
        @pl.when(pl.program_id(0) == 0)
        def _():
            dg3_ref[...] = jnp.zeros_like(dg3_ref)

        dhf = part_ref[...] + partial_sum(dg_ref, du_ref, wg_ref, wu_ref)
        x1v = x1_ref[...]
        r = _rsq(x1v)
        n = x1v * r
        dg3_ref[...] += jnp.sum(dhf * n, axis=0, keepdims=True)
        dx1_ref[...] = dy_ref[...] + _norm_bwd(dhf * g3_ref[...], n, r)

    row = pl.BlockSpec((tm, D), lambda i: (i, 0))
    vec = pl.BlockSpec((1, D), lambda i: (0, 0))
    rowshape = jax.ShapeDtypeStruct((T, D), F32)

    def specs(half):
        ablk = pl.BlockSpec((hb, tm, bf), lambda i: (half, i, 0))
        wblk = pl.BlockSpec((hb, D, bf), lambda i: (half, 0, 0), pipeline_mode=pl.Buffered(1))
        return [ablk, ablk, wblk, wblk]

    (part,), c0 = _host_call(first, name="ffn_bwd_in_a", grid=(T // tm,), comm=comm_a, args=(dg, du, wg, wu),
                             in_specs=specs(0), out_specs=[row], out_shape=[rowshape])
    res, c1 = _host_call(second, name="ffn_bwd_in_b", grid=(T // tm,), comm=comm_b,
                         args=(dg, du, wg, wu, part, x1, dy, g3),
                         in_specs=specs(1) + [row, row, row, vec], out_specs=[row, vec],
                         out_shape=[rowshape, jax.ShapeDtypeStruct((1, D), F32)])
    return res, c0, c1


def _wgrad(name, lhs, rhs, lhs_spec, rhs_spec, n_rhs, M, N, nb, nk, comm=None):
    def body(*refs):
        l_ref, r_refs = refs[0], refs[1:1 + n_rhs]
        o_refs, acc_refs = refs[1 + n_rhs:1 + 2 * n_rhs], refs[1 + 2 * n_rhs:]
        k = pl.program_id(1)
        tile = lambda ref: ref[0] if len(ref.shape) == 3 else ref[...]

        @pl.when(k == 0)
        def _():
            for acc_ref in acc_refs:
                acc_ref[...] = jnp.zeros_like(acc_ref)

        for r_ref, acc_ref in zip(r_refs, acc_refs):
            acc_ref[...] += lax.dot_general(tile(l_ref), tile(r_ref), TN_DIMS, preferred_element_type=F32)

        @pl.when(k == nk - 1)
        def _():
            for o_ref, acc_ref in zip(o_refs, acc_refs):
                o_ref[0] = acc_ref[...].astype(BF16)

    oblk = pl.BlockSpec((1, M, N), lambda j, k: (j, 0, 0))
    oshape = jax.ShapeDtypeStruct((nb, M, N), BF16)
    return _host_call(
        body, name=name, grid=(nb, nk), comm=comm, args=(lhs, *rhs),
        in_specs=[lhs_spec] + [rhs_spec] * n_rhs, out_specs=[oblk] * n_rhs, out_shape=[oshape] * n_rhs,
        scratch_shapes=[pltpu.VMEM((M, N), F32)] * n_rhs)


WGRAD_TOKENS = 2048


def _wgrad_rows(name, blocks, rhs, comm=None):
    nb, T, M = blocks.shape
    N = rhs.shape[1]
    tk = min(WGRAD_TOKENS, T)
    res, cres = _wgrad(name, blocks, [rhs], pl.BlockSpec((1, tk, M), lambda j, k: (j, k, 0)),
                       pl.BlockSpec((tk, N), lambda j, k: (k, 0)), 1, M, N, nb, T // tk, comm=comm)
    return res[0], cres


def _wgrad_wide(name, lhs, rhs, n_split, kb, comm=None):
    T, M = lhs.shape
    N = rhs.shape[1]
    slab = N // n_split
    tk = min(WGRAD_TOKENS, T)
    nk = T // tk

    def body(l_ref, r_ref, o_ref, acc_ref):
        k = pl.program_id(1)

        @pl.when(k == 0)
        def _():
            acc_ref[...] = jnp.zeros_like(acc_ref)

        acc_ref[...] += lax.dot_general(l_ref[...], r_ref[...], TN_DIMS, preferred_element_type=F32)

        @pl.when(k == nk - 1)
        def _():
            if kb == 0:
                o_ref[...] = acc_ref[...].astype(BF16)
            for b in range(kb):
                o_ref[b] = acc_ref[:, b * (slab // kb):(b + 1) * (slab // kb)].astype(BF16)

    if kb == 0:
        out_spec, out_shape = pl.BlockSpec((M, slab), lambda j, k: (0, j)), (M, N)
    else:
        out_spec, out_shape = pl.BlockSpec((kb, M, slab // kb), lambda j, k: (j, 0, 0)), (n_split * kb, M, slab // kb)
    return _host_call(
        body, name=name, grid=(n_split, nk), comm=comm, args=(lhs, rhs),
        in_specs=[pl.BlockSpec((tk, M), lambda j, k: (k, 0)), pl.BlockSpec((tk, slab), lambda j, k: (k, j))],
        out_specs=[out_spec], out_shape=[jax.ShapeDtypeStruct(out_shape, BF16)],
        scratch_shapes=[pltpu.VMEM((M, slab), F32)])


def _wgrad_out(mixed, dmo, comm, nb=NDEV):
    D = mixed.shape[1]
    res, cres = _wgrad_wide("wgrad_out", mixed, dmo, 2, 0, comm=comm)
    return res[0].reshape(nb, D // nb, D), cres


def _wgrad_in(h, dproj, comm, nb=NDEV):
    res, cres = _wgrad_wide("wgrad_in", h, dproj, nb // 2, 2, comm=comm)
    return res[0], cres


def _mixer_bwd(dx1, mo, proj, cw, pw, ps, wout, g2, comm, tm=256):
    T, D = dx1.shape
    P = proj.shape[1]
    nt = T // tm
    n_ext = tm + HALO
    hb = tm // HALO

    def body(dx1_ref, mo_ref, proj_ref, hc_ref, hu_ref, hv_ref, cw_ref, pw_ref, ps_ref, wout_ref, g2_ref,
             dmo_ref, dproj_ref, dg2_ref, dcw_ref, dps_ref, dpw_ref, dmix_ref, dconv_carry, q_carry):
        i = pl.program_id(0)
        tile = nt - 1 - i

        @pl.when(i == 0)
        def _():
            dconv_carry[...] = jnp.zeros_like(dconv_carry)
            q_carry[...] = jnp.zeros_like(q_carry)
            dg2_ref[...] = jnp.zeros_like(dg2_ref)
            dcw_ref[...] = jnp.zeros_like(dcw_ref)
            dps_ref[...] = jnp.zeros_like(dps_ref)
            dpw_ref[...] = jnp.zeros_like(dpw_ref)

        mov = mo_ref[...]
        r2 = _rsq(mov)
        n2 = mov * r2
        dx1v = dx1_ref[...]
        dg2_ref[...] += jnp.sum(dx1v * n2, axis=0, keepdims=True)
        dmo = _norm_bwd(dx1v * g2_ref[...], n2, r2).astype(BF16)
        dmo_ref[...] = dmo
        dmix_ref[...] = lax.dot_general(dmo, wout_ref[...], NT_DIMS, preferred_element_type=F32)

        has_prev = (tile > 0).astype(F32)

        for h in range(CONV_HEADS):
            lo = h * HEAD_DIM
            sl = slice(lo, lo + HEAD_DIM)
            gate_b = proj_ref[:, lo:lo + HEAD_DIM].astype(F32)
            gate_c = proj_ref[:, CONV_WIDTH + lo:CONV_WIDTH + lo + HEAD_DIM].astype(F32)
            uu = proj_ref[:, 2 * CONV_WIDTH + lo:2 * CONV_WIDTH + lo + HEAD_DIM].astype(F32)
            cu = gate_c * uu
            ext = jnp.concatenate([hc_ref[:, sl].astype(F32) * hu_ref[:, sl].astype(F32) * has_prev, cu], axis=0)
            c1 = pltpu.roll(ext, 1, 0)[HALO:]
            c2 = pltpu.roll(ext, 2, 0)[HALO:]
            w0, w1, w2 = cw_ref[h, 0:1, :], cw_ref[h, 1:2, :], cw_ref[h, 2:3, :]
            conv = w2 * cu + w1 * c1 + w0 * c2
            ya = gate_b * conv
            ra = _rsq(ya)
            dya = _norm_bwd(dmix_ref[:, sl], ya * ra, ra)
            dconv = dya * gate_b
            dcw_ref[h, 0:1, :] += jnp.sum(dconv * c2, axis=0, keepdims=True)
            dcw_ref[h, 1:2, :] += jnp.sum(dconv * c1, axis=0, keepdims=True)
            dcw_ref[h, 2:3, :] += jnp.sum(dconv * cu, axis=0, keepdims=True)
            extd = jnp.concatenate([dconv, dconv_carry[:, sl]], axis=0)
            d1 = pltpu.roll(extd, n_ext - 1, 0)[:tm]
            d2 = pltpu.roll(extd, n_ext - 2, 0)[:tm]
            dcu = w2 * dconv + w1 * d1 + w0 * d2
            dconv_carry[:, sl] = dconv[:HALO]
            dproj_ref[:, lo:lo + HEAD_DIM] = (dya * conv).astype(BF16)
            dproj_ref[:, CONV_WIDTH + lo:CONV_WIDTH + lo + HEAD_DIM] = (dcu * uu).astype(BF16)
            dproj_ref[:, 2 * CONV_WIDTH + lo:2 * CONV_WIDTH + lo + HEAD_DIM] = (dcu * gate_c).astype(BF16)

        for gi, w in enumerate(POOL_WINDOWS):
            lo = gi * POOL_GROUP_DIM
            sl = slice(lo, lo + POOL_GROUP_DIM)
            v = proj_ref[:, 3 * CONV_WIDTH + lo:3 * CONV_WIDTH + lo + POOL_GROUP_DIM].astype(F32)
            inv = _inv_count(tile * tm, tm, w)
            ext = jnp.concatenate([hv_ref[:, sl].astype(F32) * has_prev, v], axis=0)
            pooled = (_window_sum(ext, w, True)[HALO:] * inv - v).astype(BF16)
            y = jnp.dot(pooled, pw_ref[gi], preferred_element_type=F32)
            rp = _rsq(y)
            nb_ = y * rp
            dyb = dmix_ref[:, CONV_WIDTH + lo:CONV_WIDTH + lo + POOL_GROUP_DIM]
            dps_ref[:, sl] += jnp.sum(dyb * nb_, axis=0, keepdims=True)
            dy = _norm_bwd(dyb * ps_ref[:, sl], nb_, rp).astype(BF16)
            dpw_ref[gi] += lax.dot_general(pooled, dy, TN_DIMS, preferred_element_type=F32)
            dpooled = lax.dot_general(dy, pw_ref[gi], NT_DIMS, preferred_element_type=F32)
            q = dpooled * inv
            extq = jnp.concatenate([q, q_carry[:, sl]], axis=0)
            dv = _window_sum(extq, w, False)[:tm] - dpooled
            q_carry[:, sl] = q[:HALO]
            dproj_ref[:, 3 * CONV_WIDTH + lo:3 * CONV_WIDTH + lo + POOL_GROUP_DIM] = dv.astype(BF16)

    rev = lambda n: pl.BlockSpec((tm, n), lambda i: (nt - 1 - i, 0))

    def halo(col):
        return pl.BlockSpec((HALO, CONV_WIDTH), lambda i: (jnp.maximum((nt - 1 - i) * hb - 1, 0), col))

    return _host_call(
        body, name="mixer_bwd", grid=(nt,), comm=comm, args=(dx1, mo, proj, proj, proj, proj, cw, pw, ps, wout, g2),
        in_specs=[rev(D), rev(D), rev(P), halo(1), halo(2), halo(3), _whole(cw.shape), _whole(pw.shape),
                  _whole(ps.shape), _whole(wout.shape), _whole(g2.shape)],
        out_specs=[rev(D), rev(P), pl.BlockSpec((1, D), lambda i: (0, 0)),
                   pl.BlockSpec(cw.shape, lambda i: (0, 0, 0)), pl.BlockSpec(ps.shape, lambda i: (0, 0)),
                   pl.BlockSpec(pw.shape, lambda i: (0, 0, 0))],
        out_shape=[jax.ShapeDtypeStruct((T, D), BF16), jax.ShapeDtypeStruct((T, P), BF16),
                   jax.ShapeDtypeStruct((1, D), F32), jax.ShapeDtypeStruct(cw.shape, F32),
                   jax.ShapeDtypeStruct(ps.shape, F32), jax.ShapeDtypeStruct(pw.shape, F32)],
        scratch_shapes=[pltpu.VMEM((tm, D), F32), pltpu.VMEM((HALO, CONV_WIDTH), F32),
                        pltpu.VMEM((HALO, CONV_WIDTH), F32)])


def _inproj_bwd(dproj, win, x, dx1, g1, comm, tm=256):
    T, D = x.shape
    nblk, _, bn = win.shape

    def body(dp_ref, w_ref, x_ref, dx1_ref, g1_ref, gx_ref, dg1_ref):
        @pl.when(pl.program_id(0) == 0)
        def _():
            dg1_ref[...] = jnp.zeros_like(dg1_ref)

        dh = lax.dot_general(dp_ref[:, 0:bn], w_ref[0], NT_DIMS, preferred_element_type=F32)
        for k in range(1, nblk):
            dh = dh + lax.dot_general(dp_ref[:, k * bn:(k + 1) * bn], w_ref[k], NT_DIMS,
                                      preferred_element_type=F32)
        xv = x_ref[...]
        r = _rsq(xv)
        n = xv * r
        dg1_ref[...] += jnp.sum(dh * n, axis=0, keepdims=True)
        gx_ref[...] = dx1_ref[...] + _norm_bwd(dh * g1_ref[...], n, r)

    row = pl.BlockSpec((tm, D), lambda i: (i, 0))
    vec = pl.BlockSpec((1, D), lambda i: (0, 0))
    return _host_call(
        body, name="inproj_bwd", grid=(T // tm,), comm=comm, args=(dproj, win, x, dx1, g1),
        in_specs=[pl.BlockSpec((tm, nblk * bn), lambda i: (i, 0)), _whole(win.shape), row, row, vec],
        out_specs=[row, vec],
        out_shape=[jax.ShapeDtypeStruct((T, D), F32), jax.ShapeDtypeStruct((1, D), F32)])


def _adamw(w, g, m, v):
    m = ADAM_B1 * m + (1.0 - ADAM_B1) * g
    v = ADAM_B2 * v + (1.0 - ADAM_B2) * jnp.square(g)
    m_hat = m / (1.0 - ADAM_B1 ** ADAM_STEP)
    v_hat = v / (1.0 - ADAM_B2 ** ADAM_STEP)
    delta = -ADAM_LR * (m_hat / (jnp.sqrt(v_hat) + ADAM_EPS) + ADAM_WD * w)
    return delta, m, v


def _sum_adamw(parts, w, m, v, name, tr):
    r, cd = w.shape

    def body(p_ref, w_ref, m_ref, v_ref, g_ref, d_ref, mo_ref, vo_ref):
        g = p_ref[0].astype(F32)
        for k in range(1, NDEV):
            g = g + p_ref[k].astype(F32)
        g_ref[...] = g
        d_ref[...], mo_ref[...], vo_ref[...] = _adamw(w_ref[...], g, m_ref[...], v_ref[...])

    blk = pl.BlockSpec((tr, cd), lambda i: (i, 0))
    shp = jax.ShapeDtypeStruct((r, cd), F32)
    return pl.pallas_call(
        body, name=name, grid=(r // tr,),
        in_specs=[pl.BlockSpec((NDEV, tr, cd), lambda i: (0, i, 0)), blk, blk, blk],
        out_specs=[blk] * 4, out_shape=[shp] * 4,
        compiler_params=_params("arbitrary"),
    )(parts, w, m, v)


def _small_reduce_adamw(loss_part, vec_grads, dps, dcw, dpw, vec_state, ps_state, cw_state, pw_state):
    D = vec_grads[0].shape[1]
    pw_rows = pw_state[0].shape[1]
    states = list(vec_state) + [ps_state, cw_state, pw_state]
    n_in = 1 + 4 + 3 + 3 * len(states)
    n_out = 1 + 4 * len(states)

    def body(*refs):
        hbm_in, hbm_out, rest = refs[:n_in], refs[n_in:n_in + n_out], refs[n_in + n_out:]
        vin, vout = rest[:n_in], rest[n_in:n_in + n_out]
        pack, gat, cbuf, pbuf, send_sems, recv_sems, local_sems, in_sems, out_sems = rest[n_in + n_out:]
        loads = [pltpu.make_async_copy(hbm_in[k], vin[k], in_sems.at[k]) for k in range(n_in)]
        for cp in loads:
            cp.start()
        for cp in loads:
            cp.wait()
        loss_ref, dg = vin[0], vin[1:5]
        dps_ref, dcw_ref, dpw_ref = vin[5:8]
        st = vin[8:n_in]
        loss_out, outs = vout[0], vout[1:n_out]
        x, y, c = _coords()
        me = _device_index((x, y, c))

        pack[...] = jnp.zeros_like(pack)
        for k in range(4):
            pack[k:k + 1, :] = dg[k][...]
        pack[4:5, 0:dps_ref.shape[1]] = dps_ref[...]
        pack[5:6, 0:loss_ref.shape[1]] = loss_ref[...]

        def pw_slice(i):
            return dpw_ref.at[:, pl.ds(i * pw_rows, pw_rows), :]

        mine = [pltpu.make_async_copy(pack, gat.at[me], local_sems.at[0]),
                pltpu.make_async_copy(dcw_ref.at[me], cbuf.at[me], local_sems.at[1]),
                pltpu.make_async_copy(pw_slice(me), pbuf.at[me], local_sems.at[2])]
        for cp in mine:
            cp.start()
        sends, recvs = [], []
        for mask in range(1, NDEV):
            peer = (1 - x if mask & 4 else x, 1 - y if mask & 2 else y, 1 - c if mask & 1 else c)
            p = _device_index(peer)
            for k, (src, buf) in enumerate(((pack, gat), (dcw_ref.at[p], cbuf), (pw_slice(p), pbuf))):
                kw = dict(send_sem=send_sems.at[mask, k], recv_sem=recv_sems.at[mask, k],
                          device_id=peer, device_id_type=MESH)
                sends.append(pltpu.make_async_remote_copy(src_ref=src, dst_ref=buf.at[me], **kw))
                recvs.append(pltpu.make_async_remote_copy(src_ref=src, dst_ref=buf.at[p], **kw))
                sends[-1].start()
        for cp in recvs:
            cp.wait_recv()
        for cp in sends:
            cp.wait_send()
        for cp in mine:
            cp.wait()

        def slot_sum(buf):
            s = buf[0]
            for k in range(1, NDEV):
                s = s + buf[k]
            return s

        vec = slot_sum(gat)
        loss_out[...] = vec[5:6, 0:loss_ref.shape[1]]
        grads = [vec[k:k + 1, :] for k in range(4)] + [vec[4:5, 0:dps_ref.shape[1]], slot_sum(cbuf), slot_sum(pbuf)]
        for k, g in enumerate(grads):
            w_ref, m_ref, v_ref = st[3 * k:3 * k + 3]
            outs[4 * k][...] = g
            outs[4 * k + 1][...], outs[4 * k + 2][...], outs[4 * k + 3][...] = _adamw(
                w_ref[...], g, m_ref[...], v_ref[...])
        stores = [pltpu.make_async_copy(vout[k], hbm_out[k], out_sems.at[k]) for k in range(n_out)]
        for cp in stores:
            cp.start()
        for cp in stores:
            cp.wait()

    flat_state = [a for s in states for a in s]
    operands = [loss_part, *vec_grads, dps, dcw, dpw, *flat_state]
    out_shape = [jax.ShapeDtypeStruct(loss_part.shape, F32)]
    out_shape += [jax.ShapeDtypeStruct(s[0].shape, F32) for s in states for _ in range(4)]
    return pl.pallas_call(
        body, name="small_reduce_adamw",
        in_specs=[HBM_SPEC] * n_in, out_specs=[HBM_SPEC] * n_out, out_shape=out_shape,
        scratch_shapes=[pltpu.VMEM(a.shape, F32) for a in operands] + [pltpu.VMEM(s.shape, F32) for s in out_shape]
        + [pltpu.VMEM((NDEV, D), F32), pltpu.VMEM((NDEV, NDEV, D), F32),
           pltpu.VMEM((NDEV,) + cw_state[0].shape, F32), pltpu.VMEM((NDEV,) + pw_state[0].shape, F32),
           pltpu.SemaphoreType.DMA((NDEV, 3)), pltpu.SemaphoreType.DMA((NDEV, 3)),
           pltpu.SemaphoreType.DMA((3,)), pltpu.SemaphoreType.DMA((n_in,)), pltpu.SemaphoreType.DMA((n_out,))],
        compiler_params=pltpu.CompilerParams(vmem_limit_bytes=VMEM_LIMIT),
    )(*operands)


ROW_TILE = dict(w_in=512, w_gate=176, w_up=176, w_down=176, w_out=128)
FORWARD_STEP = dict(inproj=56, mixer_fwd=26, ffn_up=32)


def kernel(x, ln_mix_pre, w_in, conv_w, pool_w, pool_scale, w_out, ln_mix_post, ln_ffn_pre, w_gate, w_up, w_down, ln_ffn_post, loss_target, m_ln_mix_pre, m_w_in, m_conv_w, m_pool_w, m_pool_scale, m_w_out, m_ln_mix_post, m_ln_ffn_pre, m_w_gate, m_w_up, m_w_down, m_ln_ffn_post, v_ln_mix_pre, v_w_in, v_conv_w, v_pool_w, v_pool_scale, v_w_out, v_ln_mix_post, v_ln_ffn_pre, v_w_gate, v_w_up, v_w_down, v_ln_ffn_post):
    D = x.shape[2]
    xs, tgt = x[0], loss_target[0]
    flip = lambda arr: jnp.swapaxes(arr[0], 0, 1)
    win, wg, wu, wd, wout, pw, cw = _cast_gather_first(
        [w_in[0], flip(w_gate), flip(w_up), w_down[0], w_out[0], pool_w[0], conv_w[0]])

    (proj, h), (wout, wg) = _inproj(xs, ln_mix_pre, win, _gather_comm([wout, wg], FORWARD_STEP["inproj"]))
    wout2 = wout.reshape(D, D)
    (x1, hf, mixed, mo), (wu,) = _mixer_fwd(proj, xs, cw, pw, pool_scale, wout2, ln_mix_post, ln_ffn_pre,
                                            _gather_comm([wu], FORWARD_STEP["mixer_fwd"]))
    (g, u, a), (wd,) = _ffn_up(hf, wg, wu, _gather_comm([wd], FORWARD_STEP["ffn_up"]))
    dy, dff, loss, dg4 = _ffn_down_loss(a, wd, x1, tgt, ln_ffn_post)

    rows = lambda arr, lo, hi: (lo * arr.shape[1] // 4, (hi - lo) * arr.shape[1] // 4)
    dwd, _ = _wgrad_rows("wgrad_down", a, dff)
    (dg, du), (dwd_parts,) = _ffn_bwd_act(dff, wd, g, u, _scatter_comm([(dwd, None, *rows(dwd, 0, 2))]))
    dwg, (dwd_parts,) = _wgrad_rows("wgrad_gate", dg, hf, _scatter_comm([(dwd, dwd_parts, *rows(dwd, 2, 4))]))
    dwu, (dwg_parts,) = _wgrad_rows("wgrad_up", du, hf, _scatter_comm([(dwg, None, *rows(dwg, 0, 2))]))
    (dx1, dg3), (dwg_parts,), (dwu_parts,) = _ffn_bwd_in(
        dg, du, wg, wu, x1, dy, ln_ffn_pre, _scatter_comm([(dwg, dwg_parts, *rows(dwg, 2, 4))]),
        _scatter_comm([(dwu, None, *rows(dwu, 0, 2))]))
    (dmo, dproj, dg2, dcw, dps, dpw), (dwu_parts,) = _mixer_bwd(
        dx1, mo, proj, cw, pw, pool_scale, wout2, ln_mix_post, _scatter_comm([(dwu, dwu_parts, *rows(dwu, 2, 4))]))
    dwout, _ = _wgrad_out(mixed, dmo, None)
    dwin, (dwout_parts,) = _wgrad_in(h, dproj, _scatter_comm([(dwout, None, *rows(dwout, 0, 4))]))
    (gx, dg1), (dwin_parts,) = _inproj_bwd(dproj, win, xs, dx1, ln_mix_pre,
                                           _scatter_comm([(dwin, None, *rows(dwin, 0, 4))]))

    res = {}
    for k, parts, w, m, v in (("w_down", dwd_parts, w_down, m_w_down, v_w_down),
                              ("w_out", dwout_parts, w_out, m_w_out, v_w_out),
                              ("w_in", dwin_parts, w_in, m_w_in, v_w_in)):
        res[k] = [o.reshape(w.shape) for o in _sum_adamw(parts, w[0], m[0], v[0], "sum_adamw_" + k, ROW_TILE[k])]
    for k, parts, w, m, v in (("w_gate", dwg_parts, w_gate, m_w_gate, v_w_gate),
                              ("w_up", dwu_parts, w_up, m_w_up, v_w_up)):
        outs = _sum_adamw(parts, flip(w), flip(m), flip(v), "sum_adamw_" + k, ROW_TILE[k])
        res[k] = [jnp.swapaxes(o, 0, 1)[None] for o in outs]

    loss_sum, *small = _small_reduce_adamw(
        loss, [dg1, dg2, dg3, dg4], dps, dcw, dpw,
        [(ln_mix_pre, m_ln_mix_pre, v_ln_mix_pre), (ln_mix_post, m_ln_mix_post, v_ln_mix_post),
         (ln_ffn_pre, m_ln_ffn_pre, v_ln_ffn_pre), (ln_ffn_post, m_ln_ffn_post, v_ln_ffn_post)],
        (pool_scale, m_pool_scale, v_pool_scale), (conv_w[0], m_conv_w[0], v_conv_w[0]),
        (pool_w[0], m_pool_w[0], v_pool_w[0]))
    small_names = ["ln_mix_pre", "ln_mix_post", "ln_ffn_pre", "ln_ffn_post", "pool_scale", "conv_w", "pool_w"]
    shapes = dict(conv_w=conv_w.shape, pool_w=pool_w.shape)
    for i, k in enumerate(small_names):
        res[k] = [o.reshape(shapes[k]) if k in shapes else o for o in small[4 * i:4 * i + 4]]

    order = ["ln_mix_pre", "w_in", "conv_w", "pool_w", "pool_scale", "w_out", "ln_mix_post", "ln_ffn_pre",
             "w_gate", "w_up", "w_down", "ln_ffn_post"]
    return (loss_sum[0, 0], gx[None], *[res[k][0] for k in order], *[res[k][1] for k in order],
            *[res[k][2] for k in order], *[res[k][3] for k in order])
```
